```python
import math
import jax, jax.numpy as jnp
from jax import lax
import numpy as np

D_MODEL = 1024
BATCH = 1
SEQ = 16384
DEPTH = 2

CHUNK = 64
Q_BLOCK = 128
ROPE_THETA = 500000.0
NORM_EPS = 1e-6
NEG_INF = -1e30

N_EVEN = (DEPTH + 1) // 2
N_ODD = DEPTH // 2

H_A = 8
Q_LORA = 256
KV_LORA = 128
NOPE_A = 64
ROPE_A = 32
V_A = 64
H_B = 8
DH_B = 64
ROT_B = DH_B // 4
H_IDX = 8
D_IDX = 32
ROT_IDX = D_IDX // 4
TOPK_MAX = 256
EVEN_SPLIT_SIZES = (Q_LORA, KV_LORA, ROPE_A, H_B * DH_B, H_B * DH_B, H_B * DH_B, H_IDX * D_IDX, D_IDX, H_IDX)
IN_EVEN = sum(EVEN_SPLIT_SIZES)
MIX_EVEN = H_A * V_A + H_B * DH_B
H_C = 8
DH_C = 64
ROT_C = DH_C // 4
IN_ODD = 3 * H_C * 2 * DH_C
MIX_ODD = H_C * 2 * DH_C
D_FF = 2816
N_EXP = 8
TOP_E = 2
D_FF_E = 3584

kernel_name = 'hybrid_mla_dsa_diffattn_moe_streaming'


def rms_norm(x, g):
    xf = x.astype(jnp.float32)
    y = xf * lax.rsqrt(jnp.mean(jnp.square(xf), axis=-1, keepdims=True) + NORM_EPS)
    return (y * g.astype(jnp.float32)).astype(x.dtype)


def rope_table(seq_len, rot_dim):
    pos = jnp.arange(seq_len, dtype=jnp.float32)
    inv_freq = ROPE_THETA ** (-jnp.arange(0, rot_dim, 2, dtype=jnp.float32) / rot_dim)
    ang = pos[:, None] * inv_freq[None, :]
    return jnp.cos(ang), jnp.sin(ang)


def apply_rope(x, rot_dim):
    S = x.shape[1]
    cos, sin = rope_table(S, rot_dim)
    shp = (1, S) + (1,) * (x.ndim - 3) + (rot_dim // 2,)
    cos = cos.reshape(shp)
    sin = sin.reshape(shp)
    xr = x[..., :rot_dim].astype(jnp.float32)
    x1, x2 = xr[..., :rot_dim // 2], xr[..., rot_dim // 2:]
    rot = jnp.concatenate([x1 * cos - x2 * sin, x2 * cos + x1 * sin], axis=-1).astype(x.dtype)
    return jnp.concatenate([rot, x[..., rot_dim:]], axis=-1)


def chunk_visible(q0, n_keys):
    q_chunk = (q0 + jnp.arange(Q_BLOCK)) // CHUNK
    k_chunk = jnp.arange(n_keys) // CHUNK
    return k_chunk[None, :] <= q_chunk[:, None]


def q_slice(t, q0):
    return lax.dynamic_slice_in_dim(t, q0, Q_BLOCK, axis=1)


def sweep_query_blocks(block_fn, seq_len):
    starts = jnp.arange(seq_len // Q_BLOCK) * Q_BLOCK
    out = lax.map(block_fn, starts)
    out = jnp.moveaxis(out, 0, 1)
    return out.reshape((out.shape[0], seq_len) + out.shape[3:])


def mla_mixer(c_q, c_kv, k_rope, g_q_lat, w_uq, g_kv_lat, w_ukv):
    B, S, _ = c_q.shape
    q = (rms_norm(c_q, g_q_lat) @ w_uq).reshape(B, S, H_A, NOPE_A + ROPE_A)
    q_nope = q[..., :NOPE_A]
    q_pe = apply_rope(q[..., NOPE_A:], ROPE_A)
    kv = (rms_norm(c_kv, g_kv_lat) @ w_ukv).reshape(B, S, H_A, NOPE_A + V_A)
    k_nope, v = kv[..., :NOPE_A], kv[..., NOPE_A:]
    k_pe = apply_rope(k_rope, ROPE_A)
    scale = (NOPE_A + ROPE_A) ** -0.5

    def block(q0):
        qn = q_slice(q_nope, q0)
        qp = q_slice(q_pe, q0)
        s = (jnp.einsum('bqhd,bkhd->bhqk', qn, k_nope, preferred_element_type=jnp.float32)
             + jnp.einsum('bqhr,bkr->bhqk', qp, k_pe, preferred_element_type=jnp.float32)) * scale
        s = jnp.where(chunk_visible(q0, S), s, NEG_INF)
        p = jax.nn.softmax(s, axis=-1).astype(v.dtype)
        return jnp.einsum('bhqk,bkhd->bqhd', p, v)

    return sweep_query_blocks(block, S)


def dsa_mixer(q, k, v, q_idx, k_idx, w_idx, g_idx_k):
    B, S = q.shape[:2]
    top_k = min(TOPK_MAX, S // 4)
    q = apply_rope(q, ROT_B)
    k = apply_rope(k, ROT_B)
    q_idx = apply_rope(q_idx, ROT_IDX)
    k_idx = apply_rope(rms_norm(k_idx, g_idx_k), ROT_IDX)
    w_idx = w_idx.astype(jnp.float32) * (H_IDX ** -0.5)
    idx_scale = D_IDX ** -0.5
    scale = DH_B ** -0.5
    gather = jax.vmap(lambda t, i: t[i])

    def block(q0):
        vis = chunk_visible(q0, S)[None]
        dots = jnp.einsum('bqhd,bkd->bqhk', q_slice(q_idx, q0), k_idx,
                          preferred_element_type=jnp.float32) * idx_scale
        score = jnp.einsum('bqh,bqhk->bqk', q_slice(w_idx, q0), jax.nn.relu(dots))
        score = jnp.where(vis, score, NEG_INF)
        _, sel = lax.top_k(score, top_k)
        sel_ok = jnp.take_along_axis(jnp.broadcast_to(vis, score.shape), sel, axis=-1)
        kg = gather(k, sel)
        vg = gather(v, sel)
        s = jnp.einsum('bqhd,bqkhd->bhqk', q_slice(q, q0), kg, preferred_element_type=jnp.float32) * scale
        s = jnp.where(sel_ok[:, None], s, NEG_INF)
        p = jax.nn.softmax(s, axis=-1).astype(v.dtype)
        return jnp.einsum('bhqk,bqkhd->bqhd', p, vg)

    return sweep_query_blocks(block, S)


def even_mixer(h, w_in, g_q_lat, w_uq, g_kv_lat, w_ukv, g_idx_k, w_out):
    B, S, _ = h.shape
    z = h @ w_in
    splits = np.cumsum(EVEN_SPLIT_SIZES)[:-1].tolist()
    c_q, c_kv, k_rope, q_b, k_b, v_b, q_i, k_i, w_i = jnp.split(z, splits, axis=-1)
    o_a = mla_mixer(c_q, c_kv, k_rope, g_q_lat, w_uq, g_kv_lat, w_ukv)
    o_b = dsa_mixer(q_b.reshape(B, S, H_B, DH_B), k_b.reshape(B, S, H_B, DH_B), v_b.reshape(B, S, H_B, DH_B),
                    q_i.reshape(B, S, H_IDX, D_IDX), k_i, w_i, g_idx_k)
    o = jnp.concatenate([o_a.reshape(B, S, H_A * V_A), o_b.reshape(B, S, H_B * DH_B)], axis=-1)
    return o @ w_out


def diff_mixer(h, w_qkv, lambda_q1, lambda_k1, lambda_q2, lambda_k2, g_sub, w_out, layer):
    B, S, _ = h.shape
    lambda_init = 0.8 - 0.6 * math.exp(-0.3 * layer)
    q, k, v = jnp.split(h @ w_qkv, 3, axis=-1)
    q = apply_rope(q.reshape(B, S, H_C, 2, DH_C), ROT_C)
    k = apply_rope(k.reshape(B, S, H_C, 2, DH_C), ROT_C)
    v = v.reshape(B, S, H_C, 2 * DH_C)
    f32 = jnp.float32
    lam = (jnp.exp(jnp.sum(lambda_q1.astype(f32) * lambda_k1.astype(f32)))
           - jnp.exp(jnp.sum(lambda_q2.astype(f32) * lambda_k2.astype(f32))) + lambda_init)
    scale = DH_C ** -0.5

    def block(q0):
        s = jnp.einsum('bqhcd,bkhcd->bhcqk', q_slice(q, q0), k, preferred_element_type=jnp.float32) * scale
        s = jnp.where(chunk_visible(q0, S), s, NEG_INF)
        p = jax.nn.softmax(s, axis=-1)
        a = (p[:, :, 0] - lam * p[:, :, 1]).astype(v.dtype)
        return jnp.einsum('bhqk,bkhd->bqhd', a, v)

    o = sweep_query_blocks(block, S)
    o = rms_norm(o, g_sub) * (1.0 - lambda_init)
    return o.reshape(B, S, MIX_ODD) @ w_out


def swiglu(h, w_gate, w_up, w_down):
    return (jax.nn.silu(h @ w_gate) * (h @ w_up)) @ w_down


def moe_ffn(h, w_router, w_gate, w_up, w_down):
    logits = (h @ w_router).astype(jnp.float32)
    top_logit, top_idx = lax.top_k(logits, TOP_E)
    gates = jax.nn.softmax(top_logit, axis=-1)
    combine = jnp.sum(jax.nn.one_hot(top_idx, N_EXP, dtype=jnp.float32) * gates[..., None], axis=-2)
    out = jnp.zeros_like(h)
    for e in range(N_EXP):
        out = out + combine[..., e:e + 1].astype(h.dtype) * swiglu(h, w_gate[e], w_up[e], w_down[e])
    return out


def setup_inputs(seed: int = 0) -> dict:
    key = jax.random.key(seed)
    keys = jax.random.split(key, 32)
    kl = [keys[i] for i in range(32)]
    f32 = jnp.float32

    def w(i, shape, fan_in):
        return jax.random.normal(kl[i], shape, f32) * fan_in ** -0.5

    def gain(i, shape):
        return 1.0 + 0.02 * jax.random.normal(kl[i], shape, f32)

    E, O = N_EVEN, N_ODD
    return {
        'x': jax.random.normal(kl[0], (BATCH, SEQ, D_MODEL), f32),
        'ev_norm_mix': gain(1, (E, D_MODEL)),
        'ev_w_in': w(2, (E, D_MODEL, IN_EVEN), D_MODEL),
        'ev_g_q_lat': gain(3, (E, Q_LORA)),
        'ev_w_uq': w(4, (E, Q_LORA, H_A * (NOPE_A + ROPE_A)), Q_LORA),
        'ev_g_kv_lat': gain(5, (E, KV_LORA)),
        'ev_w_ukv': w(6, (E, KV_LORA, H_A * (NOPE_A + V_A)), KV_LORA),
        'ev_g_idx_k': gain(7, (E, D_IDX)),
        'ev_w_out': w(8, (E, MIX_EVEN, D_MODEL), MIX_EVEN),
        'ev_norm_ffn': gain(9, (E, D_MODEL)),
        'ev_w_gate': w(10, (E, D_MODEL, D_FF), D_MODEL),
        'ev_w_up': w(11, (E, D_MODEL, D_FF), D_MODEL),
        'ev_w_down': w(12, (E, D_FF, D_MODEL), D_FF),
        'od_norm_mix': gain(13, (O, D_MODEL)),
        'od_w_qkv': w(14, (O, D_MODEL, IN_ODD), D_MODEL),
        'od_lambda_q1': 0.1 * jax.random.normal(kl[15], (O, DH_C), f32),
        'od_lambda_k1': 0.1 * jax.random.normal(kl[16], (O, DH_C), f32),
        'od_lambda_q2': 0.1 * jax.random.normal(kl[17], (O, DH_C), f32),
        'od_lambda_k2': 0.1 * jax.random.normal(kl[18], (O, DH_C), f32),
        'od_g_sub': gain(19, (O, 2 * DH_C)),
        'od_w_out': w(20, (O, MIX_ODD, D_MODEL), MIX_ODD),
        'od_norm_ffn': gain(21, (O, D_MODEL)),
        'od_w_router': w(22, (O, D_MODEL, N_EXP), D_MODEL),
        'od_w_gate_e': w(23, (O, N_EXP, D_MODEL, D_FF_E), D_MODEL),
        'od_w_up_e': w(24, (O, N_EXP, D_MODEL, D_FF_E), D_MODEL),
        'od_w_down_e': w(25, (O, N_EXP, D_FF_E, D_MODEL), D_FF_E),
        'final_norm': gain(26, (D_MODEL,)),
    }


def reference(x, ev_norm_mix, ev_w_in, ev_g_q_lat, ev_w_uq, ev_g_kv_lat, ev_w_ukv, ev_g_idx_k, ev_w_out,
              ev_norm_ffn, ev_w_gate, ev_w_up, ev_w_down,
              od_norm_mix, od_w_qkv, od_lambda_q1, od_lambda_k1, od_lambda_q2, od_lambda_k2, od_g_sub, od_w_out,
              od_norm_ffn, od_w_router, od_w_gate_e, od_w_up_e, od_w_down_e,
              final_norm):
    h = x
    for layer in range(DEPTH):
        i = layer // 2
        if layer % 2 == 0:
            h = h + even_mixer(rms_norm(h, ev_norm_mix[i]), ev_w_in[i], ev_g_q_lat[i], ev_w_uq[i],
                               ev_g_kv_lat[i], ev_w_ukv[i], ev_g_idx_k[i], ev_w_out[i])
            h = h + swiglu(rms_norm(h, ev_norm_ffn[i]), ev_w_gate[i], ev_w_up[i], ev_w_down[i])
        else:
            h = h + diff_mixer(rms_norm(h, od_norm_mix[i]), od_w_qkv[i], od_lambda_q1[i], od_lambda_k1[i],
                               od_lambda_q2[i], od_lambda_k2[i], od_g_sub[i], od_w_out[i], layer)
            h = h + moe_ffn(rms_norm(h, od_norm_ffn[i]), od_w_router[i], od_w_gate_e[i], od_w_up_e[i],
                            od_w_down_e[i])
    return rms_norm(h, final_norm)
```

```python
import functools
import math

import numpy as np
import jax
import jax.numpy as jnp
from jax import lax
from jax.experimental import pallas as pl
from jax.experimental.pallas import tpu as pltpu

f32 = jnp.float32
bf16 = jnp.bfloat16
i32 = jnp.int32

D_MODEL = 1024
CHUNK = 64
ROPE_THETA = 500000.0
NORM_EPS = 1e-6
NEG_INF = -1e30

H_A, Q_LORA, KV_LORA, NOPE_A, ROPE_A, V_A = 8, 256, 128, 64, 32, 64
H_B, DH_B, ROT_B = 8, 64, 16
H_IDX, D_IDX, ROT_IDX = 8, 32, 8
TOPK_MAX = 256
H_C, DH_C, ROT_C = 8, 64, 16
D_FF, N_EXP, D_FF_E = 2816, 8, 3584

LANES = 128
HEAD_PAD_A = 128

_NEG_BITS = int(np.float32(NEG_INF).view(np.int32))
NEG_KEY = _NEG_BITS ^ 0x7FFFFFFF
INT_MIN = -(2 ** 31)

VMEM_LIMIT = 56 * 1024 * 1024


def _cparams(sem):
    return pltpu.CompilerParams(dimension_semantics=sem, vmem_limit_bytes=VMEM_LIMIT)


def _rms(x, g):
    var = jnp.mean(x * x, axis=-1, keepdims=True)
    return x * lax.rsqrt(var + NORM_EPS) * g


def _dot(a, b):
    return jnp.dot(a, b, preferred_element_type=f32)


def _dot_nt(a, b):
    return lax.dot_general(a, b, (((1,), (1,)), ((), ())), preferred_element_type=f32)


def _rope_tables(seq, rot_dim, head_width, offset, width):
    pos = jnp.arange(seq, dtype=f32)
    inv_freq = ROPE_THETA ** (-jnp.arange(0, rot_dim, 2, dtype=f32) / rot_dim)
    ang = pos[:, None] * inv_freq[None, :]
    cos, sin = jnp.cos(ang), jnp.sin(ang)
    c = jnp.ones((seq, head_width), f32).at[:, offset:offset + rot_dim].set(jnp.concatenate([cos, cos], -1))
    s = jnp.zeros((seq, head_width), f32).at[:, offset:offset + rot_dim].set(jnp.concatenate([-sin, sin], -1))
    reps = width // head_width
    return jnp.tile(c, (1, reps)), jnp.tile(s, (1, reps))


def _swap_cols(w, head_width, offset, rot_dim):
    k, n = w.shape
    half = rot_dim // 2
    w3 = w.reshape(k, n // head_width, head_width)
    out = jnp.zeros_like(w3)
    out = out.at[:, :, offset:offset + half].set(w3[:, :, offset + half:offset + rot_dim])
    out = out.at[:, :, offset + half:offset + rot_dim].set(w3[:, :, offset:offset + half])
    return out.reshape(k, n)


def _pad_cols(w, width):
    return jnp.pad(w, ((0, 0), (0, width - w.shape[1])))


def _rows_call(body, seq, tm, row_ins, const_ins, out_sds, name):
    def rspec(a):
        return pl.BlockSpec((tm, a.shape[1]), lambda i: (i, 0))

    def cspec(a):
        nd = a.ndim
        return pl.BlockSpec(a.shape, lambda i: (0,) * nd)

    return pl.pallas_call(
        body,
        grid=(seq // tm,),
        in_specs=[rspec(a) for a in row_ins] + [cspec(a) for a in const_ins],
        out_specs=[pl.BlockSpec((tm, o.shape[1]), lambda i: (i, 0)) for o in out_sds],
        out_shape=out_sds,
        compiler_params=_cparams(("parallel",)),
        name=name,
    )(*row_ins, *const_ins)


def _even_proj_body(x_ref, ca_ref, sa_ref, cb_ref, sb_ref, ci_ref, si_ref, ckr_ref, skr_ref,
                    g_ref, gq_ref, gkv_ref, gi_ref, gisw_ref,
                    wlat_ref, wq_ref, wqsw_ref, wk_ref, wv_ref, place_ref,
                    wqb_ref, wqbsw_ref, wkb_ref, wkbsw_ref, wvb_ref, wqi_ref, wqisw_ref, wsm_ref, wwi_ref,
                    qa_ref, ka_ref, va_ref, qb_ref, kb_ref, vb_ref, qi_ref, ki_ref, wi_ref):
    xn = _rms(x_ref[...], g_ref[...]).astype(bf16)
    lat = _dot(xn, wlat_ref[...])
    cqn = _rms(lat[:, :Q_LORA], gq_ref[...]).astype(bf16)
    ckvn = _rms(lat[:, Q_LORA:], gkv_ref[...]).astype(bf16)
    reps_a = qa_ref.shape[1] // LANES
    ca = jnp.tile(ca_ref[...], (1, reps_a))
    sa = jnp.tile(sa_ref[...], (1, reps_a))
    qa_ref[...] = (_dot(cqn, wq_ref[...]) * ca + _dot(cqn, wqsw_ref[...]) * sa).astype(bf16)
    small = _dot(xn, wsm_ref[...])
    kr, kr_sw = small[:, 0:ROPE_A], small[:, ROPE_A:2 * ROPE_A]
    kpe = (kr * ckr_ref[...] + kr_sw * skr_ref[...]).astype(bf16)
    ka_ref[...] = (_dot(ckvn, wk_ref[...]) + _dot(kpe, place_ref[...])).astype(bf16)
    va_ref[...] = _dot(ckvn, wv_ref[...]).astype(bf16)
    reps_b = qb_ref.shape[1] // LANES
    cb = jnp.tile(cb_ref[...], (1, reps_b))
    sb = jnp.tile(sb_ref[...], (1, reps_b))
    qb_ref[...] = (_dot(xn, wqb_ref[...]) * cb + _dot(xn, wqbsw_ref[...]) * sb).astype(bf16)
    kb_ref[...] = (_dot(xn, wkb_ref[...]) * cb + _dot(xn, wkbsw_ref[...]) * sb).astype(bf16)
    vb_ref[...] = _dot(xn, wvb_ref[...]).astype(bf16)
    reps_i = qi_ref.shape[1] // LANES
    ci = jnp.tile(ci_ref[...], (1, reps_i))
    si = jnp.tile(si_ref[...], (1, reps_i))
    qi_ref[...] = (_dot(xn, wqi_ref[...]) * ci + _dot(xn, wqisw_ref[...]) * si).astype(bf16)
    ki, ki_sw = small[:, 2 * ROPE_A:2 * ROPE_A + D_IDX], small[:, 2 * ROPE_A + D_IDX:2 * ROPE_A + 2 * D_IDX]
    r = lax.rsqrt(jnp.mean(ki * ki, axis=-1, keepdims=True) + NORM_EPS)
    ci32, si32 = ci_ref[:, 0:D_IDX], si_ref[:, 0:D_IDX]
    ki_ref[...] = (ki * r * gi_ref[...] * ci32 + ki_sw * r * gisw_ref[...] * si32).astype(bf16)
    wi_ref[...] = _dot(xn, wwi_ref[...])


def _odd_proj_body(x_ref, cb_ref, sb_ref, g_ref, wq_ref, wqsw_ref, wk_ref, wksw_ref, wv_ref,
                   q_ref, k_ref, v_ref):
    xn = _rms(x_ref[...], g_ref[...]).astype(bf16)
    reps = q_ref.shape[1] // LANES
    cb = jnp.tile(cb_ref[...], (1, reps))
    sb = jnp.tile(sb_ref[...], (1, reps))
    q_ref[...] = (_dot(xn, wq_ref[...]) * cb + _dot(xn, wqsw_ref[...]) * sb).astype(bf16)
    k_ref[...] = (_dot(xn, wk_ref[...]) * cb + _dot(xn, wksw_ref[...]) * sb).astype(bf16)
    v_ref[...] = _dot(xn, wv_ref[...]).astype(bf16)


def _out_proj2_body(x_ref, a1_ref, a2_ref, w1_ref, w2_ref, o_ref):
    o_ref[...] = x_ref[...] + _dot(a1_ref[...], w1_ref[...]) + _dot(a2_ref[...], w2_ref[...])


def _out_proj1_body(x_ref, a_ref, w_ref, o_ref):
    o_ref[...] = x_ref[...] + _dot(a_ref[...], w_ref[...])


def _router_body(x_ref, g_ref, whi_ref, wlo_ref, comb_ref):
    xn = _rms(x_ref[...], g_ref[...])
    hi = xn.astype(bf16)
    lo = (xn - hi.astype(f32)).astype(bf16)
    logits = _dot(hi, whi_ref[...]) + _dot(lo, whi_ref[...]) + _dot(hi, wlo_ref[...])
    lane = lax.broadcasted_iota(i32, logits.shape, 1).astype(f32)
    lg = jnp.where(lane < N_EXP, logits, -jnp.inf)
    m1 = jnp.max(lg, axis=1, keepdims=True)
    i1 = jnp.min(jnp.where(lg == m1, lane, float(LANES)), axis=1, keepdims=True)
    lg2 = jnp.where(lane == i1, -jnp.inf, lg)
    m2 = jnp.max(lg2, axis=1, keepdims=True)
    i2 = jnp.min(jnp.where(lg2 == m2, lane, float(LANES)), axis=1, keepdims=True)
    e2 = jnp.exp(m2 - m1)
    den = 1.0 + e2
    comb_ref[...] = jnp.where(lane == i1, 1.0 / den, 0.0) + jnp.where(lane == i2, e2 / den, 0.0)


def _dsa_select_body(qi_ref, wi_ref, kit_ref, out_ref, keys_scr, x_scr, *, tq, tk, top_k, idx_bits):
    q0 = pl.program_id(0) * tq
    n_kt = (q0 + tq + tk - 1) // tk
    qh = [qi_ref[:, h * D_IDX:(h + 1) * D_IDX] for h in range(H_IDX)]
    w = wi_ref[...]
    wb = [jnp.broadcast_to(w[:, h:h + 1], (tq, tk)) for h in range(H_IDX)]
    row = q0 + lax.broadcasted_iota(i32, (tq, 1), 0)
    row_lim = (row // CHUNK + 1) * CHUNK

    def cols_of(kt):
        c0 = pl.multiple_of(kt * tk, tk)
        return c0, c0 + lax.broadcasted_iota(i32, (tq, tk), 1)

    def score_tile(kt, carry):
        c0, col = cols_of(kt)
        kt_tile = kit_ref[:, pl.ds(c0, tk)]
        acc = jnp.zeros((tq, tk), f32)
        for h in range(H_IDX):
            acc = acc + jnp.maximum(_dot(qh[h], kt_tile), 0.0) * wb[h]
        sc = jnp.where(col < row_lim, acc, NEG_INF)
        bits = lax.bitcast_convert_type(sc, i32)
        keys_scr[:, pl.ds(c0, tk)] = jnp.where(bits < 0, bits ^ 0x7FFFFFFF, bits)
        return carry

    lax.fori_loop(0, n_kt, score_tile, 0)

    def count(pred):
        def body(kt, acc):
            c0, col = cols_of(kt)
            m = pred(keys_scr[:, pl.ds(c0, tk)], col).astype(i32)
            for u in range(tk // LANES):
                acc = acc + m[:, u * LANES:(u + 1) * LANES]
            return acc

        acc = lax.fori_loop(0, n_kt, body, jnp.zeros((tq, LANES), i32))
        return jnp.sum(acc.astype(f32), axis=1, keepdims=True).astype(i32)

    def thr_step(it, u):
        cand_u = u | lax.shift_left(jnp.int32(1), 31 - it)
        cand = cand_u ^ INT_MIN
        cnt = count(lambda ks, col: ks >= cand)
        return jnp.where(cnt >= top_k, cand_u, u)

    thr = lax.fori_loop(0, 32, thr_step, jnp.zeros((tq, 1), i32)) ^ INT_MIN
    n_gt = count(lambda ks, col: ks > thr)
    n_ge = count(lambda ks, col: ks >= thr)
    need = top_k - n_gt
    excess = jnp.logical_and(n_ge > top_k, thr > NEG_KEY)
    x_scr[...] = jnp.full(x_scr.shape, 2 ** 30, i32)

    @pl.when(jnp.max(excess.astype(f32)) > 0.0)
    def _():
        def tie_step(it, xv):
            cand = xv | lax.shift_left(jnp.int32(1), idx_bits - 1 - it)
            cnt = count(lambda ks, col: jnp.logical_and(ks == thr, col < cand))
            return jnp.where(cnt < need, cand, xv)

        xv = lax.fori_loop(0, idx_bits, tie_step, jnp.zeros((tq, 1), i32))
        x_scr[...] = jnp.broadcast_to(xv, x_scr.shape)

    xlim = x_scr[:, 0:1]
    out_ref[...] = jnp.full(out_ref.shape, NEG_INF, bf16)

    def write_tile(kt, carry):
        c0, col = cols_of(kt)
        ks = keys_scr[:, pl.ds(c0, tk)]
        sel = jnp.logical_or(ks > thr, jnp.logical_and(ks == thr, col <= xlim))
        sel = jnp.logical_and(sel, col < row_lim)
        out_ref[:, pl.ds(c0, tk)] = jnp.where(sel, 0.0, NEG_INF).astype(bf16)
        return carry

    lax.fori_loop(0, n_kt, write_tile, 0)


def _dsa_select(qi, wi, kit, seq, top_k):
    tq, tk = 128, 512
    idx_bits = max(1, int(math.ceil(math.log2(seq))))
    body = functools.partial(_dsa_select_body, tq=tq, tk=tk, top_k=top_k, idx_bits=idx_bits)
    return pl.pallas_call(
        body,
        grid=(seq // tq,),
        in_specs=[pl.BlockSpec((tq, qi.shape[1]), lambda i: (i, 0)),
                  pl.BlockSpec((tq, wi.shape[1]), lambda i: (i, 0)),
                  pl.BlockSpec(kit.shape, lambda i: (0, 0))],
        out_specs=pl.BlockSpec((tq, seq), lambda i: (i, 0)),
        out_shape=jax.ShapeDtypeStruct((seq, seq), bf16),
        scratch_shapes=[pltpu.VMEM((tq, seq), i32), pltpu.VMEM((tq, LANES), i32)],
        compiler_params=_cparams(("parallel",)),
        name="dsa_select",
    )(qi, wi, kit)


def _flash_body(it_ref, jt_ref, q_ref, k_ref, v_ref, *rest, t, n_heads, dq, dv, v_group, has_bias, diff,
                lambda_init):
    rest = list(rest)
    bias_ref = rest.pop(0) if has_bias else None
    if diff:
        lq1_ref, lk1_ref, lq2_ref, lk2_ref, gsub_ref = rest[:5]
        rest = rest[5:]
    o_ref, m_scr, l_scr, acc_scr = rest
    step = pl.program_id(0)
    i = it_ref[step]
    j = jt_ref[step]

    @pl.when(j == 0)
    def _():
        m_scr[...] = jnp.full(m_scr.shape, NEG_INF, f32)
        l_scr[...] = jnp.zeros(l_scr.shape, f32)
        acc_scr[...] = jnp.zeros(acc_scr.shape, f32)

    def attend(bias):
        for h in range(n_heads):
            hv = h // v_group
            s = _dot_nt(q_ref[:, h * dq:(h + 1) * dq], k_ref[:, h * dq:(h + 1) * dq])
            if bias is not None:
                s = s + bias
            m_prev = m_scr[h]
            m_new = jnp.maximum(m_prev, jnp.max(s, axis=1, keepdims=True))
            alpha = jnp.exp(m_prev - m_new)
            p = jnp.exp(s - jnp.tile(m_new, (1, t // LANES)))
            l_scr[h] = alpha * l_scr[h] + jnp.sum(p, axis=1, keepdims=True)
            acc_scr[h] = acc_scr[h] * alpha[:, :dv] + _dot(p.astype(bf16), v_ref[:, hv * dv:(hv + 1) * dv])
            m_scr[h] = m_new

    if has_bias:
        attend(bias_ref[...].astype(f32))
    else:
        @pl.when(j < i)
        def _():
            attend(None)

        @pl.when(j == i)
        def _():
            r = lax.broadcasted_iota(i32, (t, t), 0) // CHUNK
            c = lax.broadcasted_iota(i32, (t, t), 1) // CHUNK
            attend(jnp.where(c <= r, 0.0, NEG_INF))

    @pl.when(j == i)
    def _():
        if diff:
            lam = (jnp.exp(jnp.sum(lq1_ref[...] * lk1_ref[...], axis=1, keepdims=True))
                   - jnp.exp(jnp.sum(lq2_ref[...] * lk2_ref[...], axis=1, keepdims=True)) + lambda_init)
            for hc in range(n_heads // 2):
                o1 = acc_scr[2 * hc] / l_scr[2 * hc][:, :dv]
                o2 = acc_scr[2 * hc + 1] / l_scr[2 * hc + 1][:, :dv]
                o = _rms(o1 - lam * o2, gsub_ref[...]) * (1.0 - lambda_init)
                o_ref[:, hc * dv:(hc + 1) * dv] = o.astype(o_ref.dtype)
        else:
            for h in range(n_heads):
                o_ref[:, h * dv:(h + 1) * dv] = (acc_scr[h] / l_scr[h][:, :dv]).astype(o_ref.dtype)


def _flash(q, k, v, *, n_heads, dq, dv, v_group=1, bias=None, diff_params=None, lambda_init=0.0, name):
    seq = q.shape[0]
    t = min(512, seq)
    nq = seq // t
    pairs = [(i, j) for i in range(nq) for j in range(i + 1)]
    it = jnp.asarray([p[0] for p in pairs], i32)
    jt = jnp.asarray([p[1] for p in pairs], i32)
    n_out = (n_heads // v_group) * dv
    in_specs = [pl.BlockSpec((t, q.shape[1]), lambda s, it, jt: (it[s], 0)),
                pl.BlockSpec((t, k.shape[1]), lambda s, it, jt: (jt[s], 0)),
                pl.BlockSpec((t, v.shape[1]), lambda s, it, jt: (jt[s], 0))]
    args = [q, k, v]
    if bias is not None:
        in_specs.append(pl.BlockSpec((t, t), lambda s, it, jt: (it[s], jt[s])))
        args.append(bias)
    if diff_params is not None:
        for a in diff_params:
            in_specs.append(pl.BlockSpec(a.shape, lambda s, it, jt: (0, 0)))
            args.append(a)
    body = functools.partial(_flash_body, t=t, n_heads=n_heads, dq=dq, dv=dv, v_group=v_group,
                             has_bias=bias is not None, diff=diff_params is not None, lambda_init=lambda_init)
    return pl.pallas_call(
        body,
        grid_spec=pltpu.PrefetchScalarGridSpec(
            num_scalar_prefetch=2,
            grid=(len(pairs),),
            in_specs=in_specs,
            out_specs=pl.BlockSpec((t, n_out), lambda s, it, jt: (it[s], 0)),
            scratch_shapes=[pltpu.VMEM((n_heads, t, LANES), f32), pltpu.VMEM((n_heads, t, LANES), f32),
                            pltpu.VMEM((n_heads, t, dv), f32)]),
        out_shape=jax.ShapeDtypeStruct((seq, n_out), bf16),
        compiler_params=_cparams(("arbitrary",)),
        name=name,
    )(it, jt, *args)


def _ffn_body(*refs, moe, final):
    refs = list(refs)
    x_ref, g_ref = refs[:2]
    refs = refs[2:]
    comb_ref = refs.pop(0) if moe else None
    wg_ref, wu_ref, wd_ref = refs[:3]
    refs = refs[3:]
    fg_ref = refs.pop(0) if final else None
    o_ref, xn_scr, acc_scr = refs
    e = pl.program_id(1)
    f = pl.program_id(2)

    @pl.when(jnp.logical_and(e == 0, f == 0))
    def _():
        xn_scr[...] = _rms(x_ref[...], g_ref[...]).astype(bf16)
        acc_scr[...] = jnp.zeros(acc_scr.shape, f32)

    xn = xn_scr[...]
    gate = _dot(xn, wg_ref[0])
    up = _dot(xn, wu_ref[0])
    act = (gate / (1.0 + jnp.exp(-gate)) * up).astype(bf16)
    y = _dot(act, wd_ref[0])
    if moe:
        comb = comb_ref[...]
        lane = lax.broadcasted_iota(i32, comb.shape, 1)
        y = y * jnp.sum(jnp.where(lane == e, comb, 0.0), axis=1, keepdims=True)
    acc_scr[...] += y

    @pl.when(jnp.logical_and(e == pl.num_programs(1) - 1, f == pl.num_programs(2) - 1))
    def _():
        out = x_ref[...] + acc_scr[...]
        if final:
            out = _rms(out, fg_ref[...])
        o_ref[...] = out


def _ffn(x, g, wg, wu, wd, *, tm, tf, comb=None, final_g=None, name):
    seq = x.shape[0]
    n_e, _, dff = wg.shape
    moe = comb is not None
    final = final_g is not None
    in_specs = [pl.BlockSpec((tm, D_MODEL), lambda i, e, f: (i, 0)),
                pl.BlockSpec((1, D_MODEL), lambda i, e, f: (0, 0))]
    args = [x, g]
    if moe:
        in_specs.append(pl.BlockSpec((tm, LANES), lambda i, e, f: (i, 0)))
        args.append(comb)
    in_specs += [pl.BlockSpec((1, D_MODEL, tf), lambda i, e, f: (e, 0, f)),
                 pl.BlockSpec((1, D_MODEL, tf), lambda i, e, f: (e, 0, f)),
                 pl.BlockSpec((1, tf, D_MODEL), lambda i, e, f: (e, f, 0))]
    args += [wg, wu, wd]
    if final:
        in_specs.append(pl.BlockSpec((1, D_MODEL), lambda i, e, f: (0, 0)))
        args.append(final_g)
    return pl.pallas_call(
        functools.partial(_ffn_body, moe=moe, final=final),
        grid=(seq // tm, n_e, dff // tf),
        in_specs=in_specs,
        out_specs=pl.BlockSpec((tm, D_MODEL), lambda i, e, f: (i, 0)),
        out_shape=jax.ShapeDtypeStruct((seq, D_MODEL), f32),
        scratch_shapes=[pltpu.VMEM((tm, D_MODEL), bf16), pltpu.VMEM((tm, D_MODEL), f32)],
        compiler_params=_cparams(("parallel", "arbitrary", "arbitrary")),
        name=name,
    )(*args)


def _even_layer(h, norm_mix, w_in, g_q_lat, w_uq, g_kv_lat, w_ukv, g_idx_k, w_out, norm_ffn, w_gate, w_up, w_down):
    seq = h.shape[0]
    sizes = (Q_LORA, KV_LORA, ROPE_A, H_B * DH_B, H_B * DH_B, H_B * DH_B, H_IDX * D_IDX, D_IDX, H_IDX)
    offs = np.cumsum((0,) + sizes)
    w_cq, w_ckv, w_kr, w_qb, w_kb, w_vb, w_qi, w_ki, w_wi = [w_in[:, offs[n]:offs[n + 1]] for n in range(9)]

    scale_a = (NOPE_A + ROPE_A) ** -0.5
    wq3 = (w_uq * scale_a).reshape(Q_LORA, H_A, NOPE_A + ROPE_A)
    wq = jnp.pad(wq3, ((0, 0), (0, 0), (0, HEAD_PAD_A - NOPE_A - ROPE_A))).reshape(Q_LORA, H_A * HEAD_PAD_A)
    wq_sw = _swap_cols(wq, HEAD_PAD_A, NOPE_A, ROPE_A)
    wkv3 = w_ukv.reshape(KV_LORA, H_A, NOPE_A + V_A)
    wk = jnp.pad(wkv3[:, :, :NOPE_A], ((0, 0), (0, 0), (0, HEAD_PAD_A - NOPE_A))).reshape(KV_LORA, H_A * HEAD_PAD_A)
    wv = wkv3[:, :, NOPE_A:].reshape(KV_LORA, H_A * V_A)
    place = jnp.zeros((ROPE_A, H_A, HEAD_PAD_A), f32)
    place = place.at[:, :, NOPE_A:NOPE_A + ROPE_A].set(jnp.eye(ROPE_A, dtype=f32)[:, None, :])
    place = place.reshape(ROPE_A, H_A * HEAD_PAD_A)

    w_qb = w_qb * DH_B ** -0.5
    w_qi = w_qi * D_IDX ** -0.5
    w_small = jnp.concatenate([w_kr, _swap_cols(w_kr, ROPE_A, 0, ROPE_A), w_ki, _swap_cols(w_ki, D_IDX, 0, ROT_IDX)], 1)
    w_wi_p = _pad_cols(w_wi * H_IDX ** -0.5, LANES)
    g_idx = g_idx_k.reshape(1, D_IDX)
    g_idx_sw = jnp.concatenate([g_idx[:, ROT_IDX // 2:ROT_IDX], g_idx[:, :ROT_IDX // 2], g_idx[:, ROT_IDX:]], 1)

    ca, sa = _rope_tables(seq, ROPE_A, HEAD_PAD_A, NOPE_A, LANES)
    cb, sb = _rope_tables(seq, ROT_B, DH_B, 0, LANES)
    ci, si = _rope_tables(seq, ROT_IDX, D_IDX, 0, LANES)
    ckr, skr = _rope_tables(seq, ROPE_A, ROPE_A, 0, ROPE_A)

    consts = [norm_mix.reshape(1, -1), g_q_lat.reshape(1, -1), g_kv_lat.reshape(1, -1), g_idx, g_idx_sw]
    weights = [jnp.concatenate([w_cq, w_ckv], 1), wq, wq_sw, wk, wv, place,
               w_qb, _swap_cols(w_qb, DH_B, 0, ROT_B), w_kb, _swap_cols(w_kb, DH_B, 0, ROT_B), w_vb,
               w_qi, _swap_cols(w_qi, D_IDX, 0, ROT_IDX), w_small, w_wi_p]
    weights = [w.astype(bf16) for w in weights]
    sds = lambda n, dt: jax.ShapeDtypeStruct((seq, n), dt)
    outs = [sds(H_A * HEAD_PAD_A, bf16), sds(H_A * HEAD_PAD_A, bf16), sds(H_A * V_A, bf16),
            sds(H_B * DH_B, bf16), sds(H_B * DH_B, bf16), sds(H_B * DH_B, bf16),
            sds(H_IDX * D_IDX, bf16), sds(D_IDX, bf16), sds(LANES, f32)]
    qa, ka, va, qb, kb, vb, qi, ki, wi = _rows_call(
        _even_proj_body, seq, 256, [h, ca, sa, cb, sb, ci, si, ckr, skr], consts + weights, outs, "even_proj")

    o_a = _flash(qa, ka, va, n_heads=H_A, dq=HEAD_PAD_A, dv=V_A, name="mla_attn")
    top_k = min(TOPK_MAX, seq // 4)
    bias = _dsa_select(qi, wi, ki.T, seq, top_k)
    o_b = _flash(qb, kb, vb, n_heads=H_B, dq=DH_B, dv=DH_B, bias=bias, name="dsa_attn")

    w_out = w_out.astype(bf16)
    n_a = H_A * V_A
    (h,) = _rows_call(_out_proj2_body, seq, 512, [h, o_a, o_b], [w_out[:n_a], w_out[n_a:]],
                      [jax.ShapeDtypeStruct((seq, D_MODEL), f32)], "even_out_proj")
    return _ffn(h, norm_ffn.reshape(1, -1), w_gate.astype(bf16)[None], w_up.astype(bf16)[None],
                w_down.astype(bf16)[None], tm=512, tf=D_FF // 2, name="dense_ffn")


def _odd_layer(h, layer, norm_mix, w_qkv, lq1, lk1, lq2, lk2, g_sub, w_out, norm_ffn, w_router, w_gate_e, w_up_e,
               w_down_e, final_norm):
    seq = h.shape[0]
    lambda_init = 0.8 - 0.6 * math.exp(-0.3 * layer)
    n = H_C * 2 * DH_C
    w_q = w_qkv[:, :n] * DH_C ** -0.5
    w_k = w_qkv[:, n:2 * n]
    w_v = w_qkv[:, 2 * n:]
    cb, sb = _rope_tables(seq, ROT_C, DH_C, 0, LANES)
    weights = [w_q, _swap_cols(w_q, DH_C, 0, ROT_C), w_k, _swap_cols(w_k, DH_C, 0, ROT_C), w_v]
    weights = [w.astype(bf16) for w in weights]
    sds = jax.ShapeDtypeStruct((seq, n), bf16)
    q, k, v = _rows_call(_odd_proj_body, seq, 512, [h, cb, sb], [norm_mix.reshape(1, -1)] + weights,
                         [sds, sds, sds], "odd_proj")
    diff_params = [lq1.reshape(1, -1), lk1.reshape(1, -1), lq2.reshape(1, -1), lk2.reshape(1, -1),
                   g_sub.reshape(1, -1)]
    o = _flash(q, k, v, n_heads=2 * H_C, dq=DH_C, dv=2 * DH_C, v_group=2, diff_params=diff_params,
               lambda_init=lambda_init, name="diff_attn")
    (h,) = _rows_call(_out_proj1_body, seq, 512, [h, o], [w_out.astype(bf16)],
                      [jax.ShapeDtypeStruct((seq, D_MODEL), f32)], "odd_out_proj")
    g_ffn = norm_ffn.reshape(1, -1)
    w_r = _pad_cols(w_router, LANES)
    w_r_hi = w_r.astype(bf16)
    w_r_lo = (w_r - w_r_hi.astype(f32)).astype(bf16)
    (comb,) = _rows_call(_router_body, seq, 512, [h], [g_ffn, w_r_hi, w_r_lo],
                         [jax.ShapeDtypeStruct((seq, LANES), f32)], "router")
    return _ffn(h, g_ffn, w_gate_e.astype(bf16), w_up_e.astype(bf16), w_down_e.astype(bf16),
                tm=min(1024, seq), tf=D_FF_E // 4, comb=comb, final_g=final_norm.reshape(1, -1), name="moe_ffn")


def kernel(x, ev_norm_mix, ev_w_in, ev_g_q_lat, ev_w_uq, ev_g_kv_lat, ev_w_ukv, ev_g_idx_k, ev_w_out, ev_norm_ffn, ev_w_gate, ev_w_up, ev_w_down, od_norm_mix, od_w_qkv, od_lambda_q1, od_lambda_k1, od_lambda_q2, od_lambda_k2, od_g_sub, od_w_out, od_norm_ffn, od_w_router, od_w_gate_e, od_w_up_e, od_w_down_e, final_norm):
    batch, seq, _ = x.shape
    assert batch == 1 and ev_w_in.shape[0] == 1 and od_w_qkv.shape[0] == 1
    h = x[0]
    h = _even_layer(h, ev_norm_mix[0], ev_w_in[0], ev_g_q_lat[0], ev_w_uq[0], ev_g_kv_lat[0], ev_w_ukv[0],
                    ev_g_idx_k[0], ev_w_out[0], ev_norm_ffn[0], ev_w_gate[0], ev_w_up[0], ev_w_down[0])
    h = _odd_layer(h, 1, od_norm_mix[0], od_w_qkv[0], od_lambda_q1[0], od_lambda_k1[0], od_lambda_q2[0],
                   od_lambda_k2[0], od_g_sub[0], od_w_out[0], od_norm_ffn[0], od_w_router[0], od_w_gate_e[0],
                   od_w_up_e[0], od_w_down_e[0], final_norm)
    return h[None]
```

```python
import functools
import math

import numpy as np
import jax
import jax.numpy as jnp
from jax import lax
from jax.experimental import pallas as pl
from jax.experimental.pallas import tpu as pltpu

f32 = jnp.float32
bf16 = jnp.bfloat16
i32 = jnp.int32

D_MODEL = 1024
CHUNK = 64
ROPE_THETA = 500000.0
NORM_EPS = 1e-6
NEG_INF = -1e30
LOG2E = math.log2(math.e)

H_A, Q_LORA, KV_LORA, NOPE_A, ROPE_A, V_A = 8, 256, 128, 64, 32, 64
H_B, DH_B, ROT_B = 8, 64, 16
H_IDX, D_IDX, ROT_IDX = 8, 32, 8
TOPK_MAX = 256
H_C, DH_C, ROT_C = 8, 64, 16
D_FF, N_EXP, D_FF_E = 2816, 8, 3584

LANES = 128
HEAD_PAD_A = 128

_NEG_BITS = int(np.float32(NEG_INF).view(np.int32))
NEG_KEY = _NEG_BITS ^ 0x7FFFFFFF
INT_MIN = -(2 ** 31)

VMEM_LIMIT = 56 * 1024 * 1024


def _cparams(sem):
    return pltpu.CompilerParams(dimension_semantics=sem, vmem_limit_bytes=VMEM_LIMIT)


def _rms(x, g):
    var = jnp.mean(x * x, axis=-1, keepdims=True)
    return x * lax.rsqrt(var + NORM_EPS) * g


def _dot(a, b):
    return jnp.dot(a, b, preferred_element_type=f32)


def _dot_nt(a, b):
    return lax.dot_general(a, b, (((1,), (1,)), ((), ())), preferred_element_type=f32)


def _rope_tables(seq, rot_dim, head_width, offset, width):
    pos = jnp.arange(seq, dtype=f32)
    inv_freq = ROPE_THETA ** (-jnp.arange(0, rot_dim, 2, dtype=f32) / rot_dim)
    ang = pos[:, None] * inv_freq[None, :]
    cos, sin = jnp.cos(ang), jnp.sin(ang)
    c = jnp.ones((seq, head_width), f32).at[:, offset:offset + rot_dim].set(jnp.concatenate([cos, cos], -1))
    s = jnp.zeros((seq, head_width), f32).at[:, offset:offset + rot_dim].set(jnp.concatenate([-sin, sin], -1))
    reps = width // head_width
    return jnp.tile(c, (1, reps)), jnp.tile(s, (1, reps))


def _swap_cols(w, head_width, offset, rot_dim):
    k, n = w.shape
    half = rot_dim // 2
    w3 = w.reshape(k, n // head_width, head_width)
    out = jnp.zeros_like(w3)
    out = out.at[:, :, offset:offset + half].set(w3[:, :, offset + half:offset + rot_dim])
    out = out.at[:, :, offset + half:offset + rot_dim].set(w3[:, :, offset:offset + half])
    return out.reshape(k, n)


def _pad_cols(w, width):
    return jnp.pad(w, ((0, 0), (0, width - w.shape[1])))


def _rows_call(body, seq, tm, row_ins, const_ins, out_sds, name):
    def rspec(a):
        return pl.BlockSpec((tm, a.shape[1]), lambda i: (i, 0))

    def cspec(a):
        nd = a.ndim
        return pl.BlockSpec(a.shape, lambda i: (0,) * nd)

    return pl.pallas_call(
        body,
        grid=(seq // tm,),
        in_specs=[rspec(a) for a in row_ins] + [cspec(a) for a in const_ins],
        out_specs=[pl.BlockSpec((tm, o.shape[1]), lambda i: (i, 0)) for o in out_sds],
        out_shape=out_sds,
        compiler_params=_cparams(("parallel",)),
        name=name,
    )(*row_ins, *const_ins)


def _even_proj_body(x_ref, ca_ref, sa_ref, cb_ref, sb_ref, ci_ref, si_ref, ckr_ref, skr_ref,
                    g_ref, gq_ref, gkv_ref, gi_ref, gisw_ref,
                    wlat_ref, wq_ref, wqsw_ref, wk_ref, wv_ref, place_ref,
                    wqb_ref, wqbsw_ref, wkb_ref, wkbsw_ref, wvb_ref, wqi_ref, wqisw_ref, wsm_ref, wwi_ref,
                    qa_ref, ka_ref, va_ref, qb_ref, kb_ref, vb_ref, qi_ref, ki_ref, wi_ref):
    xn = _rms(x_ref[...], g_ref[...]).astype(bf16)
    lat = _dot(xn, wlat_ref[...])
    cqn = _rms(lat[:, :Q_LORA], gq_ref[...]).astype(bf16)
    ckvn = _rms(lat[:, Q_LORA:], gkv_ref[...]).astype(bf16)
    reps_a = qa_ref.shape[1] // LANES
    ca = jnp.tile(ca_ref[...], (1, reps_a))
    sa = jnp.tile(sa_ref[...], (1, reps_a))
    qa_ref[...] = (_dot(cqn, wq_ref[...]) * ca + _dot(cqn, wqsw_ref[...]) * sa).astype(bf16)
    small = _dot(xn, wsm_ref[...])
    kr, kr_sw = small[:, 0:ROPE_A], small[:, ROPE_A:2 * ROPE_A]
    kpe = (kr * ckr_ref[...] + kr_sw * skr_ref[...]).astype(bf16)
    ka_ref[...] = (_dot(ckvn, wk_ref[...]) + _dot(kpe, place_ref[...])).astype(bf16)
    va_ref[...] = _dot(ckvn, wv_ref[...]).astype(bf16)
    reps_b = qb_ref.shape[1] // LANES
    cb = jnp.tile(cb_ref[...], (1, reps_b))
    sb = jnp.tile(sb_ref[...], (1, reps_b))
    qb_ref[...] = (_dot(xn, wqb_ref[...]) * cb + _dot(xn, wqbsw_ref[...]) * sb).astype(bf16)
    kb_ref[...] = (_dot(xn, wkb_ref[...]) * cb + _dot(xn, wkbsw_ref[...]) * sb).astype(bf16)
    vb_ref[...] = _dot(xn, wvb_ref[...]).astype(bf16)
    reps_i = qi_ref.shape[1] // LANES
    ci = jnp.tile(ci_ref[...], (1, reps_i))
    si = jnp.tile(si_ref[...], (1, reps_i))
    qi_ref[...] = (_dot(xn, wqi_ref[...]) * ci + _dot(xn, wqisw_ref[...]) * si).astype(bf16)
    ki, ki_sw = small[:, 2 * ROPE_A:2 * ROPE_A + D_IDX], small[:, 2 * ROPE_A + D_IDX:2 * ROPE_A + 2 * D_IDX]
    r = lax.rsqrt(jnp.mean(ki * ki, axis=-1, keepdims=True) + NORM_EPS)
    ci32, si32 = ci_ref[:, 0:D_IDX], si_ref[:, 0:D_IDX]
    ki_ref[...] = (ki * r * gi_ref[...] * ci32 + ki_sw * r * gisw_ref[...] * si32).astype(bf16)
    wi_ref[...] = _dot(xn, wwi_ref[...])


def _odd_proj_body(x_ref, cb_ref, sb_ref, g_ref, wq_ref, wqsw_ref, wk_ref, wksw_ref, wv_ref,
                   q_ref, k_ref, v_ref):
    xn = _rms(x_ref[...], g_ref[...]).astype(bf16)
    reps = q_ref.shape[1] // LANES
    cb = jnp.tile(cb_ref[...], (1, reps))
    sb = jnp.tile(sb_ref[...], (1, reps))
    q_ref[...] = (_dot(xn, wq_ref[...]) * cb + _dot(xn, wqsw_ref[...]) * sb).astype(bf16)
    k_ref[...] = (_dot(xn, wk_ref[...]) * cb + _dot(xn, wksw_ref[...]) * sb).astype(bf16)
    v_ref[...] = _dot(xn, wv_ref[...]).astype(bf16)


def _out_proj2_body(x_ref, a1_ref, a2_ref, w1_ref, w2_ref, o_ref):
    o_ref[...] = x_ref[...] + _dot(a1_ref[...], w1_ref[...]) + _dot(a2_ref[...], w2_ref[...])


def _out_proj1_body(x_ref, a_ref, w_ref, o_ref):
    o_ref[...] = x_ref[...] + _dot(a_ref[...], w_ref[...])


def _router_body(x_ref, g_ref, whi_ref, wlo_ref, comb_ref):
    xn = _rms(x_ref[...], g_ref[...])
    hi = xn.astype(bf16)
    lo = (xn - hi.astype(f32)).astype(bf16)
    logits = _dot(hi, whi_ref[...]) + _dot(lo, whi_ref[...]) + _dot(hi, wlo_ref[...])
    lane = lax.broadcasted_iota(i32, logits.shape, 1).astype(f32)
    lg = jnp.where(lane < N_EXP, logits, -jnp.inf)
    m1 = jnp.max(lg, axis=1, keepdims=True)
    i1 = jnp.min(jnp.where(lg == m1, lane, float(LANES)), axis=1, keepdims=True)
    lg2 = jnp.where(lane == i1, -jnp.inf, lg)
    m2 = jnp.max(lg2, axis=1, keepdims=True)
    i2 = jnp.min(jnp.where(lg2 == m2, lane, float(LANES)), axis=1, keepdims=True)
    e2 = jnp.exp(m2 - m1)
    den = 1.0 + e2
    comb_ref[...] = jnp.where(lane == i1, 1.0 / den, 0.0) + jnp.where(lane == i2, e2 / den, 0.0)


def _dsa_select_body(qi_ref, wi_ref, kit_ref, out_ref, keys_scr, *, tq, tk, top_k, idx_bits):
    q0 = pl.program_id(0) * tq
    n_kt = (q0 + tq + tk - 1) // tk
    qh = [qi_ref[:, h * D_IDX:(h + 1) * D_IDX] for h in range(H_IDX)]
    w = wi_ref[...]
    wb = [jnp.broadcast_to(w[:, h:h + 1], (tq, tk)) for h in range(H_IDX)]
    row = q0 + lax.broadcasted_iota(i32, (tq, 1), 0)
    row_lim = (row // CHUNK + 1) * CHUNK

    def cols_of(kt):
        c0 = pl.multiple_of(kt * tk, tk)
        return c0, c0 + lax.broadcasted_iota(i32, (tq, tk), 1)

    def score_tile(kt, carry):
        c0, col = cols_of(kt)
        kt_tile = kit_ref[:, pl.ds(c0, tk)]
        acc = jnp.zeros((tq, tk), f32)
        for h in range(H_IDX):
            acc = acc + jnp.maximum(_dot(qh[h], kt_tile), 0.0) * wb[h]
        sc = jnp.where(col < row_lim, acc, NEG_INF)
        bits = lax.bitcast_convert_type(sc, i32)
        keys_scr[:, pl.ds(c0, tk)] = jnp.where(bits < 0, bits ^ 0x7FFFFFFF, bits)
        return carry

    lax.fori_loop(0, n_kt, score_tile, 0)

    def count(pred):
        def body(kt, acc):
            c0, col = cols_of(kt)
            m = pred(keys_scr[:, pl.ds(c0, tk)], col).astype(i32)
            for u in range(tk // LANES):
                acc = acc + m[:, u * LANES:(u + 1) * LANES]
            return acc

        acc = lax.fori_loop(0, n_kt, body, jnp.zeros((tq, LANES), i32))
        return jnp.sum(acc.astype(f32), axis=1, keepdims=True).astype(i32)

    def thr_step(it, u):
        cand_u = u | lax.shift_left(jnp.int32(1), 31 - it)
        cand = cand_u ^ INT_MIN
        cnt = count(lambda ks, col: ks >= cand)
        return jnp.where(cnt >= top_k, cand_u, u)

    thr = lax.fori_loop(0, 32, thr_step, jnp.zeros((tq, 1), i32)) ^ INT_MIN
    n_ge = count(lambda ks, col: ks >= thr)
    thr_vis = jnp.maximum(thr, NEG_KEY + 1)
    excess = jnp.logical_and(n_ge > top_k, thr > NEG_KEY)
    any_excess = jnp.max(excess.astype(f32)) > 0.0
    out_ref[...] = jnp.full(out_ref.shape, NEG_INF, bf16)

    @pl.when(jnp.logical_not(any_excess))
    def _():
        def write_tile(kt, carry):
            c0, _ = cols_of(kt)
            ks = keys_scr[:, pl.ds(c0, tk)]
            out_ref[:, pl.ds(c0, tk)] = jnp.where(ks >= thr_vis, 0.0, NEG_INF).astype(bf16)
            return carry

        lax.fori_loop(0, n_kt, write_tile, 0)

    @pl.when(any_excess)
    def _():
        need = top_k - count(lambda ks, col: ks > thr)

        def tie_step(it, xv):
            cand = xv | lax.shift_left(jnp.int32(1), idx_bits - 1 - it)
            cnt = count(lambda ks, col: jnp.logical_and(ks == thr, col < cand))
            return jnp.where(cnt < need, cand, xv)

        xlim = lax.fori_loop(0, idx_bits, tie_step, jnp.zeros((tq, 1), i32))

        def write_tile(kt, carry):
            c0, col = cols_of(kt)
            ks = keys_scr[:, pl.ds(c0, tk)]
            sel = jnp.logical_or(ks > thr, jnp.logical_and(ks == thr, col <= xlim))
            sel = jnp.logical_and(sel, ks >= thr_vis)
            out_ref[:, pl.ds(c0, tk)] = jnp.where(sel, 0.0, NEG_INF).astype(bf16)
            return carry

        lax.fori_loop(0, n_kt, write_tile, 0)


def _dsa_select(qi, wi, kit, seq, top_k):
    tq, tk = min(256, seq), 512
    idx_bits = max(1, int(math.ceil(math.log2(seq))))
    body = functools.partial(_dsa_select_body, tq=tq, tk=tk, top_k=top_k, idx_bits=idx_bits)
    return pl.pallas_call(
        body,
        grid=(seq // tq,),
        in_specs=[pl.BlockSpec((tq, qi.shape[1]), lambda i: (i, 0)),
                  pl.BlockSpec((tq, wi.shape[1]), lambda i: (i, 0)),
                  pl.BlockSpec(kit.shape, lambda i: (0, 0))],
        out_specs=pl.BlockSpec((tq, seq), lambda i: (i, 0)),
        out_shape=jax.ShapeDtypeStruct((seq, seq), bf16),
        scratch_shapes=[pltpu.VMEM((tq, seq), i32)],
        compiler_params=_cparams(("parallel",)),
        name="dsa_select",
    )(qi, wi, kit)


FLAG_FIRST, FLAG_MASK, FLAG_LAST = 1, 2, 4


def _flash_body(it_ref, jt_ref, ft_ref, q_ref, k_ref, v_ref, *rest, tq, tk, n_heads, dq, dv, v_group, has_bias, diff,
                lambda_init):
    rest = list(rest)
    bias_ref = rest.pop(0) if has_bias else None
    if diff:
        lq1_ref, lk1_ref, lq2_ref, lk2_ref, gsub_ref = rest[:5]
        rest = rest[5:]
    o_ref, m_scr, l_scr, acc_scr = rest
    step = pl.program_id(0)
    i = it_ref[step]
    j = jt_ref[step]
    flag = ft_ref[step]

    @pl.when((flag & FLAG_FIRST) != 0)
    def _():
        m_scr[...] = jnp.full(m_scr.shape, NEG_INF, f32)
        l_scr[...] = jnp.zeros(l_scr.shape, f32)
        acc_scr[...] = jnp.zeros(acc_scr.shape, f32)

    def attend(bias):
        for h in range(n_heads):
            hv = h // v_group
            s = _dot_nt(q_ref[:, h * dq:(h + 1) * dq], k_ref[:, h * dq:(h + 1) * dq])
            if bias is not None:
                s = s + bias
            m_prev = m_scr[h]
            m_new = jnp.maximum(m_prev, jnp.max(s, axis=1, keepdims=True))
            alpha = jnp.exp2(m_prev - m_new)
            p = jnp.exp2(s - jnp.tile(m_new, (1, tk // LANES)))
            l_scr[h] = alpha * l_scr[h] + jnp.sum(p, axis=1, keepdims=True)
            acc_scr[h] = acc_scr[h] * alpha[:, :dv] + _dot(p.astype(bf16), v_ref[:, hv * dv:(hv + 1) * dv])
            m_scr[h] = m_new

    if has_bias:
        attend(bias_ref[...].astype(f32))
    else:
        @pl.when((flag & FLAG_MASK) == 0)
        def _():
            attend(None)

        @pl.when((flag & FLAG_MASK) != 0)
        def _():
            r = (i * tq + lax.broadcasted_iota(i32, (tq, tk), 0)) // CHUNK
            c = (j * tk + lax.broadcasted_iota(i32, (tq, tk), 1)) // CHUNK
            attend(jnp.where(c <= r, 0.0, NEG_INF))

    @pl.when((flag & FLAG_LAST) != 0)
    def _():
        if diff:
            lam = (jnp.exp(jnp.sum(lq1_ref[...] * lk1_ref[...], axis=1, keepdims=True))
                   - jnp.exp(jnp.sum(lq2_ref[...] * lk2_ref[...], axis=1, keepdims=True)) + lambda_init)
            for hc in range(n_heads // 2):
                o1 = acc_scr[2 * hc] / l_scr[2 * hc][:, :dv]
                o2 = acc_scr[2 * hc + 1] / l_scr[2 * hc + 1][:, :dv]
                o = _rms(o1 - lam * o2, gsub_ref[...]) * (1.0 - lambda_init)
                o_ref[:, hc * dv:(hc + 1) * dv] = o.astype(o_ref.dtype)
        else:
            for h in range(n_heads):
                o_ref[:, h * dv:(h + 1) * dv] = (acc_scr[h] / l_scr[h][:, :dv]).astype(o_ref.dtype)


FLASH_TILE_ELEMS = 1024 * 512
FLASH_STATS_BYTES = 12 * 1024 * 1024


def _flash_tiles(seq, n_heads, dv):
    tq = 1024
    while tq > 128 and n_heads * tq * (2 * LANES + dv) * 4 > FLASH_STATS_BYTES:
        tq //= 2
    return min(tq, seq), min(FLASH_TILE_ELEMS // tq, seq)


def _flash(q, k, v, *, n_heads, dq, dv, v_group=1, bias=None, diff_params=None, lambda_init=0.0, name):
    seq = q.shape[0]
    tq, tk = _flash_tiles(seq, n_heads, dv)
    pairs = []
    for i in range(seq // tq):
        j_last = ((i + 1) * tq - 1) // tk
        for j in range(j_last + 1):
            needs_mask = (j + 1) * tk > i * tq + CHUNK
            pairs.append((i, j, (FLAG_FIRST if j == 0 else 0) | (FLAG_MASK if needs_mask else 0)
                          | (FLAG_LAST if j == j_last else 0)))
    it = jnp.asarray([p[0] for p in pairs], i32)
    jt = jnp.asarray([p[1] for p in pairs], i32)
    ft = jnp.asarray([p[2] for p in pairs], i32)
    n_out = (n_heads // v_group) * dv
    in_specs = [pl.BlockSpec((tq, q.shape[1]), lambda s, it, jt, ft: (it[s], 0)),
                pl.BlockSpec((tk, k.shape[1]), lambda s, it, jt, ft: (jt[s], 0)),
                pl.BlockSpec((tk, v.shape[1]), lambda s, it, jt, ft: (jt[s], 0))]
    args = [q, k, v]
    if bias is not None:
        in_specs.append(pl.BlockSpec((tq, tk), lambda s, it, jt, ft: (it[s], jt[s])))
        args.append(bias)
    if diff_params is not None:
        for a in diff_params:
            in_specs.append(pl.BlockSpec(a.shape, lambda s, it, jt, ft: (0, 0)))
            args.append(a)
    body = functools.partial(_flash_body, tq=tq, tk=tk, n_heads=n_heads, dq=dq, dv=dv, v_group=v_group,
                             has_bias=bias is not None, diff=diff_params is not None, lambda_init=lambda_init)
    return pl.pallas_call(
        body,
        grid_spec=pltpu.PrefetchScalarGridSpec(
            num_scalar_prefetch=3,
            grid=(len(pairs),),
            in_specs=in_specs,
            out_specs=pl.BlockSpec((tq, n_out), lambda s, it, jt, ft: (it[s], 0)),
            scratch_shapes=[pltpu.VMEM((n_heads, tq, LANES), f32), pltpu.VMEM((n_heads, tq, LANES), f32),
                            pltpu.VMEM((n_heads, tq, dv), f32)]),
        out_shape=jax.ShapeDtypeStruct((seq, n_out), bf16),
        compiler_params=_cparams(("arbitrary",)),
        name=name,
    )(it, jt, ft, *args)


def _ffn_body(*refs, moe, final):
    refs = list(refs)
    x_ref, g_ref = refs[:2]
    refs = refs[2:]
    comb_ref = refs.pop(0) if moe else None
    wg_ref, wu_ref, wd_ref = refs[:3]
    refs = refs[3:]
    fg_ref = refs.pop(0) if final else None
    o_ref, xn_scr, acc_scr = refs
    e = pl.program_id(1)
    f = pl.program_id(2)

    @pl.when(jnp.logical_and(e == 0, f == 0))
    def _():
        xn_scr[...] = _rms(x_ref[...], g_ref[...]).astype(bf16)
        acc_scr[...] = jnp.zeros(acc_scr.shape, f32)

    xn = xn_scr[...]
    gate = _dot(xn, wg_ref[0])
    up = _dot(xn, wu_ref[0])
    act = (gate / (1.0 + jnp.exp(-gate)) * up).astype(bf16)
    y = _dot(act, wd_ref[0])
    if moe:
        comb = comb_ref[...]
        lane = lax.broadcasted_iota(i32, comb.shape, 1)
        y = y * jnp.sum(jnp.where(lane == e, comb, 0.0), axis=1, keepdims=True)
    acc_scr[...] += y

    @pl.when(jnp.logical_and(e == pl.num_programs(1) - 1, f == pl.num_programs(2) - 1))
    def _():
        out = x_ref[...] + acc_scr[...]
        if final:
            out = _rms(out, fg_ref[...])
        o_ref[...] = out


def _ffn(x, g, wg, wu, wd, *, tm, tf, comb=None, final_g=None, name):
    seq = x.shape[0]
    n_e, _, dff = wg.shape
    moe = comb is not None
    final = final_g is not None
    in_specs = [pl.BlockSpec((tm, D_MODEL), lambda i, e, f: (i, 0)),
                pl.BlockSpec((1, D_MODEL), lambda i, e, f: (0, 0))]
    args = [x, g]
    if moe:
        in_specs.append(pl.BlockSpec((tm, LANES), lambda i, e, f: (i, 0)))
        args.append(comb)
    in_specs += [pl.BlockSpec((1, D_MODEL, tf), lambda i, e, f: (e, 0, f)),
                 pl.BlockSpec((1, D_MODEL, tf), lambda i, e, f: (e, 0, f)),
                 pl.BlockSpec((1, tf, D_MODEL), lambda i, e, f: (e, f, 0))]
    args += [wg, wu, wd]
    if final:
        in_specs.append(pl.BlockSpec((1, D_MODEL), lambda i, e, f: (0, 0)))
        args.append(final_g)
    return pl.pallas_call(
        functools.partial(_ffn_body, moe=moe, final=final),
        grid=(seq // tm, n_e, dff // tf),
        in_specs=in_specs,
        out_specs=pl.BlockSpec((tm, D_MODEL), lambda i, e, f: (i, 0)),
        out_shape=jax.ShapeDtypeStruct((seq, D_MODEL), f32),
        scratch_shapes=[pltpu.VMEM((tm, D_MODEL), bf16), pltpu.VMEM((tm, D_MODEL), f32)],
        compiler_params=_cparams(("parallel", "arbitrary", "arbitrary")),
        name=name,
    )(*args)


def _even_layer(h, norm_mix, w_in, g_q_lat, w_uq, g_kv_lat, w_ukv, g_idx_k, w_out, norm_ffn, w_gate, w_up, w_down):
    seq = h.shape[0]
    sizes = (Q_LORA, KV_LORA, ROPE_A, H_B * DH_B, H_B * DH_B, H_B * DH_B, H_IDX * D_IDX, D_IDX, H_IDX)
    offs = np.cumsum((0,) + sizes)
    w_cq, w_ckv, w_kr, w_qb, w_kb, w_vb, w_qi, w_ki, w_wi = [w_in[:, offs[n]:offs[n + 1]] for n in range(9)]

    scale_a = (NOPE_A + ROPE_A) ** -0.5 * LOG2E
    wq3 = (w_uq * scale_a).reshape(Q_LORA, H_A, NOPE_A + ROPE_A)
    wq = jnp.pad(wq3, ((0, 0), (0, 0), (0, HEAD_PAD_A - NOPE_A - ROPE_A))).reshape(Q_LORA, H_A * HEAD_PAD_A)
    wq_sw = _swap_cols(wq, HEAD_PAD_A, NOPE_A, ROPE_A)
    wkv3 = w_ukv.reshape(KV_LORA, H_A, NOPE_A + V_A)
    wk = jnp.pad(wkv3[:, :, :NOPE_A], ((0, 0), (0, 0), (0, HEAD_PAD_A - NOPE_A))).reshape(KV_LORA, H_A * HEAD_PAD_A)
    wv = wkv3[:, :, NOPE_A:].reshape(KV_LORA, H_A * V_A)
    place = jnp.zeros((ROPE_A, H_A, HEAD_PAD_A), f32)
    place = place.at[:, :, NOPE_A:NOPE_A + ROPE_A].set(jnp.eye(ROPE_A, dtype=f32)[:, None, :])
    place = place.reshape(ROPE_A, H_A * HEAD_PAD_A)

    w_qb = w_qb * (DH_B ** -0.5 * LOG2E)
    w_qi = w_qi * D_IDX ** -0.5
    w_small = jnp.concatenate([w_kr, _swap_cols(w_kr, ROPE_A, 0, ROPE_A), w_ki, _swap_cols(w_ki, D_IDX, 0, ROT_IDX)], 1)
    w_wi_p = _pad_cols(w_wi * H_IDX ** -0.5, LANES)
    g_idx = g_idx_k.reshape(1, D_IDX)
    g_idx_sw = jnp.concatenate([g_idx[:, ROT_IDX // 2:ROT_IDX], g_idx[:, :ROT_IDX // 2], g_idx[:, ROT_IDX:]], 1)

    ca, sa = _rope_tables(seq, ROPE_A, HEAD_PAD_A, NOPE_A, LANES)
    cb, sb = _rope_tables(seq, ROT_B, DH_B, 0, LANES)
    ci, si = _rope_tables(seq, ROT_IDX, D_IDX, 0, LANES)
    ckr, skr = _rope_tables(seq, ROPE_A, ROPE_A, 0, ROPE_A)

    consts = [norm_mix.reshape(1, -1), g_q_lat.reshape(1, -1), g_kv_lat.reshape(1, -1), g_idx, g_idx_sw]
    weights = [jnp.concatenate([w_cq, w_ckv], 1), wq, wq_sw, wk, wv, place,
               w_qb, _swap_cols(w_qb, DH_B, 0, ROT_B), w_kb, _swap_cols(w_kb, DH_B, 0, ROT_B), w_vb,
               w_qi, _swap_cols(w_qi, D_IDX, 0, ROT_IDX), w_small, w_wi_p]
    weights = [w.astype(bf16) for w in weights]
    sds = lambda n, dt: jax.ShapeDtypeStruct((seq, n), dt)
    outs = [sds(H_A * HEAD_PAD_A, bf16), sds(H_A * HEAD_PAD_A, bf16), sds(H_A * V_A, bf16),
            sds(H_B * DH_B, bf16), sds(H_B * DH_B, bf16), sds(H_B * DH_B, bf16),
            sds(H_IDX * D_IDX, bf16), sds(D_IDX, bf16), sds(LANES, f32)]
    qa, ka, va, qb, kb, vb, qi, ki, wi = _rows_call(
        _even_proj_body, seq, 256, [h, ca, sa, cb, sb, ci, si, ckr, skr], consts + weights, outs, "even_proj")

    o_a = _flash(qa, ka, va, n_heads=H_A, dq=HEAD_PAD_A, dv=V_A, name="mla_attn")
    top_k = min(TOPK_MAX, seq // 4)
    bias = _dsa_select(qi, wi, ki.T, seq, top_k)
    o_b = _flash(qb, kb, vb, n_heads=H_B, dq=DH_B, dv=DH_B, bias=bias, name="dsa_attn")

    w_out = w_out.astype(bf16)
    n_a = H_A * V_A
    (h,) = _rows_call(_out_proj2_body, seq, 512, [h, o_a, o_b], [w_out[:n_a], w_out[n_a:]],
                      [jax.ShapeDtypeStruct((seq, D_MODEL), f32)], "even_out_proj")
    return _ffn(h, norm_ffn.reshape(1, -1), w_gate.astype(bf16)[None], w_up.astype(bf16)[None],
                w_down.astype(bf16)[None], tm=512, tf=D_FF // 2, name="dense_ffn")


def _odd_layer(h, layer, norm_mix, w_qkv, lq1, lk1, lq2, lk2, g_sub, w_out, norm_ffn, w_router, w_gate_e, w_up_e,
               w_down_e, final_norm):
    seq = h.shape[0]
    lambda_init = 0.8 - 0.6 * math.exp(-0.3 * layer)
    n = H_C * 2 * DH_C
    w_q = w_qkv[:, :n] * (DH_C ** -0.5 * LOG2E)
    w_k = w_qkv[:, n:2 * n]
    w_v = w_qkv[:, 2 * n:]
    cb, sb = _rope_tables(seq, ROT_C, DH_C, 0, LANES)
    weights = [w_q, _swap_cols(w_q, DH_C, 0, ROT_C), w_k, _swap_cols(w_k, DH_C, 0, ROT_C), w_v]
    weights = [w.astype(bf16) for w in weights]
    sds = jax.ShapeDtypeStruct((seq, n), bf16)
    q, k, v = _rows_call(_odd_proj_body, seq, 512, [h, cb, sb], [norm_mix.reshape(1, -1)] + weights,
                         [sds, sds, sds], "odd_proj")
    diff_params = [lq1.reshape(1, -1), lk1.reshape(1, -1), lq2.reshape(1, -1), lk2.reshape(1, -1),
                   g_sub.reshape(1, -1)]
    o = _flash(q, k, v, n_heads=2 * H_C, dq=DH_C, dv=2 * DH_C, v_group=2, diff_params=diff_params,
               lambda_init=lambda_init, name="diff_attn")
    (h,) = _rows_call(_out_proj1_body, seq, 512, [h, o], [w_out.astype(bf16)],
                      [jax.ShapeDtypeStruct((seq, D_MODEL), f32)], "odd_out_proj")
    g_ffn = norm_ffn.reshape(1, -1)
    w_r = _pad_cols(w_router, LANES)
    w_r_hi = w_r.astype(bf16)
    w_r_lo = (w_r - w_r_hi.astype(f32)).astype(bf16)
    (comb,) = _rows_call(_router_body, seq, 512, [h], [g_ffn, w_r_hi, w_r_lo],
                         [jax.ShapeDtypeStruct((seq, LANES), f32)], "router")
    return _ffn(h, g_ffn, w_gate_e.astype(bf16), w_up_e.astype(bf16), w_down_e.astype(bf16),
                tm=min(1024, seq), tf=D_FF_E // 4, comb=comb, final_g=final_norm.reshape(1, -1), name="moe_ffn")


def kernel(x, ev_norm_mix, ev_w_in, ev_g_q_lat, ev_w_uq, ev_g_kv_lat, ev_w_ukv, ev_g_idx_k, ev_w_out, ev_norm_ffn, ev_w_gate, ev_w_up, ev_w_down, od_norm_mix, od_w_qkv, od_lambda_q1, od_lambda_k1, od_lambda_q2, od_lambda_k2, od_g_sub, od_w_out, od_norm_ffn, od_w_router, od_w_gate_e, od_w_up_e, od_w_down_e, final_norm):
    batch, seq, _ = x.shape
    assert batch == 1 and ev_w_in.shape[0] == 1 and od_w_qkv.shape[0] == 1
    h = x[0]
    h = _even_layer(h, ev_norm_mix[0], ev_w_in[0], ev_g_q_lat[0], ev_w_uq[0], ev_g_kv_lat[0], ev_w_ukv[0],
                    ev_g_idx_k[0], ev_w_out[0], ev_norm_ffn[0], ev_w_gate[0], ev_w_up[0], ev_w_down[0])
    h = _odd_layer(h, 1, od_norm_mix[0], od_w_qkv[0], od_lambda_q1[0], od_lambda_k1[0], od_lambda_q2[0],
                   od_lambda_k2[0], od_g_sub[0], od_w_out[0], od_norm_ffn[0], od_w_router[0], od_w_gate_e[0],
                   od_w_up_e[0], od_w_down_e[0], final_norm)
    return h[None]
```

```python
import functools
import math

import numpy as np
import jax
import jax.numpy as jnp
from jax import lax
from jax.experimental import pallas as pl
from jax.experimental.pallas import tpu as pltpu

f32 = jnp.float32
bf16 = jnp.bfloat16
i32 = jnp.int32

D_MODEL = 1024
CHUNK = 64
ROPE_THETA = 500000.0
NORM_EPS = 1e-6
NEG_INF = -1e30
LOG2E = math.log2(math.e)

H_A, Q_LORA, KV_LORA, NOPE_A, ROPE_A, V_A = 8, 256, 128, 64, 32, 64
H_B, DH_B, ROT_B = 8, 64, 16
H_IDX, D_IDX, ROT_IDX = 8, 32, 8
TOPK_MAX = 256
H_C, DH_C, ROT_C = 8, 64, 16
D_FF, N_EXP, D_FF_E = 2816, 8, 3584

LANES = 128
HEAD_PAD_A = 128

_NEG_BITS = int(np.float32(NEG_INF).view(np.int32))
NEG_KEY = _NEG_BITS ^ 0x7FFFFFFF
INT_MIN = -(2 ** 31)

VMEM_LIMIT = 56 * 1024 * 1024


def _cparams(sem):
    return pltpu.CompilerParams(dimension_semantics=sem, vmem_limit_bytes=VMEM_LIMIT)


def _rms(x, g):
    var = jnp.mean(x * x, axis=-1, keepdims=True)
    return x * lax.rsqrt(var + NORM_EPS) * g


def _dot(a, b):
    return jnp.dot(a, b, preferred_element_type=f32)


def _dot_nt(a, b):
    return lax.dot_general(a, b, (((1,), (1,)), ((), ())), preferred_element_type=f32)


def _rope_tables(seq, rot_dim, head_width, offset, width):
    pos = jnp.arange(seq, dtype=f32)
    inv_freq = ROPE_THETA ** (-jnp.arange(0, rot_dim, 2, dtype=f32) / rot_dim)
    ang = pos[:, None] * inv_freq[None, :]
    cos, sin = jnp.cos(ang), jnp.sin(ang)
    c = jnp.ones((seq, head_width), f32).at[:, offset:offset + rot_dim].set(jnp.concatenate([cos, cos], -1))
    s = jnp.zeros((seq, head_width), f32).at[:, offset:offset + rot_dim].set(jnp.concatenate([-sin, sin], -1))
    reps = width // head_width
    return jnp.tile(c, (1, reps)), jnp.tile(s, (1, reps))


def _swap_cols(w, head_width, offset, rot_dim):
    k, n = w.shape
    half = rot_dim // 2
    w3 = w.reshape(k, n // head_width, head_width)
    out = jnp.zeros_like(w3)
    out = out.at[:, :, offset:offset + half].set(w3[:, :, offset + half:offset + rot_dim])
    out = out.at[:, :, offset + half:offset + rot_dim].set(w3[:, :, offset:offset + half])
    return out.reshape(k, n)


def _pad_cols(w, width):
    return jnp.pad(w, ((0, 0), (0, width - w.shape[1])))


def _rows_call(body, seq, tm, row_ins, const_ins, out_sds, name):
    def rspec(a):
        return pl.BlockSpec((tm, a.shape[1]), lambda i: (i, 0))

    def cspec(a):
        nd = a.ndim
        return pl.BlockSpec(a.shape, lambda i: (0,) * nd)

    return pl.pallas_call(
        body,
        grid=(seq // tm,),
        in_specs=[rspec(a) for a in row_ins] + [cspec(a) for a in const_ins],
        out_specs=[pl.BlockSpec((tm, o.shape[1]), lambda i: (i, 0)) for o in out_sds],
        out_shape=out_sds,
        compiler_params=_cparams(("parallel",)),
        name=name,
    )(*row_ins, *const_ins)


def _even_proj_body(x_ref, ca_ref, sa_ref, cb_ref, sb_ref, ci_ref, si_ref, ckr_ref, skr_ref,
                    g_ref, gq_ref, gkv_ref, gi_ref, gisw_ref,
                    wlat_ref, wq_ref, wqsw_ref, wk_ref, wv_ref, place_ref,
                    wqb_ref, wqbsw_ref, wkb_ref, wkbsw_ref, wvb_ref, wqi_ref, wqisw_ref, wsm_ref, wwi_ref,
                    qa_ref, ka_ref, va_ref, qb_ref, kb_ref, vb_ref, qi_ref, ki_ref, wi_ref):
    xn = _rms(x_ref[...], g_ref[...]).astype(bf16)
    lat = _dot(xn, wlat_ref[...])
    cqn = _rms(lat[:, :Q_LORA], gq_ref[...]).astype(bf16)
    ckvn = _rms(lat[:, Q_LORA:], gkv_ref[...]).astype(bf16)
    reps_a = qa_ref.shape[1] // LANES
    ca = jnp.tile(ca_ref[...], (1, reps_a))
    sa = jnp.tile(sa_ref[...], (1, reps_a))
    qa_ref[...] = (_dot(cqn, wq_ref[...]) * ca + _dot(cqn, wqsw_ref[...]) * sa).astype(bf16)
    small = _dot(xn, wsm_ref[...])
    kr, kr_sw = small[:, 0:ROPE_A], small[:, ROPE_A:2 * ROPE_A]
    kpe = (kr * ckr_ref[...] + kr_sw * skr_ref[...]).astype(bf16)
    ka_ref[...] = (_dot(ckvn, wk_ref[...]) + _dot(kpe, place_ref[...])).astype(bf16)
    va_ref[...] = _dot(ckvn, wv_ref[...]).astype(bf16)
    reps_b = qb_ref.shape[1] // LANES
    cb = jnp.tile(cb_ref[...], (1, reps_b))
    sb = jnp.tile(sb_ref[...], (1, reps_b))
    qb_ref[...] = (_dot(xn, wqb_ref[...]) * cb + _dot(xn, wqbsw_ref[...]) * sb).astype(bf16)
    kb_ref[...] = (_dot(xn, wkb_ref[...]) * cb + _dot(xn, wkbsw_ref[...]) * sb).astype(bf16)
    vb_ref[...] = _dot(xn, wvb_ref[...]).astype(bf16)
    reps_i = qi_ref.shape[1] // LANES
    ci = jnp.tile(ci_ref[...], (1, reps_i))
    si = jnp.tile(si_ref[...], (1, reps_i))
    qi_ref[...] = (_dot(xn, wqi_ref[...]) * ci + _dot(xn, wqisw_ref[...]) * si).astype(bf16)
    ki, ki_sw = small[:, 2 * ROPE_A:2 * ROPE_A + D_IDX], small[:, 2 * ROPE_A + D_IDX:2 * ROPE_A + 2 * D_IDX]
    r = lax.rsqrt(jnp.mean(ki * ki, axis=-1, keepdims=True) + NORM_EPS)
    ci32, si32 = ci_ref[:, 0:D_IDX], si_ref[:, 0:D_IDX]
    ki_ref[...] = (ki * r * gi_ref[...] * ci32 + ki_sw * r * gisw_ref[...] * si32).astype(bf16)
    wi_ref[...] = _dot(xn, wwi_ref[...])


def _odd_proj_body(x_ref, cb_ref, sb_ref, g_ref, wq_ref, wqsw_ref, wk_ref, wksw_ref, wv_ref,
                   q_ref, k_ref, v_ref):
    xn = _rms(x_ref[...], g_ref[...]).astype(bf16)
    reps = q_ref.shape[1] // LANES
    cb = jnp.tile(cb_ref[...], (1, reps))
    sb = jnp.tile(sb_ref[...], (1, reps))
    q_ref[...] = (_dot(xn, wq_ref[...]) * cb + _dot(xn, wqsw_ref[...]) * sb).astype(bf16)
    k_ref[...] = (_dot(xn, wk_ref[...]) * cb + _dot(xn, wksw_ref[...]) * sb).astype(bf16)
    v_ref[...] = _dot(xn, wv_ref[...]).astype(bf16)


def _out_proj2_body(x_ref, a1_ref, a2_ref, w1_ref, w2_ref, o_ref):
    o_ref[...] = x_ref[...] + _dot(a1_ref[...], w1_ref[...]) + _dot(a2_ref[...], w2_ref[...])


def _out_proj1_body(x_ref, a_ref, w_ref, o_ref):
    o_ref[...] = x_ref[...] + _dot(a_ref[...], w_ref[...])


def _router_body(x_ref, g_ref, whi_ref, wlo_ref, xn_ref, comb_ref, rank_ref, cnt_ref, carry_scr):
    @pl.when(pl.program_id(0) == 0)
    def _():
        carry_scr[...] = jnp.zeros(carry_scr.shape, f32)

    xn = _rms(x_ref[...], g_ref[...])
    xn_ref[...] = xn.astype(bf16)
    hi = xn.astype(bf16)
    lo = (xn - hi.astype(f32)).astype(bf16)
    logits = _dot(hi, whi_ref[...]) + _dot(lo, whi_ref[...]) + _dot(hi, wlo_ref[...])
    lane = lax.broadcasted_iota(i32, logits.shape, 1).astype(f32)
    lg = jnp.where(lane < N_EXP, logits, -jnp.inf)
    m1 = jnp.max(lg, axis=1, keepdims=True)
    i1 = jnp.min(jnp.where(lg == m1, lane, float(LANES)), axis=1, keepdims=True)
    lg2 = jnp.where(lane == i1, -jnp.inf, lg)
    m2 = jnp.max(lg2, axis=1, keepdims=True)
    i2 = jnp.min(jnp.where(lg2 == m2, lane, float(LANES)), axis=1, keepdims=True)
    e2 = jnp.exp(m2 - m1)
    den = 1.0 + e2
    comb_ref[...] = jnp.where(lane == i1, 1.0 / den, 0.0) + jnp.where(lane == i2, e2 / den, 0.0)
    routed = jnp.logical_or(lane == i1, lane == i2)
    onehot = jnp.where(routed, 1.0, 0.0)
    tm = onehot.shape[0]
    earlier = lax.broadcasted_iota(i32, (tm, tm), 1) < lax.broadcasted_iota(i32, (tm, tm), 0)
    before = _dot(jnp.where(earlier, 1.0, 0.0).astype(bf16), onehot.astype(bf16))
    carry = carry_scr[0:1, :]
    rank_ref[...] = jnp.where(routed, before + carry, -1.0)
    carry = carry + jnp.sum(onehot, axis=0, keepdims=True)
    carry_scr[...] = jnp.broadcast_to(carry, carry_scr.shape)
    cnt_ref[0] = jnp.broadcast_to(carry, cnt_ref.shape[1:])


def _dsa_select_body(qi_ref, wi_ref, kit_ref, out_ref, keys_scr, *, tq, tk, top_k, idx_bits):
    q0 = pl.program_id(0) * tq
    n_kt = (q0 + tq + tk - 1) // tk
    qh = [qi_ref[:, h * D_IDX:(h + 1) * D_IDX] for h in range(H_IDX)]
    w = wi_ref[...]
    wb = [jnp.broadcast_to(w[:, h:h + 1], (tq, tk)) for h in range(H_IDX)]
    row = q0 + lax.broadcasted_iota(i32, (tq, 1), 0)
    row_lim = (row // CHUNK + 1) * CHUNK

    def cols_of(kt):
        c0 = pl.multiple_of(kt * tk, tk)
        return c0, c0 + lax.broadcasted_iota(i32, (tq, tk), 1)

    def score_tile(kt, carry):
        c0, col = cols_of(kt)
        kt_tile = kit_ref[:, pl.ds(c0, tk)]
        acc = jnp.zeros((tq, tk), f32)
        for h in range(H_IDX):
            acc = acc + jnp.maximum(_dot(qh[h], kt_tile), 0.0) * wb[h]
        sc = jnp.where(col < row_lim, acc, NEG_INF)
        bits = lax.bitcast_convert_type(sc, i32)
        keys_scr[:, pl.ds(c0, tk)] = jnp.where(bits < 0, bits ^ 0x7FFFFFFF, bits)
        return carry

    lax.fori_loop(0, n_kt, score_tile, 0)

    def count(pred):
        def body(kt, acc):
            c0, col = cols_of(kt)
            m = pred(keys_scr[:, pl.ds(c0, tk)], col).astype(i32)
            for u in range(tk // LANES):
                acc = acc + m[:, u * LANES:(u + 1) * LANES]
            return acc

        acc = lax.fori_loop(0, n_kt, body, jnp.zeros((tq, LANES), i32))
        return jnp.sum(acc.astype(f32), axis=1, keepdims=True).astype(i32)

    def thr_step(it, u):
        cand_u = u | lax.shift_left(jnp.int32(1), 31 - it)
        cand = cand_u ^ INT_MIN
        cnt = count(lambda ks, col: ks >= cand)
        return jnp.where(cnt >= top_k, cand_u, u)

    thr = lax.fori_loop(0, 32, thr_step, jnp.zeros((tq, 1), i32)) ^ INT_MIN
    n_ge = count(lambda ks, col: ks >= thr)
    thr_vis = jnp.maximum(thr, NEG_KEY + 1)
    excess = jnp.logical_and(n_ge > top_k, thr > NEG_KEY)
    any_excess = jnp.max(excess.astype(f32)) > 0.0
    out_ref[...] = jnp.full(out_ref.shape, NEG_INF, bf16)

    @pl.when(jnp.logical_not(any_excess))
    def _():
        def write_tile(kt, carry):
            c0, _ = cols_of(kt)
            ks = keys_scr[:, pl.ds(c0, tk)]
            out_ref[:, pl.ds(c0, tk)] = jnp.where(ks >= thr_vis, 0.0, NEG_INF).astype(bf16)
            return carry

        lax.fori_loop(0, n_kt, write_tile, 0)

    @pl.when(any_excess)
    def _():
        need = top_k - count(lambda ks, col: ks > thr)

        def tie_step(it, xv):
            cand = xv | lax.shift_left(jnp.int32(1), idx_bits - 1 - it)
            cnt = count(lambda ks, col: jnp.logical_and(ks == thr, col < cand))
            return jnp.where(cnt < need, cand, xv)

        xlim = lax.fori_loop(0, idx_bits, tie_step, jnp.zeros((tq, 1), i32))

        def write_tile(kt, carry):
            c0, col = cols_of(kt)
            ks = keys_scr[:, pl.ds(c0, tk)]
            sel = jnp.logical_or(ks > thr, jnp.logical_and(ks == thr, col <= xlim))
            sel = jnp.logical_and(sel, ks >= thr_vis)
            out_ref[:, pl.ds(c0, tk)] = jnp.where(sel, 0.0, NEG_INF).astype(bf16)
            return carry

        lax.fori_loop(0, n_kt, write_tile, 0)


def _dsa_select(qi, wi, kit, seq, top_k):
    tq, tk = min(256, seq), 512
    idx_bits = max(1, int(math.ceil(math.log2(seq))))
    body = functools.partial(_dsa_select_body, tq=tq, tk=tk, top_k=top_k, idx_bits=idx_bits)
    return pl.pallas_call(
        body,
        grid=(seq // tq,),
        in_specs=[pl.BlockSpec((tq, qi.shape[1]), lambda i: (i, 0)),
                  pl.BlockSpec((tq, wi.shape[1]), lambda i: (i, 0)),
                  pl.BlockSpec(kit.shape, lambda i: (0, 0))],
        out_specs=pl.BlockSpec((tq, seq), lambda i: (i, 0)),
        out_shape=jax.ShapeDtypeStruct((seq, seq), bf16),
        scratch_shapes=[pltpu.VMEM((tq, seq), i32)],
        compiler_params=_cparams(("parallel",)),
        name="dsa_select",
    )(qi, wi, kit)


FLAG_FIRST, FLAG_MASK, FLAG_LAST = 1, 2, 4


def _flash_body(it_ref, jt_ref, ft_ref, q_ref, k_ref, v_ref, *rest, tq, tk, n_heads, dq, dv, v_group, has_bias, diff,
                lambda_init):
    rest = list(rest)
    bias_ref = rest.pop(0) if has_bias else None
    if diff:
        lq1_ref, lk1_ref, lq2_ref, lk2_ref, gsub_ref = rest[:5]
        rest = rest[5:]
    o_ref, m_scr, l_scr, acc_scr = rest
    step = pl.program_id(0)
    i = it_ref[step]
    j = jt_ref[step]
    flag = ft_ref[step]

    @pl.when((flag & FLAG_FIRST) != 0)
    def _():
        m_scr[...] = jnp.full(m_scr.shape, NEG_INF, f32)
        l_scr[...] = jnp.zeros(l_scr.shape, f32)
        acc_scr[...] = jnp.zeros(acc_scr.shape, f32)

    def attend(bias):
        for h in range(n_heads):
            hv = h // v_group
            s = _dot_nt(q_ref[:, h * dq:(h + 1) * dq], k_ref[:, h * dq:(h + 1) * dq])
            if bias is not None:
                s = s + bias
            m_prev = m_scr[h]
            m_new = jnp.maximum(m_prev, jnp.max(s, axis=1, keepdims=True))
            alpha = jnp.exp2(m_prev - m_new)
            p = jnp.exp2(s - jnp.tile(m_new, (1, tk // LANES)))
            l_scr[h] = alpha * l_scr[h] + jnp.sum(p, axis=1, keepdims=True)
            acc_scr[h] = acc_scr[h] * alpha[:, :dv] + _dot(p.astype(bf16), v_ref[:, hv * dv:(hv + 1) * dv])
            m_scr[h] = m_new

    if has_bias:
        attend(bias_ref[...].astype(f32))
    else:
        @pl.when((flag & FLAG_MASK) == 0)
        def _():
            attend(None)

        @pl.when((flag & FLAG_MASK) != 0)
        def _():
            r = (i * tq + lax.broadcasted_iota(i32, (tq, tk), 0)) // CHUNK
            c = (j * tk + lax.broadcasted_iota(i32, (tq, tk), 1)) // CHUNK
            attend(jnp.where(c <= r, 0.0, NEG_INF))

    @pl.when((flag & FLAG_LAST) != 0)
    def _():
        if diff:
            lam = (jnp.exp(jnp.sum(lq1_ref[...] * lk1_ref[...], axis=1, keepdims=True))
                   - jnp.exp(jnp.sum(lq2_ref[...] * lk2_ref[...], axis=1, keepdims=True)) + lambda_init)
            for hc in range(n_heads // 2):
                o1 = acc_scr[2 * hc] / l_scr[2 * hc][:, :dv]
                o2 = acc_scr[2 * hc + 1] / l_scr[2 * hc + 1][:, :dv]
                o = _rms(o1 - lam * o2, gsub_ref[...]) * (1.0 - lambda_init)
                o_ref[:, hc * dv:(hc + 1) * dv] = o.astype(o_ref.dtype)
        else:
            for h in range(n_heads):
                o_ref[:, h * dv:(h + 1) * dv] = (acc_scr[h] / l_scr[h][:, :dv]).astype(o_ref.dtype)


FLASH_TILE_ELEMS = 1024 * 512
FLASH_STATS_BYTES = 12 * 1024 * 1024


def _flash_tiles(seq, n_heads, dv):
    tq = 1024
    while tq > 128 and n_heads * tq * (2 * LANES + dv) * 4 > FLASH_STATS_BYTES:
        tq //= 2
    return min(tq, seq), min(FLASH_TILE_ELEMS // tq, seq)


def _flash(q, k, v, *, n_heads, dq, dv, v_group=1, bias=None, diff_params=None, lambda_init=0.0, name):
    seq = q.shape[0]
    tq, tk = _flash_tiles(seq, n_heads, dv)
    pairs = []
    for i in range(seq // tq):
        j_last = ((i + 1) * tq - 1) // tk
        for j in range(j_last + 1):
            needs_mask = (j + 1) * tk > i * tq + CHUNK
            pairs.append((i, j, (FLAG_FIRST if j == 0 else 0) | (FLAG_MASK if needs_mask else 0)
                          | (FLAG_LAST if j == j_last else 0)))
    it = jnp.asarray([p[0] for p in pairs], i32)
    jt = jnp.asarray([p[1] for p in pairs], i32)
    ft = jnp.asarray([p[2] for p in pairs], i32)
    n_out = (n_heads // v_group) * dv
    in_specs = [pl.BlockSpec((tq, q.shape[1]), lambda s, it, jt, ft: (it[s], 0)),
                pl.BlockSpec((tk, k.shape[1]), lambda s, it, jt, ft: (jt[s], 0)),
                pl.BlockSpec((tk, v.shape[1]), lambda s, it, jt, ft: (jt[s], 0))]
    args = [q, k, v]
    if bias is not None:
        in_specs.append(pl.BlockSpec((tq, tk), lambda s, it, jt, ft: (it[s], jt[s])))
        args.append(bias)
    if diff_params is not None:
        for a in diff_params:
            in_specs.append(pl.BlockSpec(a.shape, lambda s, it, jt, ft: (0, 0)))
            args.append(a)
    body = functools.partial(_flash_body, tq=tq, tk=tk, n_heads=n_heads, dq=dq, dv=dv, v_group=v_group,
                             has_bias=bias is not None, diff=diff_params is not None, lambda_init=lambda_init)
    return pl.pallas_call(
        body,
        grid_spec=pltpu.PrefetchScalarGridSpec(
            num_scalar_prefetch=3,
            grid=(len(pairs),),
            in_specs=in_specs,
            out_specs=pl.BlockSpec((tq, n_out), lambda s, it, jt, ft: (it[s], 0)),
            scratch_shapes=[pltpu.VMEM((n_heads, tq, LANES), f32), pltpu.VMEM((n_heads, tq, LANES), f32),
                            pltpu.VMEM((n_heads, tq, dv), f32)]),
        out_shape=jax.ShapeDtypeStruct((seq, n_out), bf16),
        compiler_params=_cparams(("arbitrary",)),
        name=name,
    )(it, jt, ft, *args)


def _silu_mul(gate, up):
    return gate / (1.0 + jnp.exp(-gate)) * up


def _ffn_body(x_ref, g_ref, wg_ref, wu_ref, wd_ref, o_ref, xn_scr, acc_scr):
    f = pl.program_id(1)

    @pl.when(f == 0)
    def _():
        xn_scr[...] = _rms(x_ref[...], g_ref[...]).astype(bf16)
        acc_scr[...] = jnp.zeros(acc_scr.shape, f32)

    xn = xn_scr[...]
    act = _silu_mul(_dot(xn, wg_ref[...]), _dot(xn, wu_ref[...])).astype(bf16)
    acc_scr[...] += _dot(act, wd_ref[...])

    @pl.when(f == pl.num_programs(1) - 1)
    def _():
        o_ref[...] = x_ref[...] + acc_scr[...]


def _ffn(x, g, wg, wu, wd, *, tm, tf, name):
    seq = x.shape[0]
    dff = wg.shape[1]
    return pl.pallas_call(
        _ffn_body,
        grid=(seq // tm, dff // tf),
        in_specs=[pl.BlockSpec((tm, D_MODEL), lambda i, f: (i, 0)),
                  pl.BlockSpec((1, D_MODEL), lambda i, f: (0, 0)),
                  pl.BlockSpec((D_MODEL, tf), lambda i, f: (0, f)),
                  pl.BlockSpec((D_MODEL, tf), lambda i, f: (0, f)),
                  pl.BlockSpec((tf, D_MODEL), lambda i, f: (f, 0))],
        out_specs=pl.BlockSpec((tm, D_MODEL), lambda i, f: (i, 0)),
        out_shape=jax.ShapeDtypeStruct((seq, D_MODEL), f32),
        scratch_shapes=[pltpu.VMEM((tm, D_MODEL), bf16), pltpu.VMEM((tm, D_MODEL), f32)],
        compiler_params=_cparams(("parallel", "arbitrary")),
        name=name,
    )(x, g, wg, wu, wd)


MOE_TILE = 512
PAIR_FIRST, PAIR_VALID, PAIR_LAST = 1, 2, 4


def _route(x, g, w_router):
    seq = x.shape[0]
    tm = min(MOE_TILE, seq)
    nb = seq // tm
    w_r = _pad_cols(w_router, LANES)
    w_hi = w_r.astype(bf16)
    w_lo = (w_r - w_hi.astype(f32)).astype(bf16)
    row = lambda n: pl.BlockSpec((tm, n), lambda i: (i, 0))
    const = lambda a: pl.BlockSpec(a.shape, lambda i: (0, 0))
    return pl.pallas_call(
        _router_body,
        grid=(nb,),
        in_specs=[row(D_MODEL), const(g), const(w_hi), const(w_lo)],
        out_specs=[row(D_MODEL), row(LANES), row(LANES), pl.BlockSpec((1, 8, LANES), lambda i: (i, 0, 0))],
        out_shape=[jax.ShapeDtypeStruct((seq, D_MODEL), bf16), jax.ShapeDtypeStruct((seq, LANES), f32),
                   jax.ShapeDtypeStruct((seq, LANES), f32), jax.ShapeDtypeStruct((nb, 8, LANES), f32)],
        scratch_shapes=[pltpu.VMEM((8, LANES), f32)],
        compiler_params=_cparams(("arbitrary",)),
        name="router",
    )(x, g, w_hi, w_lo)


def _moe_schedule(counts_after, seq):
    tm = min(MOE_TILE, seq)
    nb = seq // tm
    n_tiles = 2 * nb + N_EXP
    kmax = nb + 1
    max_pairs = n_tiles + N_EXP * nb
    bounds = jnp.concatenate([jnp.zeros((1, N_EXP), i32), counts_after[:, 0, :N_EXP].astype(i32)], 0)
    cnt = bounds[-1]
    ntile = (cnt + tm - 1) // tm
    tile_end = jnp.cumsum(ntile)
    tile_start = tile_end - ntile
    n_valid = tile_end[-1]
    p_ids = jnp.minimum(jnp.arange(n_tiles, dtype=i32), n_valid - 1)
    tile_expert = jnp.minimum(jnp.sum(p_ids[:, None] >= tile_end[None, :], axis=1), N_EXP - 1).astype(i32)
    tile_valid = (jnp.arange(n_tiles, dtype=i32) < n_valid).astype(i32)

    lo = jnp.transpose(bounds[:-1])[:, None, :]
    hi = jnp.transpose(bounds[1:])[:, None, :]
    k0 = (jnp.arange(kmax, dtype=i32) * tm)[None, :, None]
    meet = jnp.maximum(k0, lo) < jnp.minimum(k0 + tm, hi)
    n_pairs = jnp.sum(meet)
    s_ids = jnp.arange(max_pairs, dtype=i32)
    s_eff = jnp.minimum(s_ids, n_pairs - 1)

    def pair_list(flat, decode, group_of):
        idx = jnp.nonzero(flat, size=max_pairs, fill_value=0)[0].astype(i32)[s_eff]
        e, k, b = decode(idx)
        p = tile_start[e] + k
        grp = group_of(p, b)
        valid = s_ids < n_pairs
        first = jnp.concatenate([jnp.ones((1,), bool), grp[1:] != grp[:-1]])
        last = jnp.concatenate([grp[1:] != grp[:-1], jnp.ones((1,), bool)]) | (s_ids == n_pairs - 1)
        flags = jnp.where(valid, PAIR_VALID + PAIR_FIRST * first + PAIR_LAST * last, 0).astype(i32)
        return p.astype(i32), b.astype(i32), e.astype(i32), k.astype(i32), flags

    tile_major = pair_list(meet.reshape(-1),
                           lambda i: (i // (kmax * nb), (i // nb) % kmax, i % nb), lambda p, b: p)
    block_major = pair_list(jnp.transpose(meet, (2, 0, 1)).reshape(-1),
                            lambda i: ((i // kmax) % N_EXP, i % kmax, i // (N_EXP * kmax)), lambda p, b: b)
    return n_tiles, tile_expert, tile_valid, tile_major, block_major


def _moe_gather_body(pt, pb, pe, pk, pf, xn_ref, rank_t_ref, o_ref):
    s = pl.program_id(0)
    flag = pf[s]
    tmg, tb = o_ref.shape[0], xn_ref.shape[0]

    @pl.when((flag & PAIR_FIRST) != 0)
    def _():
        o_ref[...] = jnp.zeros(o_ref.shape, o_ref.dtype)

    @pl.when((flag & PAIR_VALID) != 0)
    def _():
        r = rank_t_ref[pl.ds(pe[s], 1), :] - (pk[s] * tmg).astype(f32)
        rows = lax.broadcasted_iota(i32, (tmg, tb), 0).astype(f32)
        onehot = jnp.where(rows == r, 1.0, 0.0).astype(bf16)
        o_ref[...] = o_ref[...] + _dot(onehot, xn_ref[...]).astype(o_ref.dtype)


def _moe_ffn_body(te, tv, x_ref, wg_ref, wu_ref, wd_ref, y_ref, acc_scr):
    p = pl.program_id(0)
    f = pl.program_id(1)
    last_f = f == pl.num_programs(1) - 1

    @pl.when(tv[p] != 0)
    def _():
        @pl.when(f == 0)
        def _():
            acc_scr[...] = jnp.zeros(acc_scr.shape, f32)

        x = x_ref[...]
        act = _silu_mul(_dot(x, wg_ref[0]), _dot(x, wu_ref[0])).astype(bf16)
        acc_scr[...] += _dot(act, wd_ref[0])

        @pl.when(last_f)
        def _():
            y_ref[...] = acc_scr[...].astype(y_ref.dtype)

    @pl.when(jnp.logical_and(tv[p] == 0, last_f))
    def _():
        y_ref[...] = jnp.zeros(y_ref.shape, y_ref.dtype)


def _moe_combine_body(ct, cb, ce, ck, cf, h_ref, y_ref, rank_ref, comb_ref, fg_ref, o_ref, acc_scr):
    s = pl.program_id(0)
    flag = cf[s]
    tb, tmg = h_ref.shape[0], y_ref.shape[0]

    @pl.when((flag & PAIR_FIRST) != 0)
    def _():
        acc_scr[...] = h_ref[...]

    @pl.when((flag & PAIR_VALID) != 0)
    def _():
        lane = lax.broadcasted_iota(i32, rank_ref.shape, 1)
        mine = lane == ce[s]
        r = jnp.sum(jnp.where(mine, rank_ref[...], 0.0), axis=1, keepdims=True) - (ck[s] * tmg).astype(f32)
        gate = jnp.sum(jnp.where(mine, comb_ref[...], 0.0), axis=1, keepdims=True)
        cols = lax.broadcasted_iota(i32, (tb, tmg), 1).astype(f32)
        onehot = jnp.where(cols == r, 1.0, 0.0).astype(bf16)
        acc_scr[...] += gate * _dot(onehot, y_ref[...])

    @pl.when((flag & PAIR_LAST) != 0)
    def _():
        o_ref[...] = _rms(acc_scr[...], fg_ref[...])


def _moe(h, g_ffn, w_router, w_gate_e, w_up_e, w_down_e, final_g):
    seq = h.shape[0]
    tm = min(MOE_TILE, seq)
    xn, comb, rank, counts_after = _route(h, g_ffn, w_router)
    n_tiles, tile_expert, tile_valid, tile_major, block_major = _moe_schedule(counts_after, seq)
    n_pairs = tile_major[0].shape[0]
    rank_t = jnp.transpose(rank[:, :8])

    x_sorted = pl.pallas_call(
        _moe_gather_body,
        grid_spec=pltpu.PrefetchScalarGridSpec(
            num_scalar_prefetch=5, grid=(n_pairs,),
            in_specs=[pl.BlockSpec((tm, D_MODEL), lambda s, pt, pb, pe, pk, pf: (pb[s], 0)),
                      pl.BlockSpec((8, tm), lambda s, pt, pb, pe, pk, pf: (0, pb[s]))],
            out_specs=pl.BlockSpec((tm, D_MODEL), lambda s, pt, pb, pe, pk, pf: (pt[s], 0))),
        out_shape=jax.ShapeDtypeStruct((n_tiles * tm, D_MODEL), bf16),
        compiler_params=_cparams(("arbitrary",)),
        name="moe_gather",
    )(*tile_major, xn, rank_t)

    tf = D_FF_E // 4
    n_f = D_FF_E // tf
    f_eff = lambda f, v: f * v + (n_f - 1) * (1 - v)
    y_sorted = pl.pallas_call(
        _moe_ffn_body,
        grid_spec=pltpu.PrefetchScalarGridSpec(
            num_scalar_prefetch=2, grid=(n_tiles, n_f),
            in_specs=[pl.BlockSpec((tm, D_MODEL), lambda p, f, te, tv: (p, 0)),
                      pl.BlockSpec((1, D_MODEL, tf), lambda p, f, te, tv: (te[p], 0, f_eff(f, tv[p]))),
                      pl.BlockSpec((1, D_MODEL, tf), lambda p, f, te, tv: (te[p], 0, f_eff(f, tv[p]))),
                      pl.BlockSpec((1, tf, D_MODEL), lambda p, f, te, tv: (te[p], f_eff(f, tv[p]), 0))],
            out_specs=pl.BlockSpec((tm, D_MODEL), lambda p, f, te, tv: (p, 0)),
            scratch_shapes=[pltpu.VMEM((tm, D_MODEL), f32)]),
        out_shape=jax.ShapeDtypeStruct((n_tiles * tm, D_MODEL), bf16),
        compiler_params=_cparams(("arbitrary", "arbitrary")),
        name="moe_ffn",
    )(tile_expert, tile_valid, x_sorted, w_gate_e.astype(bf16), w_up_e.astype(bf16), w_down_e.astype(bf16))

    return pl.pallas_call(
        _moe_combine_body,
        grid_spec=pltpu.PrefetchScalarGridSpec(
            num_scalar_prefetch=5, grid=(n_pairs,),
            in_specs=[pl.BlockSpec((tm, D_MODEL), lambda s, ct, cb, ce, ck, cf: (cb[s], 0)),
                      pl.BlockSpec((tm, D_MODEL), lambda s, ct, cb, ce, ck, cf: (ct[s], 0)),
                      pl.BlockSpec((tm, LANES), lambda s, ct, cb, ce, ck, cf: (cb[s], 0)),
                      pl.BlockSpec((tm, LANES), lambda s, ct, cb, ce, ck, cf: (cb[s], 0)),
                      pl.BlockSpec((1, D_MODEL), lambda s, ct, cb, ce, ck, cf: (0, 0))],
            out_specs=pl.BlockSpec((tm, D_MODEL), lambda s, ct, cb, ce, ck, cf: (cb[s], 0)),
            scratch_shapes=[pltpu.VMEM((tm, D_MODEL), f32)]),
        out_shape=jax.ShapeDtypeStruct((seq, D_MODEL), f32),
        compiler_params=_cparams(("arbitrary",)),
        name="moe_combine",
    )(*block_major, h, y_sorted, rank, comb, final_g)


def _even_layer(h, norm_mix, w_in, g_q_lat, w_uq, g_kv_lat, w_ukv, g_idx_k, w_out, norm_ffn, w_gate, w_up, w_down):
    seq = h.shape[0]
    sizes = (Q_LORA, KV_LORA, ROPE_A, H_B * DH_B, H_B * DH_B, H_B * DH_B, H_IDX * D_IDX, D_IDX, H_IDX)
    offs = np.cumsum((0,) + sizes)
    w_cq, w_ckv, w_kr, w_qb, w_kb, w_vb, w_qi, w_ki, w_wi = [w_in[:, offs[n]:offs[n + 1]] for n in range(9)]

    scale_a = (NOPE_A + ROPE_A) ** -0.5 * LOG2E
    wq3 = (w_uq * scale_a).reshape(Q_LORA, H_A, NOPE_A + ROPE_A)
    wq = jnp.pad(wq3, ((0, 0), (0, 0), (0, HEAD_PAD_A - NOPE_A - ROPE_A))).reshape(Q_LORA, H_A * HEAD_PAD_A)
    wq_sw = _swap_cols(wq, HEAD_PAD_A, NOPE_A, ROPE_A)
    wkv3 = w_ukv.reshape(KV_LORA, H_A, NOPE_A + V_A)
    wk = jnp.pad(wkv3[:, :, :NOPE_A], ((0, 0), (0, 0), (0, HEAD_PAD_A - NOPE_A))).reshape(KV_LORA, H_A * HEAD_PAD_A)
    wv = wkv3[:, :, NOPE_A:].reshape(KV_LORA, H_A * V_A)
    place = jnp.zeros((ROPE_A, H_A, HEAD_PAD_A), f32)
    place = place.at[:, :, NOPE_A:NOPE_A + ROPE_A].set(jnp.eye(ROPE_A, dtype=f32)[:, None, :])
    place = place.reshape(ROPE_A, H_A * HEAD_PAD_A)

    w_qb = w_qb * (DH_B ** -0.5 * LOG2E)
    w_qi = w_qi * D_IDX ** -0.5
    w_small = jnp.concatenate([w_kr, _swap_cols(w_kr, ROPE_A, 0, ROPE_A), w_ki, _swap_cols(w_ki, D_IDX, 0, ROT_IDX)], 1)
    w_wi_p = _pad_cols(w_wi * H_IDX ** -0.5, LANES)
    g_idx = g_idx_k.reshape(1, D_IDX)
    g_idx_sw = jnp.concatenate([g_idx[:, ROT_IDX // 2:ROT_IDX], g_idx[:, :ROT_IDX // 2], g_idx[:, ROT_IDX:]], 1)

    ca, sa = _rope_tables(seq, ROPE_A, HEAD_PAD_A, NOPE_A, LANES)
    cb, sb = _rope_tables(seq, ROT_B, DH_B, 0, LANES)
    ci, si = _rope_tables(seq, ROT_IDX, D_IDX, 0, LANES)
    ckr, skr = _rope_tables(seq, ROPE_A, ROPE_A, 0, ROPE_A)

    consts = [norm_mix.reshape(1, -1), g_q_lat.reshape(1, -1), g_kv_lat.reshape(1, -1), g_idx, g_idx_sw]
    weights = [jnp.concatenate([w_cq, w_ckv], 1), wq, wq_sw, wk, wv, place,
               w_qb, _swap_cols(w_qb, DH_B, 0, ROT_B), w_kb, _swap_cols(w_kb, DH_B, 0, ROT_B), w_vb,
               w_qi, _swap_cols(w_qi, D_IDX, 0, ROT_IDX), w_small, w_wi_p]
    weights = [w.astype(bf16) for w in weights]
    sds = lambda n, dt: jax.ShapeDtypeStruct((seq, n), dt)
    outs = [sds(H_A * HEAD_PAD_A, bf16), sds(H_A * HEAD_PAD_A, bf16), sds(H_A * V_A, bf16),
            sds(H_B * DH_B, bf16), sds(H_B * DH_B, bf16), sds(H_B * DH_B, bf16),
            sds(H_IDX * D_IDX, bf16), sds(D_IDX, bf16), sds(LANES, f32)]
    qa, ka, va, qb, kb, vb, qi, ki, wi = _rows_call(
        _even_proj_body, seq, 256, [h, ca, sa, cb, sb, ci, si, ckr, skr], consts + weights, outs, "even_proj")

    o_a = _flash(qa, ka, va, n_heads=H_A, dq=HEAD_PAD_A, dv=V_A, name="mla_attn")
    top_k = min(TOPK_MAX, seq // 4)
    bias = _dsa_select(qi, wi, ki.T, seq, top_k)
    o_b = _flash(qb, kb, vb, n_heads=H_B, dq=DH_B, dv=DH_B, bias=bias, name="dsa_attn")

    w_out = w_out.astype(bf16)
    n_a = H_A * V_A
    (h,) = _rows_call(_out_proj2_body, seq, 512, [h, o_a, o_b], [w_out[:n_a], w_out[n_a:]],
                      [jax.ShapeDtypeStruct((seq, D_MODEL), f32)], "even_out_proj")
    return _ffn(h, norm_ffn.reshape(1, -1), w_gate.astype(bf16), w_up.astype(bf16), w_down.astype(bf16),
                tm=512, tf=D_FF // 2, name="dense_ffn")


def _odd_layer(h, layer, norm_mix, w_qkv, lq1, lk1, lq2, lk2, g_sub, w_out, norm_ffn, w_router, w_gate_e, w_up_e,
               w_down_e, final_norm):
    seq = h.shape[0]
    lambda_init = 0.8 - 0.6 * math.exp(-0.3 * layer)
    n = H_C * 2 * DH_C
    w_q = w_qkv[:, :n] * (DH_C ** -0.5 * LOG2E)
    w_k = w_qkv[:, n:2 * n]
    w_v = w_qkv[:, 2 * n:]
    cb, sb = _rope_tables(seq, ROT_C, DH_C, 0, LANES)
    weights = [w_q, _swap_cols(w_q, DH_C, 0, ROT_C), w_k, _swap_cols(w_k, DH_C, 0, ROT_C), w_v]
    weights = [w.astype(bf16) for w in weights]
    sds = jax.ShapeDtypeStruct((seq, n), bf16)
    q, k, v = _rows_call(_odd_proj_body, seq, 512, [h, cb, sb], [norm_mix.reshape(1, -1)] + weights,
                         [sds, sds, sds], "odd_proj")
    diff_params = [lq1.reshape(1, -1), lk1.reshape(1, -1), lq2.reshape(1, -1), lk2.reshape(1, -1),
                   g_sub.reshape(1, -1)]
    o = _flash(q, k, v, n_heads=2 * H_C, dq=DH_C, dv=2 * DH_C, v_group=2, diff_params=diff_params,
               lambda_init=lambda_init, name="diff_attn")
    (h,) = _rows_call(_out_proj1_body, seq, 512, [h, o], [w_out.astype(bf16)],
                      [jax.ShapeDtypeStruct((seq, D_MODEL), f32)], "odd_out_proj")
    return _moe(h, norm_ffn.reshape(1, -1), w_router, w_gate_e, w_up_e, w_down_e, final_norm.reshape(1, -1))


def kernel(x, ev_norm_mix, ev_w_in, ev_g_q_lat, ev_w_uq, ev_g_kv_lat, ev_w_ukv, ev_g_idx_k, ev_w_out, ev_norm_ffn, ev_w_gate, ev_w_up, ev_w_down, od_norm_mix, od_w_qkv, od_lambda_q1, od_lambda_k1, od_lambda_q2, od_lambda_k2, od_g_sub, od_w_out, od_norm_ffn, od_w_router, od_w_gate_e, od_w_up_e, od_w_down_e, final_norm):
    batch, seq, _ = x.shape
    assert batch == 1 and ev_w_in.shape[0] == 1 and od_w_qkv.shape[0] == 1
    h = x[0]
    h = _even_layer(h, ev_norm_mix[0], ev_w_in[0], ev_g_q_lat[0], ev_w_uq[0], ev_g_kv_lat[0], ev_w_ukv[0],
                    ev_g_idx_k[0], ev_w_out[0], ev_norm_ffn[0], ev_w_gate[0], ev_w_up[0], ev_w_down[0])
    h = _odd_layer(h, 1, od_norm_mix[0], od_w_qkv[0], od_lambda_q1[0], od_lambda_k1[0], od_lambda_q2[0],
                   od_lambda_k2[0], od_g_sub[0], od_w_out[0], od_norm_ffn[0], od_w_router[0], od_w_gate_e[0],
                   od_w_up_e[0], od_w_down_e[0], final_norm)
    return h[None]
```

```python
import functools
import math

import numpy as np
import jax
import jax.numpy as jnp
from jax import lax
from jax.experimental import pallas as pl
from jax.experimental.pallas import tpu as pltpu

f32 = jnp.float32
bf16 = jnp.bfloat16
i32 = jnp.int32

D_MODEL = 1024
CHUNK = 64
ROPE_THETA = 500000.0
NORM_EPS = 1e-6
NEG_INF = -1e30
LOG2E = math.log2(math.e)

H_A, Q_LORA, KV_LORA, NOPE_A, ROPE_A, V_A = 8, 256, 128, 64, 32, 64
H_B, DH_B, ROT_B = 8, 64, 16
H_IDX, D_IDX, ROT_IDX = 8, 32, 8
TOPK_MAX = 256
H_C, DH_C, ROT_C = 8, 64, 16
D_FF, N_EXP, D_FF_E = 2816, 8, 3584

LANES = 128
HEAD_PAD_A = 128

_NEG_BITS = int(np.float32(NEG_INF).view(np.int32))
NEG_KEY = _NEG_BITS ^ 0x7FFFFFFF
INT_MIN = -(2 ** 31)

VMEM_LIMIT = 56 * 1024 * 1024


def _cparams(sem):
    return pltpu.CompilerParams(dimension_semantics=sem, vmem_limit_bytes=VMEM_LIMIT)


def _rms(x, g):
    var = jnp.mean(x * x, axis=-1, keepdims=True)
    return x * lax.rsqrt(var + NORM_EPS) * g


def _dot(a, b):
    return jnp.dot(a, b, preferred_element_type=f32)


def _dot_nt(a, b):
    return lax.dot_general(a, b, (((1,), (1,)), ((), ())), preferred_element_type=f32)


def _rope_tables(seq, rot_dim, head_width, offset, width):
    pos = jnp.arange(seq, dtype=f32)
    inv_freq = ROPE_THETA ** (-jnp.arange(0, rot_dim, 2, dtype=f32) / rot_dim)
    ang = pos[:, None] * inv_freq[None, :]
    cos, sin = jnp.cos(ang), jnp.sin(ang)
    c = jnp.ones((seq, head_width), f32).at[:, offset:offset + rot_dim].set(jnp.concatenate([cos, cos], -1))
    s = jnp.zeros((seq, head_width), f32).at[:, offset:offset + rot_dim].set(jnp.concatenate([-sin, sin], -1))
    reps = width // head_width
    return jnp.tile(c, (1, reps)), jnp.tile(s, (1, reps))


def _swap_cols(w, head_width, offset, rot_dim):
    k, n = w.shape
    half = rot_dim // 2
    w3 = w.reshape(k, n // head_width, head_width)
    out = jnp.zeros_like(w3)
    out = out.at[:, :, offset:offset + half].set(w3[:, :, offset + half:offset + rot_dim])
    out = out.at[:, :, offset + half:offset + rot_dim].set(w3[:, :, offset:offset + half])
    return out.reshape(k, n)


def _pad_cols(w, width):
    return jnp.pad(w, ((0, 0), (0, width - w.shape[1])))


def _rows_call(body, seq, tm, row_ins, const_ins, out_sds, name):
    def rspec(a):
        return pl.BlockSpec((tm, a.shape[1]), lambda i: (i, 0))

    def cspec(a):
        nd = a.ndim
        return pl.BlockSpec(a.shape, lambda i: (0,) * nd)

    return pl.pallas_call(
        body,
        grid=(seq // tm,),
        in_specs=[rspec(a) for a in row_ins] + [cspec(a) for a in const_ins],
        out_specs=[pl.BlockSpec((tm, o.shape[1]), lambda i: (i, 0)) for o in out_sds],
        out_shape=out_sds,
        compiler_params=_cparams(("parallel",)),
        name=name,
    )(*row_ins, *const_ins)


def _even_proj_body(x_ref, ca_ref, sa_ref, cb_ref, sb_ref, ci_ref, si_ref, ckr_ref, skr_ref,
                    g_ref, gq_ref, gkv_ref, gi_ref, gisw_ref, onea_ref, oneb_ref,
                    wlat_ref, wq_ref, wqsw_ref, wk_ref, wv_ref, place_ref,
                    wqb_ref, wqbsw_ref, wkb_ref, wkbsw_ref, wvb_ref, wqi_ref, wqisw_ref, wsm_ref, wwi_ref,
                    qa_ref, ka_ref, va_ref, qb_ref, kb_ref, vb_ref, qi_ref, ki_ref, wi_ref):
    xn = _rms(x_ref[...], g_ref[...]).astype(bf16)
    lat = _dot(xn, wlat_ref[...])
    cqn = _rms(lat[:, :Q_LORA], gq_ref[...]).astype(bf16)
    ckvn = _rms(lat[:, Q_LORA:], gkv_ref[...]).astype(bf16)
    reps_a = qa_ref.shape[1] // LANES
    ca = jnp.tile(ca_ref[...], (1, reps_a))
    sa = jnp.tile(sa_ref[...], (1, reps_a))
    qa_ref[...] = (_dot(cqn, wq_ref[...]) * ca + _dot(cqn, wqsw_ref[...]) * sa).astype(bf16)
    small = _dot(xn, wsm_ref[...])
    kr, kr_sw = small[:, 0:ROPE_A], small[:, ROPE_A:2 * ROPE_A]
    kpe = (kr * ckr_ref[...] + kr_sw * skr_ref[...]).astype(bf16)
    ka_ref[...] = (_dot(ckvn, wk_ref[...]) + _dot(kpe, place_ref[...])).astype(bf16)
    va_ref[...] = (_dot(ckvn, wv_ref[...]) + onea_ref[...]).astype(bf16)
    reps_b = qb_ref.shape[1] // LANES
    cb = jnp.tile(cb_ref[...], (1, reps_b))
    sb = jnp.tile(sb_ref[...], (1, reps_b))
    qb_ref[...] = (_dot(xn, wqb_ref[...]) * cb + _dot(xn, wqbsw_ref[...]) * sb).astype(bf16)
    kb_ref[...] = (_dot(xn, wkb_ref[...]) * cb + _dot(xn, wkbsw_ref[...]) * sb).astype(bf16)
    vb_ref[...] = (_dot(xn, wvb_ref[...]) + oneb_ref[...]).astype(bf16)
    reps_i = qi_ref.shape[1] // LANES
    ci = jnp.tile(ci_ref[...], (1, reps_i))
    si = jnp.tile(si_ref[...], (1, reps_i))
    qi_ref[...] = (_dot(xn, wqi_ref[...]) * ci + _dot(xn, wqisw_ref[...]) * si).astype(bf16)
    ki, ki_sw = small[:, 2 * ROPE_A:2 * ROPE_A + D_IDX], small[:, 2 * ROPE_A + D_IDX:2 * ROPE_A + 2 * D_IDX]
    r = lax.rsqrt(jnp.mean(ki * ki, axis=-1, keepdims=True) + NORM_EPS)
    ci32, si32 = ci_ref[:, 0:D_IDX], si_ref[:, 0:D_IDX]
    ki_ref[...] = (ki * r * gi_ref[...] * ci32 + ki_sw * r * gisw_ref[...] * si32).astype(bf16)
    wi_ref[...] = _dot(xn, wwi_ref[...])


def _odd_proj_body(x_ref, cb_ref, sb_ref, g_ref, one_ref, wq_ref, wqsw_ref, wk_ref, wksw_ref, wv_ref,
                   q_ref, k_ref, v_ref):
    xn = _rms(x_ref[...], g_ref[...]).astype(bf16)
    reps = q_ref.shape[1] // LANES
    cb = jnp.tile(cb_ref[...], (1, reps))
    sb = jnp.tile(sb_ref[...], (1, reps))
    q_ref[...] = (_dot(xn, wq_ref[...]) * cb + _dot(xn, wqsw_ref[...]) * sb).astype(bf16)
    k_ref[...] = (_dot(xn, wk_ref[...]) * cb + _dot(xn, wksw_ref[...]) * sb).astype(bf16)
    v_ref[...] = (_dot(xn, wv_ref[...]) + one_ref[...]).astype(bf16)


def _out_proj2_body(x_ref, a1_ref, a2_ref, w1_ref, w2_ref, o_ref):
    o_ref[...] = x_ref[...] + _dot(a1_ref[...], w1_ref[...]) + _dot(a2_ref[...], w2_ref[...])


def _out_proj1_body(x_ref, a_ref, w_ref, o_ref):
    o_ref[...] = x_ref[...] + _dot(a_ref[...], w_ref[...])


def _router_body(x_ref, g_ref, whi_ref, wlo_ref, xn_ref, comb_ref, rank_ref, cnt_ref, carry_scr):
    @pl.when(pl.program_id(0) == 0)
    def _():
        carry_scr[...] = jnp.zeros(carry_scr.shape, f32)

    xn = _rms(x_ref[...], g_ref[...])
    xn_ref[...] = xn.astype(bf16)
    hi = xn.astype(bf16)
    lo = (xn - hi.astype(f32)).astype(bf16)
    logits = _dot(hi, whi_ref[...]) + _dot(lo, whi_ref[...]) + _dot(hi, wlo_ref[...])
    lane = lax.broadcasted_iota(i32, logits.shape, 1).astype(f32)
    lg = jnp.where(lane < N_EXP, logits, -jnp.inf)
    m1 = jnp.max(lg, axis=1, keepdims=True)
    i1 = jnp.min(jnp.where(lg == m1, lane, float(LANES)), axis=1, keepdims=True)
    lg2 = jnp.where(lane == i1, -jnp.inf, lg)
    m2 = jnp.max(lg2, axis=1, keepdims=True)
    i2 = jnp.min(jnp.where(lg2 == m2, lane, float(LANES)), axis=1, keepdims=True)
    e2 = jnp.exp(m2 - m1)
    den = 1.0 + e2
    comb_ref[...] = jnp.where(lane == i1, 1.0 / den, 0.0) + jnp.where(lane == i2, e2 / den, 0.0)
    routed = jnp.logical_or(lane == i1, lane == i2)
    onehot = jnp.where(routed, 1.0, 0.0)
    tm = onehot.shape[0]
    earlier = lax.broadcasted_iota(i32, (tm, tm), 1) < lax.broadcasted_iota(i32, (tm, tm), 0)
    before = _dot(jnp.where(earlier, 1.0, 0.0).astype(bf16), onehot.astype(bf16))
    carry = carry_scr[0:1, :]
    rank_ref[...] = jnp.where(routed, before + carry, -1.0)
    carry = carry + jnp.sum(onehot, axis=0, keepdims=True)
    carry_scr[...] = jnp.broadcast_to(carry, carry_scr.shape)
    cnt_ref[0] = jnp.broadcast_to(carry, cnt_ref.shape[1:])


SEL_COUNT_ROWS, SEL_COUNT_COLS = 64, 1024


def _dsa_select_body(qi_ref, wi_ref, kit_ref, out_ref, keys_scr, *, tq, tk, top_k, idx_bits):
    q0 = pl.program_id(0) * tq
    n_ktc = (q0 + tq + SEL_COUNT_COLS - 1) // SEL_COUNT_COLS
    n_kt = n_ktc * (SEL_COUNT_COLS // tk)
    qh = [qi_ref[:, h * D_IDX:(h + 1) * D_IDX] for h in range(H_IDX)]
    w = wi_ref[...]
    wb = [jnp.broadcast_to(w[:, h:h + 1], (tq, tk)) for h in range(H_IDX)]
    row = q0 + lax.broadcasted_iota(i32, (tq, 1), 0)
    row_lim = (row // CHUNK + 1) * CHUNK

    def cols_of(kt):
        c0 = pl.multiple_of(kt * tk, tk)
        return c0, c0 + lax.broadcasted_iota(i32, (tq, tk), 1)

    def score_tile(kt, carry):
        c0, col = cols_of(kt)
        kt_tile = kit_ref[:, pl.ds(c0, tk)]
        acc = jnp.zeros((tq, tk), f32)
        for h in range(H_IDX):
            acc = acc + jnp.maximum(_dot(qh[h], kt_tile), 0.0) * wb[h]
        sc = jnp.where(col < row_lim, acc, NEG_INF)
        bits = lax.bitcast_convert_type(sc, i32)
        keys_scr[:, pl.ds(c0, tk)] = jnp.where(bits < 0, bits ^ 0x7FFFFFFF, bits)
        return carry

    lax.fori_loop(0, n_kt, score_tile, 0)

    def count(pred):
        def body(kt, acc):
            c0, col = cols_of(kt)
            m = pred(keys_scr[:, pl.ds(c0, tk)], col).astype(i32)
            for u in range(tk // LANES):
                acc = acc + m[:, u * LANES:(u + 1) * LANES]
            return acc

        acc = lax.fori_loop(0, n_kt, body, jnp.zeros((tq, LANES), i32))
        return jnp.sum(acc.astype(f32), axis=1, keepdims=True).astype(i32)

    ones_mat = jnp.ones((LANES, LANES), bf16)

    def count_ge(cand_rep):
        parts = []
        for rg in range(tq // SEL_COUNT_ROWS):
            rows = slice(rg * SEL_COUNT_ROWS, (rg + 1) * SEL_COUNT_ROWS)
            cand = cand_rep[rows]

            def body(kt, acc, rows=rows, cand=cand):
                c0 = pl.multiple_of(kt * SEL_COUNT_COLS, SEL_COUNT_COLS)
                ks = keys_scr[rows, pl.ds(c0, SEL_COUNT_COLS)]
                for u in range(SEL_COUNT_COLS // LANES):
                    acc = acc + jnp.where(ks[:, u * LANES:(u + 1) * LANES] >= cand, 1.0, 0.0)
                return acc

            parts.append(lax.fori_loop(0, n_ktc, body, jnp.zeros((SEL_COUNT_ROWS, LANES), f32)))
        acc = jnp.concatenate(parts, axis=0)
        return _dot(acc.astype(bf16), ones_mat)

    def thr_step(it, u):
        cand_u = u | lax.shift_left(jnp.int32(1), 31 - it)
        cnt = count_ge(cand_u ^ INT_MIN)
        return jnp.where(cnt >= top_k, cand_u, u)

    thr_rep = lax.fori_loop(0, 32, thr_step, jnp.zeros((tq, LANES), i32)) ^ INT_MIN
    n_ge = count_ge(thr_rep)[:, 0:1].astype(i32)
    thr = thr_rep[:, 0:1]
    thr_vis = jnp.maximum(thr, NEG_KEY + 1)
    excess = jnp.logical_and(n_ge > top_k, thr > NEG_KEY)
    any_excess = jnp.max(excess.astype(f32)) > 0.0
    out_ref[...] = jnp.full(out_ref.shape, NEG_INF, bf16)

    @pl.when(jnp.logical_not(any_excess))
    def _():
        def write_tile(kt, carry):
            c0, _ = cols_of(kt)
            ks = keys_scr[:, pl.ds(c0, tk)]
            out_ref[:, pl.ds(c0, tk)] = jnp.where(ks >= thr_vis, 0.0, NEG_INF).astype(bf16)
            return carry

        lax.fori_loop(0, n_kt, write_tile, 0)

    @pl.when(any_excess)
    def _():
        need = top_k - count(lambda ks, col: ks > thr)

        def tie_step(it, xv):
            cand = xv | lax.shift_left(jnp.int32(1), idx_bits - 1 - it)
            cnt = count(lambda ks, col: jnp.logical_and(ks == thr, col < cand))
            return jnp.where(cnt < need, cand, xv)

        xlim = lax.fori_loop(0, idx_bits, tie_step, jnp.zeros((tq, 1), i32))

        def write_tile(kt, carry):
            c0, col = cols_of(kt)
            ks = keys_scr[:, pl.ds(c0, tk)]
            sel = jnp.logical_or(ks > thr, jnp.logical_and(ks == thr, col <= xlim))
            sel = jnp.logical_and(sel, ks >= thr_vis)
            out_ref[:, pl.ds(c0, tk)] = jnp.where(sel, 0.0, NEG_INF).astype(bf16)
            return carry

        lax.fori_loop(0, n_kt, write_tile, 0)


def _dsa_select(qi, wi, kit, seq, top_k):
    tq, tk = min(256, seq), 512
    idx_bits = max(1, int(math.ceil(math.log2(seq))))
    body = functools.partial(_dsa_select_body, tq=tq, tk=tk, top_k=top_k, idx_bits=idx_bits)
    return pl.pallas_call(
        body,
        grid=(seq // tq,),
        in_specs=[pl.BlockSpec((tq, qi.shape[1]), lambda i: (i, 0)),
                  pl.BlockSpec((tq, wi.shape[1]), lambda i: (i, 0)),
                  pl.BlockSpec(kit.shape, lambda i: (0, 0))],
        out_specs=pl.BlockSpec((tq, seq), lambda i: (i, 0)),
        out_shape=jax.ShapeDtypeStruct((seq, seq), bf16),
        scratch_shapes=[pltpu.VMEM((tq, seq), i32)],
        compiler_params=_cparams(("parallel",)),
        name="dsa_select",
    )(qi, wi, kit)


FLAG_FIRST, FLAG_MASK, FLAG_LAST = 1, 2, 4


def _flash_body(it_ref, jt_ref, ft_ref, q_ref, k_ref, v_ref, *rest, tq, tk, n_heads, dq, dv, v_group, has_bias, diff,
                lambda_init):
    rest = list(rest)
    bias_ref = rest.pop(0) if has_bias else None
    if diff:
        lq1_ref, lk1_ref, lq2_ref, lk2_ref, gsub_ref = rest[:5]
        rest = rest[5:]
    o_ref, m_scr, acc_scr = rest
    dvp = _v_pad(dv)
    step = pl.program_id(0)
    i = it_ref[step]
    j = jt_ref[step]
    flag = ft_ref[step]

    @pl.when((flag & FLAG_FIRST) != 0)
    def _():
        m_scr[...] = jnp.full(m_scr.shape, NEG_INF, f32)
        acc_scr[...] = jnp.zeros(acc_scr.shape, f32)

    def attend(bias):
        for h in range(n_heads):
            hv = h // v_group
            s = _dot_nt(q_ref[:, h * dq:(h + 1) * dq], k_ref[:, h * dq:(h + 1) * dq])
            if bias is not None:
                s = s + bias
            m_prev = m_scr[h]
            m_new = jnp.maximum(m_prev, jnp.max(s, axis=1, keepdims=True))
            alpha = jnp.exp2(m_prev - m_new)
            p = jnp.exp2(s - jnp.tile(m_new, (1, tk // LANES)))
            acc_scr[h] = (acc_scr[h] * jnp.tile(alpha, (1, dvp // LANES))
                          + _dot(p.astype(bf16), v_ref[:, hv * dvp:(hv + 1) * dvp]))
            m_scr[h] = m_new

    if has_bias:
        attend(bias_ref[...].astype(f32))
    else:
        @pl.when((flag & FLAG_MASK) == 0)
        def _():
            attend(None)

        @pl.when((flag & FLAG_MASK) != 0)
        def _():
            r = (i * tq + lax.broadcasted_iota(i32, (tq, tk), 0)) // CHUNK
            c = (j * tk + lax.broadcasted_iota(i32, (tq, tk), 1)) // CHUNK
            attend(jnp.where(c <= r, 0.0, NEG_INF))

    @pl.when((flag & FLAG_LAST) != 0)
    def _():
        if diff:
            lam = (jnp.exp(jnp.sum(lq1_ref[...] * lk1_ref[...], axis=1, keepdims=True))
                   - jnp.exp(jnp.sum(lq2_ref[...] * lk2_ref[...], axis=1, keepdims=True)) + lambda_init)
            for hc in range(n_heads // 2):
                a1, a2 = acc_scr[2 * hc], acc_scr[2 * hc + 1]
                o1 = a1[:, :dv] / a1[:, dv:dv + 1]
                o2 = a2[:, :dv] / a2[:, dv:dv + 1]
                o = _rms(o1 - lam * o2, gsub_ref[...]) * (1.0 - lambda_init)
                o_ref[:, hc * dv:(hc + 1) * dv] = o.astype(o_ref.dtype)
        else:
            for h in range(n_heads):
                a = acc_scr[h]
                o_ref[:, h * dv:(h + 1) * dv] = (a[:, :dv] / a[:, dv:dv + 1]).astype(o_ref.dtype)


FLASH_TILE_ELEMS = 1024 * 512
FLASH_STATS_BYTES = 12 * 1024 * 1024


def _v_pad(dv):
    return (dv // LANES + 1) * LANES


def _with_ones_column(w_v, n_heads, dv):
    k = w_v.shape[0]
    dvp = _v_pad(dv)
    w = jnp.pad(w_v.reshape(k, n_heads, dv), ((0, 0), (0, 0), (0, dvp - dv))).reshape(k, n_heads * dvp)
    one = jnp.zeros((1, n_heads, dvp), f32).at[:, :, dv].set(1.0).reshape(1, n_heads * dvp)
    return w, one


def _flash_tiles(seq, n_heads, dv):
    tq = 1024
    while tq > 128 and n_heads * tq * (LANES + _v_pad(dv)) * 4 > FLASH_STATS_BYTES:
        tq //= 2
    return min(tq, seq), min(FLASH_TILE_ELEMS // tq, seq)


def _flash(q, k, v, *, n_heads, dq, dv, v_group=1, bias=None, diff_params=None, lambda_init=0.0, name):
    seq = q.shape[0]
    tq, tk = _flash_tiles(seq, n_heads, dv)
    pairs = []
    for i in range(seq // tq):
        j_last = ((i + 1) * tq - 1) // tk
        for j in range(j_last + 1):
            needs_mask = (j + 1) * tk > i * tq + CHUNK
            pairs.append((i, j, (FLAG_FIRST if j == 0 else 0) | (FLAG_MASK if needs_mask else 0)
                          | (FLAG_LAST if j == j_last else 0)))
    it = jnp.asarray([p[0] for p in pairs], i32)
    jt = jnp.asarray([p[1] for p in pairs], i32)
    ft = jnp.asarray([p[2] for p in pairs], i32)
    n_out = (n_heads // v_group) * dv
    in_specs = [pl.BlockSpec((tq, q.shape[1]), lambda s, it, jt, ft: (it[s], 0)),
                pl.BlockSpec((tk, k.shape[1]), lambda s, it, jt, ft: (jt[s], 0)),
                pl.BlockSpec((tk, v.shape[1]), lambda s, it, jt, ft: (jt[s], 0))]
    args = [q, k, v]
    if bias is not None:
        in_specs.append(pl.BlockSpec((tq, tk), lambda s, it, jt, ft: (it[s], jt[s])))
        args.append(bias)
    if diff_params is not None:
        for a in diff_params:
            in_specs.append(pl.BlockSpec(a.shape, lambda s, it, jt, ft: (0, 0)))
            args.append(a)
    body = functools.partial(_flash_body, tq=tq, tk=tk, n_heads=n_heads, dq=dq, dv=dv, v_group=v_group,
                             has_bias=bias is not None, diff=diff_params is not None, lambda_init=lambda_init)
    return pl.pallas_call(
        body,
        grid_spec=pltpu.PrefetchScalarGridSpec(
            num_scalar_prefetch=3,
            grid=(len(pairs),),
            in_specs=in_specs,
            out_specs=pl.BlockSpec((tq, n_out), lambda s, it, jt, ft: (it[s], 0)),
            scratch_shapes=[pltpu.VMEM((n_heads, tq, LANES), f32), pltpu.VMEM((n_heads, tq, _v_pad(dv)), f32)]),
        out_shape=jax.ShapeDtypeStruct((seq, n_out), bf16),
        compiler_params=_cparams(("arbitrary",)),
        name=name,
    )(it, jt, ft, *args)


def _silu_mul(gate, up):
    return gate / (1.0 + jnp.exp(-gate)) * up


def _ffn_body(x_ref, g_ref, wg_ref, wu_ref, wd_ref, o_ref, xn_scr, acc_scr):
    f = pl.program_id(1)

    @pl.when(f == 0)
    def _():
        xn_scr[...] = _rms(x_ref[...], g_ref[...]).astype(bf16)
        acc_scr[...] = jnp.zeros(acc_scr.shape, f32)

    xn = xn_scr[...]
    act = _silu_mul(_dot(xn, wg_ref[...]), _dot(xn, wu_ref[...])).astype(bf16)
    acc_scr[...] += _dot(act, wd_ref[...])

    @pl.when(f == pl.num_programs(1) - 1)
    def _():
        o_ref[...] = x_ref[...] + acc_scr[...]


def _ffn(x, g, wg, wu, wd, *, tm, tf, name):
    seq = x.shape[0]
    dff = wg.shape[1]
    return pl.pallas_call(
        _ffn_body,
        grid=(seq // tm, dff // tf),
        in_specs=[pl.BlockSpec((tm, D_MODEL), lambda i, f: (i, 0)),
                  pl.BlockSpec((1, D_MODEL), lambda i, f: (0, 0)),
                  pl.BlockSpec((D_MODEL, tf), lambda i, f: (0, f)),
                  pl.BlockSpec((D_MODEL, tf), lambda i, f: (0, f)),
                  pl.BlockSpec((tf, D_MODEL), lambda i, f: (f, 0))],
        out_specs=pl.BlockSpec((tm, D_MODEL), lambda i, f: (i, 0)),
        out_shape=jax.ShapeDtypeStruct((seq, D_MODEL), f32),
        scratch_shapes=[pltpu.VMEM((tm, D_MODEL), bf16), pltpu.VMEM((tm, D_MODEL), f32)],
        compiler_params=_cparams(("parallel", "arbitrary")),
        name=name,
    )(x, g, wg, wu, wd)


MOE_TILE = 512
PAIR_FIRST, PAIR_VALID, PAIR_LAST = 1, 2, 4


def _route(x, g, w_router):
    seq = x.shape[0]
    tm = min(MOE_TILE, seq)
    nb = seq // tm
    w_r = _pad_cols(w_router, LANES)
    w_hi = w_r.astype(bf16)
    w_lo = (w_r - w_hi.astype(f32)).astype(bf16)
    row = lambda n: pl.BlockSpec((tm, n), lambda i: (i, 0))
    const = lambda a: pl.BlockSpec(a.shape, lambda i: (0, 0))
    return pl.pallas_call(
        _router_body,
        grid=(nb,),
        in_specs=[row(D_MODEL), const(g), const(w_hi), const(w_lo)],
        out_specs=[row(D_MODEL), row(LANES), row(LANES), pl.BlockSpec((1, 8, LANES), lambda i: (i, 0, 0))],
        out_shape=[jax.ShapeDtypeStruct((seq, D_MODEL), bf16), jax.ShapeDtypeStruct((seq, LANES), f32),
                   jax.ShapeDtypeStruct((seq, LANES), f32), jax.ShapeDtypeStruct((nb, 8, LANES), f32)],
        scratch_shapes=[pltpu.VMEM((8, LANES), f32)],
        compiler_params=_cparams(("arbitrary",)),
        name="router",
    )(x, g, w_hi, w_lo)


def _moe_schedule(counts_after, seq):
    tm = min(MOE_TILE, seq)
    nb = seq // tm
    n_tiles = 2 * nb + N_EXP
    kmax = nb + 1
    max_pairs = n_tiles + N_EXP * nb
    bounds = jnp.concatenate([jnp.zeros((1, N_EXP), i32), counts_after[:, 0, :N_EXP].astype(i32)], 0)
    cnt = bounds[-1]
    ntile = (cnt + tm - 1) // tm
    tile_end = jnp.cumsum(ntile)
    tile_start = tile_end - ntile
    n_valid = tile_end[-1]
    p_ids = jnp.minimum(jnp.arange(n_tiles, dtype=i32), n_valid - 1)
    tile_expert = jnp.minimum(jnp.sum(p_ids[:, None] >= tile_end[None, :], axis=1), N_EXP - 1).astype(i32)
    tile_valid = (jnp.arange(n_tiles, dtype=i32) < n_valid).astype(i32)

    lo = jnp.transpose(bounds[:-1])[:, None, :]
    hi = jnp.transpose(bounds[1:])[:, None, :]
    k0 = (jnp.arange(kmax, dtype=i32) * tm)[None, :, None]
    meet = jnp.maximum(k0, lo) < jnp.minimum(k0 + tm, hi)
    n_pairs = jnp.sum(meet)
    s_ids = jnp.arange(max_pairs, dtype=i32)
    s_eff = jnp.minimum(s_ids, n_pairs - 1)

    def pair_list(flat, decode, group_of):
        idx = jnp.nonzero(flat, size=max_pairs, fill_value=0)[0].astype(i32)[s_eff]
        e, k, b = decode(idx)
        p = tile_start[e] + k
        grp = group_of(p, b)
        valid = s_ids < n_pairs
        first = jnp.concatenate([jnp.ones((1,), bool), grp[1:] != grp[:-1]])
        last = jnp.concatenate([grp[1:] != grp[:-1], jnp.ones((1,), bool)]) | (s_ids == n_pairs - 1)
        flags = jnp.where(valid, PAIR_VALID + PAIR_FIRST * first + PAIR_LAST * last, 0).astype(i32)
        return p.astype(i32), b.astype(i32), e.astype(i32), k.astype(i32), flags

    tile_major = pair_list(meet.reshape(-1),
                           lambda i: (i // (kmax * nb), (i // nb) % kmax, i % nb), lambda p, b: p)
    block_major = pair_list(jnp.transpose(meet, (2, 0, 1)).reshape(-1),
                            lambda i: ((i // kmax) % N_EXP, i % kmax, i // (N_EXP * kmax)), lambda p, b: b)
    return n_tiles, tile_expert, tile_valid, tile_major, block_major


def _moe_gather_body(pt, pb, pe, pk, pf, xn_ref, rank_t_ref, o_ref):
    s = pl.program_id(0)
    flag = pf[s]
    tmg, tb = o_ref.shape[0], xn_ref.shape[0]

    @pl.when((flag & PAIR_FIRST) != 0)
    def _():
        o_ref[...] = jnp.zeros(o_ref.shape, o_ref.dtype)

    @pl.when((flag & PAIR_VALID) != 0)
    def _():
        r = rank_t_ref[pl.ds(pe[s], 1), :] - (pk[s] * tmg).astype(f32)
        rows = lax.broadcasted_iota(i32, (tmg, tb), 0).astype(f32)
        onehot = jnp.where(rows == r, 1.0, 0.0).astype(bf16)
        o_ref[...] = o_ref[...] + _dot(onehot, xn_ref[...]).astype(o_ref.dtype)


def _moe_ffn_body(te, tv, x_ref, wg_ref, wu_ref, wd_ref, y_ref, acc_scr):
    p = pl.program_id(0)
    f = pl.program_id(1)
    last_f = f == pl.num_programs(1) - 1

    @pl.when(tv[p] != 0)
    def _():
        @pl.when(f == 0)
        def _():
            acc_scr[...] = jnp.zeros(acc_scr.shape, f32)

        x = x_ref[...]
        act = _silu_mul(_dot(x, wg_ref[0]), _dot(x, wu_ref[0])).astype(bf16)
        acc_scr[...] += _dot(act, wd_ref[0])

        @pl.when(last_f)
        def _():
            y_ref[...] = acc_scr[...].astype(y_ref.dtype)

    @pl.when(jnp.logical_and(tv[p] == 0, last_f))
    def _():
        y_ref[...] = jnp.zeros(y_ref.shape, y_ref.dtype)


def _moe_combine_body(ct, cb, ce, ck, cf, h_ref, y_ref, rank_ref, comb_ref, fg_ref, o_ref, acc_scr):
    s = pl.program_id(0)
    flag = cf[s]
    tb, tmg = h_ref.shape[0], y_ref.shape[0]

    @pl.when((flag & PAIR_FIRST) != 0)
    def _():
        acc_scr[...] = h_ref[...]

    @pl.when((flag & PAIR_VALID) != 0)
    def _():
        lane = lax.broadcasted_iota(i32, rank_ref.shape, 1)
        mine = lane == ce[s]
        r = jnp.sum(jnp.where(mine, rank_ref[...], 0.0), axis=1, keepdims=True) - (ck[s] * tmg).astype(f32)
        gate = jnp.sum(jnp.where(mine, comb_ref[...], 0.0), axis=1, keepdims=True)
        cols = lax.broadcasted_iota(i32, (tb, tmg), 1).astype(f32)
        onehot = jnp.where(cols == r, 1.0, 0.0).astype(bf16)
        acc_scr[...] += gate * _dot(onehot, y_ref[...])

    @pl.when((flag & PAIR_LAST) != 0)
    def _():
        o_ref[...] = _rms(acc_scr[...], fg_ref[...])


def _moe(h, g_ffn, w_router, w_gate_e, w_up_e, w_down_e, final_g):
    seq = h.shape[0]
    tm = min(MOE_TILE, seq)
    xn, comb, rank, counts_after = _route(h, g_ffn, w_router)
    n_tiles, tile_expert, tile_valid, tile_major, block_major = _moe_schedule(counts_after, seq)
    n_pairs = tile_major[0].shape[0]
    rank_t = jnp.transpose(rank[:, :8])

    x_sorted = pl.pallas_call(
        _moe_gather_body,
        grid_spec=pltpu.PrefetchScalarGridSpec(
            num_scalar_prefetch=5, grid=(n_pairs,),
            in_specs=[pl.BlockSpec((tm, D_MODEL), lambda s, pt, pb, pe, pk, pf: (pb[s], 0)),
                      pl.BlockSpec((8, tm), lambda s, pt, pb, pe, pk, pf: (0, pb[s]))],
            out_specs=pl.BlockSpec((tm, D_MODEL), lambda s, pt, pb, pe, pk, pf: (pt[s], 0))),
        out_shape=jax.ShapeDtypeStruct((n_tiles * tm, D_MODEL), bf16),
        compiler_params=_cparams(("arbitrary",)),
        name="moe_gather",
    )(*tile_major, xn, rank_t)

    tf = D_FF_E // 4
    n_f = D_FF_E // tf
    f_eff = lambda f, v: f * v + (n_f - 1) * (1 - v)
    y_sorted = pl.pallas_call(
        _moe_ffn_body,
        grid_spec=pltpu.PrefetchScalarGridSpec(
            num_scalar_prefetch=2, grid=(n_tiles, n_f),
            in_specs=[pl.BlockSpec((tm, D_MODEL), lambda p, f, te, tv: (p, 0)),
                      pl.BlockSpec((1, D_MODEL, tf), lambda p, f, te, tv: (te[p], 0, f_eff(f, tv[p]))),
                      pl.BlockSpec((1, D_MODEL, tf), lambda p, f, te, tv: (te[p], 0, f_eff(f, tv[p]))),
                      pl.BlockSpec((1, tf, D_MODEL), lambda p, f, te, tv: (te[p], f_eff(f, tv[p]), 0))],
            out_specs=pl.BlockSpec((tm, D_MODEL), lambda p, f, te, tv: (p, 0)),
            scratch_shapes=[pltpu.VMEM((tm, D_MODEL), f32)]),
        out_shape=jax.ShapeDtypeStruct((n_tiles * tm, D_MODEL), bf16),
        compiler_params=_cparams(("arbitrary", "arbitrary")),
        name="moe_ffn",
    )(tile_expert, tile_valid, x_sorted, w_gate_e.astype(bf16), w_up_e.astype(bf16), w_down_e.astype(bf16))

    return pl.pallas_call(
        _moe_combine_body,
        grid_spec=pltpu.PrefetchScalarGridSpec(
            num_scalar_prefetch=5, grid=(n_pairs,),
            in_specs=[pl.BlockSpec((tm, D_MODEL), lambda s, ct, cb, ce, ck, cf: (cb[s], 0)),
                      pl.BlockSpec((tm, D_MODEL), lambda s, ct, cb, ce, ck, cf: (ct[s], 0)),
                      pl.BlockSpec((tm, LANES), lambda s, ct, cb, ce, ck, cf: (cb[s], 0)),
                      pl.BlockSpec((tm, LANES), lambda s, ct, cb, ce, ck, cf: (cb[s], 0)),
                      pl.BlockSpec((1, D_MODEL), lambda s, ct, cb, ce, ck, cf: (0, 0))],
            out_specs=pl.BlockSpec((tm, D_MODEL), lambda s, ct, cb, ce, ck, cf: (cb[s], 0)),
            scratch_shapes=[pltpu.VMEM((tm, D_MODEL), f32)]),
        out_shape=jax.ShapeDtypeStruct((seq, D_MODEL), f32),
        compiler_params=_cparams(("arbitrary",)),
        name="moe_combine",
    )(*block_major, h, y_sorted, rank, comb, final_g)


def _even_layer(h, norm_mix, w_in, g_q_lat, w_uq, g_kv_lat, w_ukv, g_idx_k, w_out, norm_ffn, w_gate, w_up, w_down):
    seq = h.shape[0]
    sizes = (Q_LORA, KV_LORA, ROPE_A, H_B * DH_B, H_B * DH_B, H_B * DH_B, H_IDX * D_IDX, D_IDX, H_IDX)
    offs = np.cumsum((0,) + sizes)
    w_cq, w_ckv, w_kr, w_qb, w_kb, w_vb, w_qi, w_ki, w_wi = [w_in[:, offs[n]:offs[n + 1]] for n in range(9)]

    scale_a = (NOPE_A + ROPE_A) ** -0.5 * LOG2E
    wq3 = (w_uq * scale_a).reshape(Q_LORA, H_A, NOPE_A + ROPE_A)
    wq = jnp.pad(wq3, ((0, 0), (0, 0), (0, HEAD_PAD_A - NOPE_A - ROPE_A))).reshape(Q_LORA, H_A * HEAD_PAD_A)
    wq_sw = _swap_cols(wq, HEAD_PAD_A, NOPE_A, ROPE_A)
    wkv3 = w_ukv.reshape(KV_LORA, H_A, NOPE_A + V_A)
    wk = jnp.pad(wkv3[:, :, :NOPE_A], ((0, 0), (0, 0), (0, HEAD_PAD_A - NOPE_A))).reshape(KV_LORA, H_A * HEAD_PAD_A)
    wv, one_a = _with_ones_column(wkv3[:, :, NOPE_A:].reshape(KV_LORA, H_A * V_A), H_A, V_A)
    w_vb, one_b = _with_ones_column(w_vb, H_B, DH_B)
    place = jnp.zeros((ROPE_A, H_A, HEAD_PAD_A), f32)
    place = place.at[:, :, NOPE_A:NOPE_A + ROPE_A].set(jnp.eye(ROPE_A, dtype=f32)[:, None, :])
    place = place.reshape(ROPE_A, H_A * HEAD_PAD_A)

    w_qb = w_qb * (DH_B ** -0.5 * LOG2E)
    w_qi = w_qi * D_IDX ** -0.5
    w_small = jnp.concatenate([w_kr, _swap_cols(w_kr, ROPE_A, 0, ROPE_A), w_ki, _swap_cols(w_ki, D_IDX, 0, ROT_IDX)], 1)
    w_wi_p = _pad_cols(w_wi * H_IDX ** -0.5, LANES)
    g_idx = g_idx_k.reshape(1, D_IDX)
    g_idx_sw = jnp.concatenate([g_idx[:, ROT_IDX // 2:ROT_IDX], g_idx[:, :ROT_IDX // 2], g_idx[:, ROT_IDX:]], 1)

    ca, sa = _rope_tables(seq, ROPE_A, HEAD_PAD_A, NOPE_A, LANES)
    cb, sb = _rope_tables(seq, ROT_B, DH_B, 0, LANES)
    ci, si = _rope_tables(seq, ROT_IDX, D_IDX, 0, LANES)
    ckr, skr = _rope_tables(seq, ROPE_A, ROPE_A, 0, ROPE_A)

    consts = [norm_mix.reshape(1, -1), g_q_lat.reshape(1, -1), g_kv_lat.reshape(1, -1), g_idx, g_idx_sw,
              one_a, one_b]
    weights = [jnp.concatenate([w_cq, w_ckv], 1), wq, wq_sw, wk, wv, place,
               w_qb, _swap_cols(w_qb, DH_B, 0, ROT_B), w_kb, _swap_cols(w_kb, DH_B, 0, ROT_B), w_vb,
               w_qi, _swap_cols(w_qi, D_IDX, 0, ROT_IDX), w_small, w_wi_p]
    weights = [w.astype(bf16) for w in weights]
    sds = lambda n, dt: jax.ShapeDtypeStruct((seq, n), dt)
    outs = [sds(H_A * HEAD_PAD_A, bf16), sds(H_A * HEAD_PAD_A, bf16), sds(H_A * _v_pad(V_A), bf16),
            sds(H_B * DH_B, bf16), sds(H_B * DH_B, bf16), sds(H_B * _v_pad(DH_B), bf16),
            sds(H_IDX * D_IDX, bf16), sds(D_IDX, bf16), sds(LANES, f32)]
    qa, ka, va, qb, kb, vb, qi, ki, wi = _rows_call(
        _even_proj_body, seq, 256, [h, ca, sa, cb, sb, ci, si, ckr, skr], consts + weights, outs, "even_proj")

    o_a = _flash(qa, ka, va, n_heads=H_A, dq=HEAD_PAD_A, dv=V_A, name="mla_attn")
    top_k = min(TOPK_MAX, seq // 4)
    bias = _dsa_select(qi, wi, ki.T, seq, top_k)
    o_b = _flash(qb, kb, vb, n_heads=H_B, dq=DH_B, dv=DH_B, bias=bias, name="dsa_attn")

    w_out = w_out.astype(bf16)
    n_a = H_A * V_A
    (h,) = _rows_call(_out_proj2_body, seq, 512, [h, o_a, o_b], [w_out[:n_a], w_out[n_a:]],
                      [jax.ShapeDtypeStruct((seq, D_MODEL), f32)], "even_out_proj")
    return _ffn(h, norm_ffn.reshape(1, -1), w_gate.astype(bf16), w_up.astype(bf16), w_down.astype(bf16),
                tm=512, tf=D_FF // 2, name="dense_ffn")


def _odd_layer(h, layer, norm_mix, w_qkv, lq1, lk1, lq2, lk2, g_sub, w_out, norm_ffn, w_router, w_gate_e, w_up_e,
               w_down_e, final_norm):
    seq = h.shape[0]
    lambda_init = 0.8 - 0.6 * math.exp(-0.3 * layer)
    n = H_C * 2 * DH_C
    w_q = w_qkv[:, :n] * (DH_C ** -0.5 * LOG2E)
    w_k = w_qkv[:, n:2 * n]
    w_v, one_v = _with_ones_column(w_qkv[:, 2 * n:], H_C, 2 * DH_C)
    cb, sb = _rope_tables(seq, ROT_C, DH_C, 0, LANES)
    weights = [w_q, _swap_cols(w_q, DH_C, 0, ROT_C), w_k, _swap_cols(w_k, DH_C, 0, ROT_C), w_v]
    weights = [w.astype(bf16) for w in weights]
    sds = jax.ShapeDtypeStruct((seq, n), bf16)
    sds_v = jax.ShapeDtypeStruct((seq, w_v.shape[1]), bf16)
    q, k, v = _rows_call(_odd_proj_body, seq, 512, [h, cb, sb], [norm_mix.reshape(1, -1), one_v] + weights,
                         [sds, sds, sds_v], "odd_proj")
    diff_params = [lq1.reshape(1, -1), lk1.reshape(1, -1), lq2.reshape(1, -1), lk2.reshape(1, -1),
                   g_sub.reshape(1, -1)]
    o = _flash(q, k, v, n_heads=2 * H_C, dq=DH_C, dv=2 * DH_C, v_group=2, diff_params=diff_params,
               lambda_init=lambda_init, name="diff_attn")
    (h,) = _rows_call(_out_proj1_body, seq, 512, [h, o], [w_out.astype(bf16)],
                      [jax.ShapeDtypeStruct((seq, D_MODEL), f32)], "odd_out_proj")
    return _moe(h, norm_ffn.reshape(1, -1), w_router, w_gate_e, w_up_e, w_down_e, final_norm.reshape(1, -1))


def kernel(x, ev_norm_mix, ev_w_in, ev_g_q_lat, ev_w_uq, ev_g_kv_lat, ev_w_ukv, ev_g_idx_k, ev_w_out, ev_norm_ffn, ev_w_gate, ev_w_up, ev_w_down, od_norm_mix, od_w_qkv, od_lambda_q1, od_lambda_k1, od_lambda_q2, od_lambda_k2, od_g_sub, od_w_out, od_norm_ffn, od_w_router, od_w_gate_e, od_w_up_e, od_w_down_e, final_norm):
    batch, seq, _ = x.shape
    assert batch == 1 and ev_w_in.shape[0] == 1 and od_w_qkv.shape[0] == 1
    h = x[0]
    h = _even_layer(h, ev_norm_mix[0], ev_w_in[0], ev_g_q_lat[0], ev_w_uq[0], ev_g_kv_lat[0], ev_w_ukv[0],
                    ev_g_idx_k[0], ev_w_out[0], ev_norm_ffn[0], ev_w_gate[0], ev_w_up[0], ev_w_down[0])
    h = _odd_layer(h, 1, od_norm_mix[0], od_w_qkv[0], od_lambda_q1[0], od_lambda_k1[0], od_lambda_q2[0],
                   od_lambda_k2[0], od_g_sub[0], od_w_out[0], od_norm_ffn[0], od_w_router[0], od_w_gate_e[0],
                   od_w_up_e[0], od_w_down_e[0], final_norm)
    return h[None]
```

```python
import functools
import math

import numpy as np
import jax
import jax.numpy as jnp
from jax import lax
from jax.experimental import pallas as pl
from jax.experimental.pallas import tpu as pltpu

f32 = jnp.float32
bf16 = jnp.bfloat16
i32 = jnp.int32
i16 = jnp.int16

D_MODEL = 1024
CHUNK = 64
ROPE_THETA = 500000.0
NORM_EPS = 1e-6
NEG_INF = -1e30
LOG2E = math.log2(math.e)

H_A, Q_LORA, KV_LORA, NOPE_A, ROPE_A, V_A = 8, 256, 128, 64, 32, 64
H_B, DH_B, ROT_B = 8, 64, 16
H_IDX, D_IDX, ROT_IDX = 8, 32, 8
TOPK_MAX = 256
H_C, DH_C, ROT_C = 8, 64, 16
D_FF, N_EXP, D_FF_E = 2816, 8, 3584

LANES = 128
HEAD_PAD_A = 128

_NEG_BITS = int(np.float32(NEG_INF).view(np.int32))
NEG_KEY = _NEG_BITS ^ 0x7FFFFFFF
INT_MIN = -(2 ** 31)

VMEM_LIMIT = 56 * 1024 * 1024


def _cparams(sem):
    return pltpu.CompilerParams(dimension_semantics=sem, vmem_limit_bytes=VMEM_LIMIT)


def _rms(x, g):
    var = jnp.mean(x * x, axis=-1, keepdims=True)
    return x * lax.rsqrt(var + NORM_EPS) * g


def _dot(a, b):
    return jnp.dot(a, b, preferred_element_type=f32)


def _dot_nt(a, b):
    return lax.dot_general(a, b, (((1,), (1,)), ((), ())), preferred_element_type=f32)


def _rope_tables(seq, rot_dim, head_width, offset, width):
    pos = jnp.arange(seq, dtype=f32)
    inv_freq = ROPE_THETA ** (-jnp.arange(0, rot_dim, 2, dtype=f32) / rot_dim)
    ang = pos[:, None] * inv_freq[None, :]
    cos, sin = jnp.cos(ang), jnp.sin(ang)
    c = jnp.ones((seq, head_width), f32).at[:, offset:offset + rot_dim].set(jnp.concatenate([cos, cos], -1))
    s = jnp.zeros((seq, head_width), f32).at[:, offset:offset + rot_dim].set(jnp.concatenate([-sin, sin], -1))
    reps = width // head_width
    return jnp.tile(c, (1, reps)), jnp.tile(s, (1, reps))


def _swap_cols(w, head_width, offset, rot_dim):
    k, n = w.shape
    half = rot_dim // 2
    w3 = w.reshape(k, n // head_width, head_width)
    out = jnp.zeros_like(w3)
    out = out.at[:, :, offset:offset + half].set(w3[:, :, offset + half:offset + rot_dim])
    out = out.at[:, :, offset + half:offset + rot_dim].set(w3[:, :, offset:offset + half])
    return out.reshape(k, n)


def _pad_cols(w, width):
    return jnp.pad(w, ((0, 0), (0, width - w.shape[1])))


def _rows_call(body, seq, tm, row_ins, const_ins, out_sds, name):
    def rspec(a):
        return pl.BlockSpec((tm, a.shape[1]), lambda i: (i, 0))

    def cspec(a):
        nd = a.ndim
        return pl.BlockSpec(a.shape, lambda i: (0,) * nd)

    return pl.pallas_call(
        body,
        grid=(seq // tm,),
        in_specs=[rspec(a) for a in row_ins] + [cspec(a) for a in const_ins],
        out_specs=[pl.BlockSpec((tm, o.shape[1]), lambda i: (i, 0)) for o in out_sds],
        out_shape=out_sds,
        compiler_params=_cparams(("parallel",)),
        name=name,
    )(*row_ins, *const_ins)


def _even_proj_body(x_ref, ca_ref, sa_ref, cb_ref, sb_ref, ci_ref, si_ref, ckr_ref, skr_ref,
                    g_ref, gq_ref, gkv_ref, gi_ref, gisw_ref, onea_ref, oneb_ref,
                    wlat_ref, wq_ref, wqsw_ref, wk_ref, wv_ref, place_ref,
                    wqb_ref, wqbsw_ref, wkb_ref, wkbsw_ref, wvb_ref, wqi_ref, wqisw_ref, wsm_ref, wwi_ref,
                    qa_ref, ka_ref, va_ref, qb_ref, kb_ref, vb_ref, qi_ref, ki_ref, wi_ref):
    xn = _rms(x_ref[...], g_ref[...]).astype(bf16)
    lat = _dot(xn, wlat_ref[...])
    cqn = _rms(lat[:, :Q_LORA], gq_ref[...]).astype(bf16)
    ckvn = _rms(lat[:, Q_LORA:], gkv_ref[...]).astype(bf16)
    reps_a = qa_ref.shape[1] // LANES
    ca = jnp.tile(ca_ref[...], (1, reps_a))
    sa = jnp.tile(sa_ref[...], (1, reps_a))
    qa_ref[...] = (_dot(cqn, wq_ref[...]) * ca + _dot(cqn, wqsw_ref[...]) * sa).astype(bf16)
    small = _dot(xn, wsm_ref[...])
    kr, kr_sw = small[:, 0:ROPE_A], small[:, ROPE_A:2 * ROPE_A]
    kpe = (kr * ckr_ref[...] + kr_sw * skr_ref[...]).astype(bf16)
    ka_ref[...] = (_dot(ckvn, wk_ref[...]) + _dot(kpe, place_ref[...])).astype(bf16)
    va_ref[...] = (_dot(ckvn, wv_ref[...]) + onea_ref[...]).astype(bf16)
    reps_b = qb_ref.shape[1] // LANES
    cb = jnp.tile(cb_ref[...], (1, reps_b))
    sb = jnp.tile(sb_ref[...], (1, reps_b))
    qb_ref[...] = (_dot(xn, wqb_ref[...]) * cb + _dot(xn, wqbsw_ref[...]) * sb).astype(bf16)
    kb_ref[...] = (_dot(xn, wkb_ref[...]) * cb + _dot(xn, wkbsw_ref[...]) * sb).astype(bf16)
    vb_ref[...] = (_dot(xn, wvb_ref[...]) + oneb_ref[...]).astype(bf16)
    reps_i = qi_ref.shape[1] // LANES
    ci = jnp.tile(ci_ref[...], (1, reps_i))
    si = jnp.tile(si_ref[...], (1, reps_i))
    qi_ref[...] = (_dot(xn, wqi_ref[...]) * ci + _dot(xn, wqisw_ref[...]) * si).astype(bf16)
    ki, ki_sw = small[:, 2 * ROPE_A:2 * ROPE_A + D_IDX], small[:, 2 * ROPE_A + D_IDX:2 * ROPE_A + 2 * D_IDX]
    r = lax.rsqrt(jnp.mean(ki * ki, axis=-1, keepdims=True) + NORM_EPS)
    ci32, si32 = ci_ref[:, 0:D_IDX], si_ref[:, 0:D_IDX]
    ki_ref[...] = (ki * r * gi_ref[...] * ci32 + ki_sw * r * gisw_ref[...] * si32).astype(bf16)
    wi_ref[...] = _dot(xn, wwi_ref[...])


def _odd_proj_body(x_ref, cb_ref, sb_ref, g_ref, one_ref, wq_ref, wqsw_ref, wk_ref, wksw_ref, wv_ref,
                   q_ref, k_ref, v_ref):
    xn = _rms(x_ref[...], g_ref[...]).astype(bf16)
    reps = q_ref.shape[1] // LANES
    cb = jnp.tile(cb_ref[...], (1, reps))
    sb = jnp.tile(sb_ref[...], (1, reps))
    q_ref[...] = (_dot(xn, wq_ref[...]) * cb + _dot(xn, wqsw_ref[...]) * sb).astype(bf16)
    k_ref[...] = (_dot(xn, wk_ref[...]) * cb + _dot(xn, wksw_ref[...]) * sb).astype(bf16)
    v_ref[...] = (_dot(xn, wv_ref[...]) + one_ref[...]).astype(bf16)


def _out_proj2_body(x_ref, a1_ref, a2_ref, w1_ref, w2_ref, o_ref):
    o_ref[...] = x_ref[...] + _dot(a1_ref[...], w1_ref[...]) + _dot(a2_ref[...], w2_ref[...])


def _out_proj1_body(x_ref, a_ref, w_ref, o_ref):
    o_ref[...] = x_ref[...] + _dot(a_ref[...], w_ref[...])


def _router_body(x_ref, g_ref, whi_ref, wlo_ref, xn_ref, comb_ref, rank_ref, cnt_ref, carry_scr):
    @pl.when(pl.program_id(0) == 0)
    def _():
        carry_scr[...] = jnp.zeros(carry_scr.shape, f32)

    xn = _rms(x_ref[...], g_ref[...])
    xn_ref[...] = xn.astype(bf16)
    hi = xn.astype(bf16)
    lo = (xn - hi.astype(f32)).astype(bf16)
    logits = _dot(hi, whi_ref[...]) + _dot(lo, whi_ref[...]) + _dot(hi, wlo_ref[...])
    lane = lax.broadcasted_iota(i32, logits.shape, 1).astype(f32)
    lg = jnp.where(lane < N_EXP, logits, -jnp.inf)
    m1 = jnp.max(lg, axis=1, keepdims=True)
    i1 = jnp.min(jnp.where(lg == m1, lane, float(LANES)), axis=1, keepdims=True)
    lg2 = jnp.where(lane == i1, -jnp.inf, lg)
    m2 = jnp.max(lg2, axis=1, keepdims=True)
    i2 = jnp.min(jnp.where(lg2 == m2, lane, float(LANES)), axis=1, keepdims=True)
    e2 = jnp.exp(m2 - m1)
    den = 1.0 + e2
    comb_ref[...] = jnp.where(lane == i1, 1.0 / den, 0.0) + jnp.where(lane == i2, e2 / den, 0.0)
    routed = jnp.logical_or(lane == i1, lane == i2)
    onehot = jnp.where(routed, 1.0, 0.0)
    tm = onehot.shape[0]
    earlier = lax.broadcasted_iota(i32, (tm, tm), 1) < lax.broadcasted_iota(i32, (tm, tm), 0)
    before = _dot(jnp.where(earlier, 1.0, 0.0).astype(bf16), onehot.astype(bf16))
    carry = carry_scr[0:1, :]
    rank_ref[...] = jnp.where(routed, before + carry, -1.0)
    carry = carry + jnp.sum(onehot, axis=0, keepdims=True)
    carry_scr[...] = jnp.broadcast_to(carry, carry_scr.shape)
    cnt_ref[0] = jnp.broadcast_to(carry, cnt_ref.shape[1:])


SEL_COUNT_ROWS, SEL_COUNT_COLS = 64, 1024


def _dsa_select_body(qi_ref, wi_ref, kit_ref, out_ref, keys_scr, hi16_scr, low16_scr, *, tq, tk, top_k, idx_bits):
    q0 = pl.program_id(0) * tq
    n_ktc = (q0 + tq + SEL_COUNT_COLS - 1) // SEL_COUNT_COLS
    n_kt = n_ktc * (SEL_COUNT_COLS // tk)
    qh = [qi_ref[:, h * D_IDX:(h + 1) * D_IDX] for h in range(H_IDX)]
    w = wi_ref[...]
    wb = [jnp.broadcast_to(w[:, h:h + 1], (tq, tk)) for h in range(H_IDX)]
    row = q0 + lax.broadcasted_iota(i32, (tq, 1), 0)
    row_lim = (row // CHUNK + 1) * CHUNK

    def cols_of(kt):
        c0 = pl.multiple_of(kt * tk, tk)
        return c0, c0 + lax.broadcasted_iota(i32, (tq, tk), 1)

    def score_tile(kt, carry):
        c0, col = cols_of(kt)
        kt_tile = kit_ref[:, pl.ds(c0, tk)]
        acc = jnp.zeros((tq, tk), f32)
        for h in range(H_IDX):
            acc = acc + jnp.maximum(_dot(qh[h], kt_tile), 0.0) * wb[h]
        sc = jnp.where(col < row_lim, acc, NEG_INF)
        bits = lax.bitcast_convert_type(sc, i32)
        key = jnp.where(bits < 0, bits ^ 0x7FFFFFFF, bits)
        keys_scr[:, pl.ds(c0, tk)] = key
        hi16_scr[:, pl.ds(c0, tk)] = lax.shift_right_arithmetic(key, 16).astype(i16)
        return carry

    lax.fori_loop(0, n_kt, score_tile, 0)

    def count(pred):
        def body(kt, acc):
            c0, col = cols_of(kt)
            m = pred(keys_scr[:, pl.ds(c0, tk)], col).astype(i32)
            for u in range(tk // LANES):
                acc = acc + m[:, u * LANES:(u + 1) * LANES]
            return acc

        acc = lax.fori_loop(0, n_kt, body, jnp.zeros((tq, LANES), i32))
        return jnp.sum(acc.astype(f32), axis=1, keepdims=True).astype(i32)

    ones_mat = jnp.ones((LANES, LANES), bf16)

    def count_ge(cand_rep):
        parts = []
        for rg in range(tq // SEL_COUNT_ROWS):
            rows = slice(rg * SEL_COUNT_ROWS, (rg + 1) * SEL_COUNT_ROWS)
            cand = cand_rep[rows]

            def body(kt, acc, rows=rows, cand=cand):
                c0 = pl.multiple_of(kt * SEL_COUNT_COLS, SEL_COUNT_COLS)
                ks = keys_scr[rows, pl.ds(c0, SEL_COUNT_COLS)]
                for u in range(SEL_COUNT_COLS // LANES):
                    acc = acc + jnp.where(ks[:, u * LANES:(u + 1) * LANES] >= cand, 1.0, 0.0)
                return acc

            parts.append(lax.fori_loop(0, n_ktc, body, jnp.zeros((SEL_COUNT_ROWS, LANES), f32)))
        acc = jnp.concatenate(parts, axis=0)
        return _dot(acc.astype(bf16), ones_mat)

    def count16(scr, cand_rep, strict):
        rows16 = 2 * SEL_COUNT_ROWS
        parts = []
        for rg in range(tq // rows16):
            rows = slice(rg * rows16, (rg + 1) * rows16)
            cand = cand_rep[rows].astype(i16)

            def body(kt, acc, rows=rows, cand=cand):
                c0 = pl.multiple_of(kt * SEL_COUNT_COLS, SEL_COUNT_COLS)
                ks = scr[rows, pl.ds(c0, SEL_COUNT_COLS)]
                for u in range(SEL_COUNT_COLS // LANES):
                    blk = ks[:, u * LANES:(u + 1) * LANES]
                    hit = blk > cand if strict else blk >= cand
                    acc = acc + jnp.where(hit, jnp.int16(1), jnp.int16(0))
                return acc

            parts.append(lax.fori_loop(0, n_ktc, body, jnp.zeros((rows16, LANES), i16)))
        acc = jnp.concatenate(parts, axis=0)
        return _dot(acc.astype(f32).astype(bf16), ones_mat)

    def thr_step_hi(it, u):
        cand_u = u | lax.shift_left(jnp.int32(1), 31 - it)
        cnt = count16(hi16_scr, lax.shift_right_arithmetic(cand_u ^ INT_MIN, 16), False)
        return jnp.where(cnt >= top_k, cand_u, u)

    u_hi = lax.fori_loop(0, 16, thr_step_hi, jnp.zeros((tq, LANES), i32))
    thr_hi = lax.shift_right_arithmetic(u_hi ^ INT_MIN, 16)
    n_above = count16(hi16_scr, thr_hi, True)

    def low_tile(kt, carry):
        c0 = pl.multiple_of(kt * tk, tk)
        low = ((keys_scr[:, pl.ds(c0, tk)] & 0xFFFF) - 32768).astype(i16)
        same = hi16_scr[:, pl.ds(c0, tk)] == jnp.tile(thr_hi, (1, tk // LANES)).astype(i16)
        low16_scr[:, pl.ds(c0, tk)] = jnp.where(same, low, jnp.int16(-32768))
        return carry

    lax.fori_loop(0, n_kt, low_tile, 0)

    def thr_step_lo(it, u):
        cand_u = u | lax.shift_left(jnp.int32(1), 31 - it)
        cnt = n_above + count16(low16_scr, (cand_u & 0xFFFF) - 32768, False)
        return jnp.where(cnt >= top_k, cand_u, u)

    thr_rep = lax.fori_loop(16, 32, thr_step_lo, u_hi) ^ INT_MIN
    n_ge = count_ge(thr_rep)[:, 0:1].astype(i32)
    thr = thr_rep[:, 0:1]
    thr_vis = jnp.maximum(thr, NEG_KEY + 1)
    excess = jnp.logical_and(n_ge > top_k, thr > NEG_KEY)
    any_excess = jnp.max(excess.astype(f32)) > 0.0
    out_ref[...] = jnp.full(out_ref.shape, NEG_INF, bf16)

    @pl.when(jnp.logical_not(any_excess))
    def _():
        def write_tile(kt, carry):
            c0, _ = cols_of(kt)
            ks = keys_scr[:, pl.ds(c0, tk)]
            out_ref[:, pl.ds(c0, tk)] = jnp.where(ks >= thr_vis, 0.0, NEG_INF).astype(bf16)
            return carry

        lax.fori_loop(0, n_kt, write_tile, 0)

    @pl.when(any_excess)
    def _():
        need = top_k - count(lambda ks, col: ks > thr)

        def tie_step(it, xv):
            cand = xv | lax.shift_left(jnp.int32(1), idx_bits - 1 - it)
            cnt = count(lambda ks, col: jnp.logical_and(ks == thr, col < cand))
            return jnp.where(cnt < need, cand, xv)

        xlim = lax.fori_loop(0, idx_bits, tie_step, jnp.zeros((tq, 1), i32))

        def write_tile(kt, carry):
            c0, col = cols_of(kt)
            ks = keys_scr[:, pl.ds(c0, tk)]
            sel = jnp.logical_or(ks > thr, jnp.logical_and(ks == thr, col <= xlim))
            sel = jnp.logical_and(sel, ks >= thr_vis)
            out_ref[:, pl.ds(c0, tk)] = jnp.where(sel, 0.0, NEG_INF).astype(bf16)
            return carry

        lax.fori_loop(0, n_kt, write_tile, 0)


def _dsa_select(qi, wi, kit, seq, top_k):
    tq, tk = min(256, seq), 512
    idx_bits = max(1, int(math.ceil(math.log2(seq))))
    body = functools.partial(_dsa_select_body, tq=tq, tk=tk, top_k=top_k, idx_bits=idx_bits)
    return pl.pallas_call(
        body,
        grid=(seq // tq,),
        in_specs=[pl.BlockSpec((tq, qi.shape[1]), lambda i: (i, 0)),
                  pl.BlockSpec((tq, wi.shape[1]), lambda i: (i, 0)),
                  pl.BlockSpec(kit.shape, lambda i: (0, 0))],
        out_specs=pl.BlockSpec((tq, seq), lambda i: (i, 0)),
        out_shape=jax.ShapeDtypeStruct((seq, seq), bf16),
        scratch_shapes=[pltpu.VMEM((tq, seq), i32), pltpu.VMEM((tq, seq), i16), pltpu.VMEM((tq, seq), i16)],
        compiler_params=_cparams(("parallel",)),
        name="dsa_select",
    )(qi, wi, kit)


FLAG_FIRST, FLAG_MASK, FLAG_LAST = 1, 2, 4


def _flash_body(it_ref, jt_ref, ft_ref, q_ref, k_ref, v_ref, *rest, tq, tk, n_heads, dq, dv, v_group, has_bias, diff,
                lambda_init):
    rest = list(rest)
    bias_ref = rest.pop(0) if has_bias else None
    if diff:
        lq1_ref, lk1_ref, lq2_ref, lk2_ref, gsub_ref = rest[:5]
        rest = rest[5:]
    o_ref, m_scr, acc_scr = rest
    dvp = _v_pad(dv)
    step = pl.program_id(0)
    i = it_ref[step]
    j = jt_ref[step]
    flag = ft_ref[step]

    @pl.when((flag & FLAG_FIRST) != 0)
    def _():
        m_scr[...] = jnp.full(m_scr.shape, NEG_INF, f32)
        acc_scr[...] = jnp.zeros(acc_scr.shape, f32)

    def attend(bias):
        for h in range(n_heads):
            hv = h // v_group
            s = _dot_nt(q_ref[:, h * dq:(h + 1) * dq], k_ref[:, h * dq:(h + 1) * dq])
            if bias is not None:
                s = s + bias
            m_prev = m_scr[h]
            m_new = jnp.maximum(m_prev, jnp.max(s, axis=1, keepdims=True))
            alpha = jnp.exp2(m_prev - m_new)
            p = jnp.exp2(s - jnp.tile(m_new, (1, tk // LANES)))
            acc_scr[h] = (acc_scr[h] * jnp.tile(alpha, (1, dvp // LANES))
                          + _dot(p.astype(bf16), v_ref[:, hv * dvp:(hv + 1) * dvp]))
            m_scr[h] = m_new

    if has_bias:
        attend(bias_ref[...].astype(f32))
    else:
        @pl.when((flag & FLAG_MASK) == 0)
        def _():
            attend(None)

        @pl.when((flag & FLAG_MASK) != 0)
        def _():
            r = (i * tq + lax.broadcasted_iota(i32, (tq, tk), 0)) // CHUNK
            c = (j * tk + lax.broadcasted_iota(i32, (tq, tk), 1)) // CHUNK
            attend(jnp.where(c <= r, 0.0, NEG_INF))

    @pl.when((flag & FLAG_LAST) != 0)
    def _():
        if diff:
            lam = (jnp.exp(jnp.sum(lq1_ref[...] * lk1_ref[...], axis=1, keepdims=True))
                   - jnp.exp(jnp.sum(lq2_ref[...] * lk2_ref[...], axis=1, keepdims=True)) + lambda_init)
            for hc in range(n_heads // 2):
                a1, a2 = acc_scr[2 * hc], acc_scr[2 * hc + 1]
                o1 = a1[:, :dv] / a1[:, dv:dv + 1]
                o2 = a2[:, :dv] / a2[:, dv:dv + 1]
                o = _rms(o1 - lam * o2, gsub_ref[...]) * (1.0 - lambda_init)
                o_ref[:, hc * dv:(hc + 1) * dv] = o.astype(o_ref.dtype)
        else:
            for h in range(n_heads):
                a = acc_scr[h]
                o_ref[:, h * dv:(h + 1) * dv] = (a[:, :dv] / a[:, dv:dv + 1]).astype(o_ref.dtype)


FLASH_TILE_ELEMS = 1024 * 512
FLASH_STATS_BYTES = 12 * 1024 * 1024


def _v_pad(dv):
    return (dv // LANES + 1) * LANES


def _with_ones_column(w_v, n_heads, dv):
    k = w_v.shape[0]
    dvp = _v_pad(dv)
    w = jnp.pad(w_v.reshape(k, n_heads, dv), ((0, 0), (0, 0), (0, dvp - dv))).reshape(k, n_heads * dvp)
    one = jnp.zeros((1, n_heads, dvp), f32).at[:, :, dv].set(1.0).reshape(1, n_heads * dvp)
    return w, one


def _flash_tiles(seq, n_heads, dv):
    tq = 1024
    while tq > 128 and n_heads * tq * (LANES + _v_pad(dv)) * 4 > FLASH_STATS_BYTES:
        tq //= 2
    return min(tq, seq), min(FLASH_TILE_ELEMS // tq, seq)


def _flash(q, k, v, *, n_heads, dq, dv, v_group=1, bias=None, diff_params=None, lambda_init=0.0, name):
    seq = q.shape[0]
    tq, tk = _flash_tiles(seq, n_heads, dv)
    pairs = []
    for i in range(seq // tq):
        j_last = ((i + 1) * tq - 1) // tk
        for j in range(j_last + 1):
            needs_mask = (j + 1) * tk > i * tq + CHUNK
            pairs.append((i, j, (FLAG_FIRST if j == 0 else 0) | (FLAG_MASK if needs_mask else 0)
                          | (FLAG_LAST if j == j_last else 0)))
    it = jnp.asarray([p[0] for p in pairs], i32)
    jt = jnp.asarray([p[1] for p in pairs], i32)
    ft = jnp.asarray([p[2] for p in pairs], i32)
    n_out = (n_heads // v_group) * dv
    in_specs = [pl.BlockSpec((tq, q.shape[1]), lambda s, it, jt, ft: (it[s], 0)),
                pl.BlockSpec((tk, k.shape[1]), lambda s, it, jt, ft: (jt[s], 0)),
                pl.BlockSpec((tk, v.shape[1]), lambda s, it, jt, ft: (jt[s], 0))]
    args = [q, k, v]
    if bias is not None:
        in_specs.append(pl.BlockSpec((tq, tk), lambda s, it, jt, ft: (it[s], jt[s])))
        args.append(bias)
    if diff_params is not None:
        for a in diff_params:
            in_specs.append(pl.BlockSpec(a.shape, lambda s, it, jt, ft: (0, 0)))
            args.append(a)
    body = functools.partial(_flash_body, tq=tq, tk=tk, n_heads=n_heads, dq=dq, dv=dv, v_group=v_group,
                             has_bias=bias is not None, diff=diff_params is not None, lambda_init=lambda_init)
    return pl.pallas_call(
        body,
        grid_spec=pltpu.PrefetchScalarGridSpec(
            num_scalar_prefetch=3,
            grid=(len(pairs),),
            in_specs=in_specs,
            out_specs=pl.BlockSpec((tq, n_out), lambda s, it, jt, ft: (it[s], 0)),
            scratch_shapes=[pltpu.VMEM((n_heads, tq, LANES), f32), pltpu.VMEM((n_heads, tq, _v_pad(dv)), f32)]),
        out_shape=jax.ShapeDtypeStruct((seq, n_out), bf16),
        compiler_params=_cparams(("arbitrary",)),
        name=name,
    )(it, jt, ft, *args)


def _silu_mul(gate, up):
    return gate / (1.0 + jnp.exp(-gate)) * up


def _ffn_body(x_ref, g_ref, wg_ref, wu_ref, wd_ref, o_ref, xn_scr, acc_scr):
    f = pl.program_id(1)

    @pl.when(f == 0)
    def _():
        xn_scr[...] = _rms(x_ref[...], g_ref[...]).astype(bf16)
        acc_scr[...] = jnp.zeros(acc_scr.shape, f32)

    xn = xn_scr[...]
    act = _silu_mul(_dot(xn, wg_ref[...]), _dot(xn, wu_ref[...])).astype(bf16)
    acc_scr[...] += _dot(act, wd_ref[...])

    @pl.when(f == pl.num_programs(1) - 1)
    def _():
        o_ref[...] = x_ref[...] + acc_scr[...]


def _ffn(x, g, wg, wu, wd, *, tm, tf, name):
    seq = x.shape[0]
    dff = wg.shape[1]
    return pl.pallas_call(
        _ffn_body,
        grid=(seq // tm, dff // tf),
        in_specs=[pl.BlockSpec((tm, D_MODEL), lambda i, f: (i, 0)),
                  pl.BlockSpec((1, D_MODEL), lambda i, f: (0, 0)),
                  pl.BlockSpec((D_MODEL, tf), lambda i, f: (0, f)),
                  pl.BlockSpec((D_MODEL, tf), lambda i, f: (0, f)),
                  pl.BlockSpec((tf, D_MODEL), lambda i, f: (f, 0))],
        out_specs=pl.BlockSpec((tm, D_MODEL), lambda i, f: (i, 0)),
        out_shape=jax.ShapeDtypeStruct((seq, D_MODEL), f32),
        scratch_shapes=[pltpu.VMEM((tm, D_MODEL), bf16), pltpu.VMEM((tm, D_MODEL), f32)],
        compiler_params=_cparams(("parallel", "arbitrary")),
        name=name,
    )(x, g, wg, wu, wd)


MOE_TILE = 512
PAIR_FIRST, PAIR_VALID, PAIR_LAST = 1, 2, 4


def _route(x, g, w_router):
    seq = x.shape[0]
    tm = min(MOE_TILE, seq)
    nb = seq // tm
    w_r = _pad_cols(w_router, LANES)
    w_hi = w_r.astype(bf16)
    w_lo = (w_r - w_hi.astype(f32)).astype(bf16)
    row = lambda n: pl.BlockSpec((tm, n), lambda i: (i, 0))
    const = lambda a: pl.BlockSpec(a.shape, lambda i: (0, 0))
    return pl.pallas_call(
        _router_body,
        grid=(nb,),
        in_specs=[row(D_MODEL), const(g), const(w_hi), const(w_lo)],
        out_specs=[row(D_MODEL), row(LANES), row(LANES), pl.BlockSpec((1, 8, LANES), lambda i: (i, 0, 0))],
        out_shape=[jax.ShapeDtypeStruct((seq, D_MODEL), bf16), jax.ShapeDtypeStruct((seq, LANES), f32),
                   jax.ShapeDtypeStruct((seq, LANES), f32), jax.ShapeDtypeStruct((nb, 8, LANES), f32)],
        scratch_shapes=[pltpu.VMEM((8, LANES), f32)],
        compiler_params=_cparams(("arbitrary",)),
        name="router",
    )(x, g, w_hi, w_lo)


def _moe_schedule(counts_after, seq):
    tm = min(MOE_TILE, seq)
    nb = seq // tm
    n_tiles = 2 * nb + N_EXP
    kmax = nb + 1
    max_pairs = n_tiles + N_EXP * nb
    bounds = jnp.concatenate([jnp.zeros((1, N_EXP), i32), counts_after[:, 0, :N_EXP].astype(i32)], 0)
    cnt = bounds[-1]
    ntile = (cnt + tm - 1) // tm
    tile_end = jnp.cumsum(ntile)
    tile_start = tile_end - ntile
    n_valid = tile_end[-1]
    p_ids = jnp.minimum(jnp.arange(n_tiles, dtype=i32), n_valid - 1)
    tile_expert = jnp.minimum(jnp.sum(p_ids[:, None] >= tile_end[None, :], axis=1), N_EXP - 1).astype(i32)
    tile_valid = (jnp.arange(n_tiles, dtype=i32) < n_valid).astype(i32)

    lo = jnp.transpose(bounds[:-1])[:, None, :]
    hi = jnp.transpose(bounds[1:])[:, None, :]
    k0 = (jnp.arange(kmax, dtype=i32) * tm)[None, :, None]
    meet = jnp.maximum(k0, lo) < jnp.minimum(k0 + tm, hi)
    n_pairs = jnp.sum(meet)
    s_ids = jnp.arange(max_pairs, dtype=i32)
    s_eff = jnp.minimum(s_ids, n_pairs - 1)

    def pair_list(flat, decode, group_of):
        idx = jnp.nonzero(flat, size=max_pairs, fill_value=0)[0].astype(i32)[s_eff]
        e, k, b = decode(idx)
        p = tile_start[e] + k
        grp = group_of(p, b)
        valid = s_ids < n_pairs
        first = jnp.concatenate([jnp.ones((1,), bool), grp[1:] != grp[:-1]])
        last = jnp.concatenate([grp[1:] != grp[:-1], jnp.ones((1,), bool)]) | (s_ids == n_pairs - 1)
        flags = jnp.where(valid, PAIR_VALID + PAIR_FIRST * first + PAIR_LAST * last, 0).astype(i32)
        return p.astype(i32), b.astype(i32), e.astype(i32), k.astype(i32), flags

    tile_major = pair_list(meet.reshape(-1),
                           lambda i: (i // (kmax * nb), (i // nb) % kmax, i % nb), lambda p, b: p)
    block_major = pair_list(jnp.transpose(meet, (2, 0, 1)).reshape(-1),
                            lambda i: ((i // kmax) % N_EXP, i % kmax, i // (N_EXP * kmax)), lambda p, b: b)
    return n_tiles, tile_expert, tile_valid, tile_major, block_major


def _moe_gather_body(pt, pb, pe, pk, pf, xn_ref, rank_t_ref, o_ref):
    s = pl.program_id(0)
    flag = pf[s]
    tmg, tb = o_ref.shape[0], xn_ref.shape[0]

    @pl.when((flag & PAIR_FIRST) != 0)
    def _():
        o_ref[...] = jnp.zeros(o_ref.shape, o_ref.dtype)

    @pl.when((flag & PAIR_VALID) != 0)
    def _():
        r = rank_t_ref[pl.ds(pe[s], 1), :] - (pk[s] * tmg).astype(f32)
        rows = lax.broadcasted_iota(i32, (tmg, tb), 0).astype(f32)
        onehot = jnp.where(rows == r, 1.0, 0.0).astype(bf16)
        o_ref[...] = o_ref[...] + _dot(onehot, xn_ref[...]).astype(o_ref.dtype)


def _moe_ffn_body(te, tv, x_ref, wg_ref, wu_ref, wd_ref, y_ref, acc_scr):
    p = pl.program_id(0)
    f = pl.program_id(1)
    last_f = f == pl.num_programs(1) - 1

    @pl.when(tv[p] != 0)
    def _():
        @pl.when(f == 0)
        def _():
            acc_scr[...] = jnp.zeros(acc_scr.shape, f32)

        x = x_ref[...]
        act = _silu_mul(_dot(x, wg_ref[0]), _dot(x, wu_ref[0])).astype(bf16)
        acc_scr[...] += _dot(act, wd_ref[0])

        @pl.when(last_f)
        def _():
            y_ref[...] = acc_scr[...].astype(y_ref.dtype)

    @pl.when(jnp.logical_and(tv[p] == 0, last_f))
    def _():
        y_ref[...] = jnp.zeros(y_ref.shape, y_ref.dtype)


def _moe_combine_body(ct, cb, ce, ck, cf, h_ref, y_ref, rank_ref, comb_ref, fg_ref, o_ref, acc_scr):
    s = pl.program_id(0)
    flag = cf[s]
    tb, tmg = h_ref.shape[0], y_ref.shape[0]

    @pl.when((flag & PAIR_FIRST) != 0)
    def _():
        acc_scr[...] = h_ref[...]

    @pl.when((flag & PAIR_VALID) != 0)
    def _():
        lane = lax.broadcasted_iota(i32, rank_ref.shape, 1)
        mine = lane == ce[s]
        r = jnp.sum(jnp.where(mine, rank_ref[...], 0.0), axis=1, keepdims=True) - (ck[s] * tmg).astype(f32)
        gate = jnp.sum(jnp.where(mine, comb_ref[...], 0.0), axis=1, keepdims=True)
        cols = lax.broadcasted_iota(i32, (tb, tmg), 1).astype(f32)
        onehot = jnp.where(cols == r, 1.0, 0.0).astype(bf16)
        acc_scr[...] += gate * _dot(onehot, y_ref[...])

    @pl.when((flag & PAIR_LAST) != 0)
    def _():
        o_ref[...] = _rms(acc_scr[...], fg_ref[...])


def _moe(h, g_ffn, w_router, w_gate_e, w_up_e, w_down_e, final_g):
    seq = h.shape[0]
    tm = min(MOE_TILE, seq)
    xn, comb, rank, counts_after = _route(h, g_ffn, w_router)
    n_tiles, tile_expert, tile_valid, tile_major, block_major = _moe_schedule(counts_after, seq)
    n_pairs = tile_major[0].shape[0]
    rank_t = jnp.transpose(rank[:, :8])

    x_sorted = pl.pallas_call(
        _moe_gather_body,
        grid_spec=pltpu.PrefetchScalarGridSpec(
            num_scalar_prefetch=5, grid=(n_pairs,),
            in_specs=[pl.BlockSpec((tm, D_MODEL), lambda s, pt, pb, pe, pk, pf: (pb[s], 0)),
                      pl.BlockSpec((8, tm), lambda s, pt, pb, pe, pk, pf: (0, pb[s]))],
            out_specs=pl.BlockSpec((tm, D_MODEL), lambda s, pt, pb, pe, pk, pf: (pt[s], 0))),
        out_shape=jax.ShapeDtypeStruct((n_tiles * tm, D_MODEL), bf16),
        compiler_params=_cparams(("arbitrary",)),
        name="moe_gather",
    )(*tile_major, xn, rank_t)

    tf = D_FF_E // 4
    n_f = D_FF_E // tf
    f_eff = lambda f, v: f * v + (n_f - 1) * (1 - v)
    y_sorted = pl.pallas_call(
        _moe_ffn_body,
        grid_spec=pltpu.PrefetchScalarGridSpec(
            num_scalar_prefetch=2, grid=(n_tiles, n_f),
            in_specs=[pl.BlockSpec((tm, D_MODEL), lambda p, f, te, tv: (p, 0)),
                      pl.BlockSpec((1, D_MODEL, tf), lambda p, f, te, tv: (te[p], 0, f_eff(f, tv[p]))),
                      pl.BlockSpec((1, D_MODEL, tf), lambda p, f, te, tv: (te[p], 0, f_eff(f, tv[p]))),
                      pl.BlockSpec((1, tf, D_MODEL), lambda p, f, te, tv: (te[p], f_eff(f, tv[p]), 0))],
            out_specs=pl.BlockSpec((tm, D_MODEL), lambda p, f, te, tv: (p, 0)),
            scratch_shapes=[pltpu.VMEM((tm, D_MODEL), f32)]),
        out_shape=jax.ShapeDtypeStruct((n_tiles * tm, D_MODEL), bf16),
        compiler_params=_cparams(("arbitrary", "arbitrary")),
        name="moe_ffn",
    )(tile_expert, tile_valid, x_sorted, w_gate_e.astype(bf16), w_up_e.astype(bf16), w_down_e.astype(bf16))

    return pl.pallas_call(
        _moe_combine_body,
        grid_spec=pltpu.PrefetchScalarGridSpec(
            num_scalar_prefetch=5, grid=(n_pairs,),
            in_specs=[pl.BlockSpec((tm, D_MODEL), lambda s, ct, cb, ce, ck, cf: (cb[s], 0)),
                      pl.BlockSpec((tm, D_MODEL), lambda s, ct, cb, ce, ck, cf: (ct[s], 0)),
                      pl.BlockSpec((tm, LANES), lambda s, ct, cb, ce, ck, cf: (cb[s], 0)),
                      pl.BlockSpec((tm, LANES), lambda s, ct, cb, ce, ck, cf: (cb[s], 0)),
                      pl.BlockSpec((1, D_MODEL), lambda s, ct, cb, ce, ck, cf: (0, 0))],
            out_specs=pl.BlockSpec((tm, D_MODEL), lambda s, ct, cb, ce, ck, cf: (cb[s], 0)),
            scratch_shapes=[pltpu.VMEM((tm, D_MODEL), f32)]),
        out_shape=jax.ShapeDtypeStruct((seq, D_MODEL), f32),
        compiler_params=_cparams(("arbitrary",)),
        name="moe_combine",
    )(*block_major, h, y_sorted, rank, comb, final_g)


def _even_layer(h, norm_mix, w_in, g_q_lat, w_uq, g_kv_lat, w_ukv, g_idx_k, w_out, norm_ffn, w_gate, w_up, w_down):
    seq = h.shape[0]
    sizes = (Q_LORA, KV_LORA, ROPE_A, H_B * DH_B, H_B * DH_B, H_B * DH_B, H_IDX * D_IDX, D_IDX, H_IDX)
    offs = np.cumsum((0,) + sizes)
    w_cq, w_ckv, w_kr, w_qb, w_kb, w_vb, w_qi, w_ki, w_wi = [w_in[:, offs[n]:offs[n + 1]] for n in range(9)]

    scale_a = (NOPE_A + ROPE_A) ** -0.5 * LOG2E
    wq3 = (w_uq * scale_a).reshape(Q_LORA, H_A, NOPE_A + ROPE_A)
    wq = jnp.pad(wq3, ((0, 0), (0, 0), (0, HEAD_PAD_A - NOPE_A - ROPE_A))).reshape(Q_LORA, H_A * HEAD_PAD_A)
    wq_sw = _swap_cols(wq, HEAD_PAD_A, NOPE_A, ROPE_A)
    wkv3 = w_ukv.reshape(KV_LORA, H_A, NOPE_A + V_A)
    wk = jnp.pad(wkv3[:, :, :NOPE_A], ((0, 0), (0, 0), (0, HEAD_PAD_A - NOPE_A))).reshape(KV_LORA, H_A * HEAD_PAD_A)
    wv, one_a = _with_ones_column(wkv3[:, :, NOPE_A:].reshape(KV_LORA, H_A * V_A), H_A, V_A)
    w_vb, one_b = _with_ones_column(w_vb, H_B, DH_B)
    place = jnp.zeros((ROPE_A, H_A, HEAD_PAD_A), f32)
    place = place.at[:, :, NOPE_A:NOPE_A + ROPE_A].set(jnp.eye(ROPE_A, dtype=f32)[:, None, :])
    place = place.reshape(ROPE_A, H_A * HEAD_PAD_A)

    w_qb = w_qb * (DH_B ** -0.5 * LOG2E)
    w_qi = w_qi * D_IDX ** -0.5
    w_small = jnp.concatenate([w_kr, _swap_cols(w_kr, ROPE_A, 0, ROPE_A), w_ki, _swap_cols(w_ki, D_IDX, 0, ROT_IDX)], 1)
    w_wi_p = _pad_cols(w_wi * H_IDX ** -0.5, LANES)
    g_idx = g_idx_k.reshape(1, D_IDX)
    g_idx_sw = jnp.concatenate([g_idx[:, ROT_IDX // 2:ROT_IDX], g_idx[:, :ROT_IDX // 2], g_idx[:, ROT_IDX:]], 1)

    ca, sa = _rope_tables(seq, ROPE_A, HEAD_PAD_A, NOPE_A, LANES)
    cb, sb = _rope_tables(seq, ROT_B, DH_B, 0, LANES)
    ci, si = _rope_tables(seq, ROT_IDX, D_IDX, 0, LANES)
    ckr, skr = _rope_tables(seq, ROPE_A, ROPE_A, 0, ROPE_A)

    consts = [norm_mix.reshape(1, -1), g_q_lat.reshape(1, -1), g_kv_lat.reshape(1, -1), g_idx, g_idx_sw,
              one_a, one_b]
    weights = [jnp.concatenate([w_cq, w_ckv], 1), wq, wq_sw, wk, wv, place,
               w_qb, _swap_cols(w_qb, DH_B, 0, ROT_B), w_kb, _swap_cols(w_kb, DH_B, 0, ROT_B), w_vb,
               w_qi, _swap_cols(w_qi, D_IDX, 0, ROT_IDX), w_small, w_wi_p]
    weights = [w.astype(bf16) for w in weights]
    sds = lambda n, dt: jax.ShapeDtypeStruct((seq, n), dt)
    outs = [sds(H_A * HEAD_PAD_A, bf16), sds(H_A * HEAD_PAD_A, bf16), sds(H_A * _v_pad(V_A), bf16),
            sds(H_B * DH_B, bf16), sds(H_B * DH_B, bf16), sds(H_B * _v_pad(DH_B), bf16),
            sds(H_IDX * D_IDX, bf16), sds(D_IDX, bf16), sds(LANES, f32)]
    qa, ka, va, qb, kb, vb, qi, ki, wi = _rows_call(
        _even_proj_body, seq, 256, [h, ca, sa, cb, sb, ci, si, ckr, skr], consts + weights, outs, "even_proj")

    o_a = _flash(qa, ka, va, n_heads=H_A, dq=HEAD_PAD_A, dv=V_A, name="mla_attn")
    top_k = min(TOPK_MAX, seq // 4)
    bias = _dsa_select(qi, wi, ki.T, seq, top_k)
    o_b = _flash(qb, kb, vb, n_heads=H_B, dq=DH_B, dv=DH_B, bias=bias, name="dsa_attn")

    w_out = w_out.astype(bf16)
    n_a = H_A * V_A
    (h,) = _rows_call(_out_proj2_body, seq, 512, [h, o_a, o_b], [w_out[:n_a], w_out[n_a:]],
                      [jax.ShapeDtypeStruct((seq, D_MODEL), f32)], "even_out_proj")
    return _ffn(h, norm_ffn.reshape(1, -1), w_gate.astype(bf16), w_up.astype(bf16), w_down.astype(bf16),
                tm=512, tf=D_FF // 2, name="dense_ffn")


def _odd_layer(h, layer, norm_mix, w_qkv, lq1, lk1, lq2, lk2, g_sub, w_out, norm_ffn, w_router, w_gate_e, w_up_e,
               w_down_e, final_norm):
    seq = h.shape[0]
    lambda_init = 0.8 - 0.6 * math.exp(-0.3 * layer)
    n = H_C * 2 * DH_C
    w_q = w_qkv[:, :n] * (DH_C ** -0.5 * LOG2E)
    w_k = w_qkv[:, n:2 * n]
    w_v, one_v = _with_ones_column(w_qkv[:, 2 * n:], H_C, 2 * DH_C)
    cb, sb = _rope_tables(seq, ROT_C, DH_C, 0, LANES)
    weights = [w_q, _swap_cols(w_q, DH_C, 0, ROT_C), w_k, _swap_cols(w_k, DH_C, 0, ROT_C), w_v]
    weights = [w.astype(bf16) for w in weights]
    sds = jax.ShapeDtypeStruct((seq, n), bf16)
    sds_v = jax.ShapeDtypeStruct((seq, w_v.shape[1]), bf16)
    q, k, v = _rows_call(_odd_proj_body, seq, 512, [h, cb, sb], [norm_mix.reshape(1, -1), one_v] + weights,
                         [sds, sds, sds_v], "odd_proj")
    diff_params = [lq1.reshape(1, -1), lk1.reshape(1, -1), lq2.reshape(1, -1), lk2.reshape(1, -1),
                   g_sub.reshape(1, -1)]
    o = _flash(q, k, v, n_heads=2 * H_C, dq=DH_C, dv=2 * DH_C, v_group=2, diff_params=diff_params,
               lambda_init=lambda_init, name="diff_attn")
    (h,) = _rows_call(_out_proj1_body, seq, 512, [h, o], [w_out.astype(bf16)],
                      [jax.ShapeDtypeStruct((seq, D_MODEL), f32)], "odd_out_proj")
    return _moe(h, norm_ffn.reshape(1, -1), w_router, w_gate_e, w_up_e, w_down_e, final_norm.reshape(1, -1))


def kernel(x, ev_norm_mix, ev_w_in, ev_g_q_lat, ev_w_uq, ev_g_kv_lat, ev_w_ukv, ev_g_idx_k, ev_w_out, ev_norm_ffn, ev_w_gate, ev_w_up, ev_w_down, od_norm_mix, od_w_qkv, od_lambda_q1, od_lambda_k1, od_lambda_q2, od_lambda_k2, od_g_sub, od_w_out, od_norm_ffn, od_w_router, od_w_gate_e, od_w_up_e, od_w_down_e, final_norm):
    batch, seq, _ = x.shape
    assert batch == 1 and ev_w_in.shape[0] == 1 and od_w_qkv.shape[0] == 1
    h = x[0]
    h = _even_layer(h, ev_norm_mix[0], ev_w_in[0], ev_g_q_lat[0], ev_w_uq[0], ev_g_kv_lat[0], ev_w_ukv[0],
                    ev_g_idx_k[0], ev_w_out[0], ev_norm_ffn[0], ev_w_gate[0], ev_w_up[0], ev_w_down[0])
    h = _odd_layer(h, 1, od_norm_mix[0], od_w_qkv[0], od_lambda_q1[0], od_lambda_k1[0], od_lambda_q2[0],
                   od_lambda_k2[0], od_g_sub[0], od_w_out[0], od_norm_ffn[0], od_w_router[0], od_w_gate_e[0],
                   od_w_up_e[0], od_w_down_e[0], final_norm)
    return h[None]
```

```python
import functools
import math

import numpy as np
import jax
import jax.numpy as jnp
from jax import lax
from jax.experimental import pallas as pl
from jax.experimental.pallas import tpu as pltpu

f32 = jnp.float32
bf16 = jnp.bfloat16
i32 = jnp.int32
i16 = jnp.int16

D_MODEL = 1024
CHUNK = 64
ROPE_THETA = 500000.0
NORM_EPS = 1e-6
NEG_INF = -1e30
LOG2E = math.log2(math.e)

H_A, Q_LORA, KV_LORA, NOPE_A, ROPE_A, V_A = 8, 256, 128, 64, 32, 64
H_B, DH_B, ROT_B = 8, 64, 16
H_IDX, D_IDX, ROT_IDX = 8, 32, 8
TOPK_MAX = 256
H_C, DH_C, ROT_C = 8, 64, 16
D_FF, N_EXP, D_FF_E = 2816, 8, 3584

LANES = 128
HEAD_PAD_A = 128

_NEG_BITS = int(np.float32(NEG_INF).view(np.int32))
NEG_KEY = _NEG_BITS ^ 0x7FFFFFFF
INT_MIN = -(2 ** 31)

VMEM_LIMIT = 56 * 1024 * 1024


def _cparams(sem):
    return pltpu.CompilerParams(dimension_semantics=sem, vmem_limit_bytes=VMEM_LIMIT)


def _rms(x, g):
    var = jnp.mean(x * x, axis=-1, keepdims=True)
    return x * lax.rsqrt(var + NORM_EPS) * g


def _dot(a, b):
    return jnp.dot(a, b, preferred_element_type=f32)


def _dot_nt(a, b):
    return lax.dot_general(a, b, (((1,), (1,)), ((), ())), preferred_element_type=f32)


def _rope_tables(seq, rot_dim, head_width, offset, width):
    pos = jnp.arange(seq, dtype=f32)
    inv_freq = ROPE_THETA ** (-jnp.arange(0, rot_dim, 2, dtype=f32) / rot_dim)
    ang = pos[:, None] * inv_freq[None, :]
    cos, sin = jnp.cos(ang), jnp.sin(ang)
    c = jnp.ones((seq, head_width), f32).at[:, offset:offset + rot_dim].set(jnp.concatenate([cos, cos], -1))
    s = jnp.zeros((seq, head_width), f32).at[:, offset:offset + rot_dim].set(jnp.concatenate([-sin, sin], -1))
    reps = width // head_width
    return jnp.tile(c, (1, reps)), jnp.tile(s, (1, reps))


def _swap_cols(w, head_width, offset, rot_dim):
    k, n = w.shape
    half = rot_dim // 2
    w3 = w.reshape(k, n // head_width, head_width)
    out = jnp.zeros_like(w3)
    out = out.at[:, :, offset:offset + half].set(w3[:, :, offset + half:offset + rot_dim])
    out = out.at[:, :, offset + half:offset + rot_dim].set(w3[:, :, offset:offset + half])
    return out.reshape(k, n)


def _pad_cols(w, width):
    return jnp.pad(w, ((0, 0), (0, width - w.shape[1])))


def _rows_call(body, seq, tm, row_ins, const_ins, out_sds, name):
    def rspec(a):
        return pl.BlockSpec((tm, a.shape[1]), lambda i: (i, 0))

    def cspec(a):
        nd = a.ndim
        return pl.BlockSpec(a.shape, lambda i: (0,) * nd)

    return pl.pallas_call(
        body,
        grid=(seq // tm,),
        in_specs=[rspec(a) for a in row_ins] + [cspec(a) for a in const_ins],
        out_specs=[pl.BlockSpec((tm, o.shape[1]), lambda i: (i, 0)) for o in out_sds],
        out_shape=out_sds,
        compiler_params=_cparams(("parallel",)),
        name=name,
    )(*row_ins, *const_ins)


def _even_proj_body(x_ref, ca_ref, sa_ref, cb_ref, sb_ref, ci_ref, si_ref, ckr_ref, skr_ref,
                    g_ref, gq_ref, gkv_ref, gi_ref, gisw_ref, onea_ref, oneb_ref,
                    wlat_ref, wq_ref, wqsw_ref, wk_ref, wv_ref, place_ref,
                    wqb_ref, wqbsw_ref, wkb_ref, wkbsw_ref, wvb_ref, wqi_ref, wqisw_ref, wsm_ref, wwi_ref,
                    qa_ref, ka_ref, va_ref, qb_ref, kb_ref, vb_ref, qi_ref, ki_ref, wi_ref):
    xn = _rms(x_ref[...], g_ref[...]).astype(bf16)
    lat = _dot(xn, wlat_ref[...])
    cqn = _rms(lat[:, :Q_LORA], gq_ref[...]).astype(bf16)
    ckvn = _rms(lat[:, Q_LORA:], gkv_ref[...]).astype(bf16)
    reps_a = qa_ref.shape[1] // LANES
    ca = jnp.tile(ca_ref[...], (1, reps_a))
    sa = jnp.tile(sa_ref[...], (1, reps_a))
    qa_ref[...] = (_dot(cqn, wq_ref[...]) * ca + _dot(cqn, wqsw_ref[...]) * sa).astype(bf16)
    small = _dot(xn, wsm_ref[...])
    kr, kr_sw = small[:, 0:ROPE_A], small[:, ROPE_A:2 * ROPE_A]
    kpe = (kr * ckr_ref[...] + kr_sw * skr_ref[...]).astype(bf16)
    ka_ref[...] = (_dot(ckvn, wk_ref[...]) + _dot(kpe, place_ref[...])).astype(bf16)
    va_ref[...] = (_dot(ckvn, wv_ref[...]) + onea_ref[...]).astype(bf16)
    reps_b = qb_ref.shape[1] // LANES
    cb = jnp.tile(cb_ref[...], (1, reps_b))
    sb = jnp.tile(sb_ref[...], (1, reps_b))
    qb_ref[...] = (_dot(xn, wqb_ref[...]) * cb + _dot(xn, wqbsw_ref[...]) * sb).astype(bf16)
    kb_ref[...] = (_dot(xn, wkb_ref[...]) * cb + _dot(xn, wkbsw_ref[...]) * sb).astype(bf16)
    vb_ref[...] = (_dot(xn, wvb_ref[...]) + oneb_ref[...]).astype(bf16)
    reps_i = qi_ref.shape[1] // LANES
    ci = jnp.tile(ci_ref[...], (1, reps_i))
    si = jnp.tile(si_ref[...], (1, reps_i))
    qi_ref[...] = (_dot(xn, wqi_ref[...]) * ci + _dot(xn, wqisw_ref[...]) * si).astype(bf16)
    ki, ki_sw = small[:, 2 * ROPE_A:2 * ROPE_A + D_IDX], small[:, 2 * ROPE_A + D_IDX:2 * ROPE_A + 2 * D_IDX]
    r = lax.rsqrt(jnp.mean(ki * ki, axis=-1, keepdims=True) + NORM_EPS)
    ci32, si32 = ci_ref[:, 0:D_IDX], si_ref[:, 0:D_IDX]
    ki_ref[...] = (ki * r * gi_ref[...] * ci32 + ki_sw * r * gisw_ref[...] * si32).astype(bf16)
    wi_ref[...] = _dot(xn, wwi_ref[...])


def _odd_proj_body(x_ref, cb_ref, sb_ref, g_ref, one_ref, wq_ref, wqsw_ref, wk_ref, wksw_ref, wv_ref,
                   q_ref, k_ref, v_ref):
    xn = _rms(x_ref[...], g_ref[...]).astype(bf16)
    reps = q_ref.shape[1] // LANES
    cb = jnp.tile(cb_ref[...], (1, reps))
    sb = jnp.tile(sb_ref[...], (1, reps))
    q_ref[...] = (_dot(xn, wq_ref[...]) * cb + _dot(xn, wqsw_ref[...]) * sb).astype(bf16)
    k_ref[...] = (_dot(xn, wk_ref[...]) * cb + _dot(xn, wksw_ref[...]) * sb).astype(bf16)
    v_ref[...] = (_dot(xn, wv_ref[...]) + one_ref[...]).astype(bf16)


def _out_proj2_body(x_ref, a1_ref, a2_ref, w1_ref, w2_ref, o_ref):
    o_ref[...] = x_ref[...] + _dot(a1_ref[...], w1_ref[...]) + _dot(a2_ref[...], w2_ref[...])


def _out_proj1_body(x_ref, a_ref, w_ref, o_ref):
    o_ref[...] = x_ref[...] + _dot(a_ref[...], w_ref[...])


def _router_body(x_ref, g_ref, whi_ref, wlo_ref, xn_ref, comb_ref, rank_ref, cnt_ref, carry_scr):
    @pl.when(pl.program_id(0) == 0)
    def _():
        carry_scr[...] = jnp.zeros(carry_scr.shape, f32)

    xn = _rms(x_ref[...], g_ref[...])
    xn_ref[...] = xn.astype(bf16)
    hi = xn.astype(bf16)
    lo = (xn - hi.astype(f32)).astype(bf16)
    logits = _dot(hi, whi_ref[...]) + _dot(lo, whi_ref[...]) + _dot(hi, wlo_ref[...])
    lane = lax.broadcasted_iota(i32, logits.shape, 1).astype(f32)
    lg = jnp.where(lane < N_EXP, logits, -jnp.inf)
    m1 = jnp.max(lg, axis=1, keepdims=True)
    i1 = jnp.min(jnp.where(lg == m1, lane, float(LANES)), axis=1, keepdims=True)
    lg2 = jnp.where(lane == i1, -jnp.inf, lg)
    m2 = jnp.max(lg2, axis=1, keepdims=True)
    i2 = jnp.min(jnp.where(lg2 == m2, lane, float(LANES)), axis=1, keepdims=True)
    e2 = jnp.exp(m2 - m1)
    den = 1.0 + e2
    comb_ref[...] = jnp.where(lane == i1, 1.0 / den, 0.0) + jnp.where(lane == i2, e2 / den, 0.0)
    routed = jnp.logical_or(lane == i1, lane == i2)
    onehot = jnp.where(routed, 1.0, 0.0)
    tm = onehot.shape[0]
    earlier = lax.broadcasted_iota(i32, (tm, tm), 1) < lax.broadcasted_iota(i32, (tm, tm), 0)
    before = _dot(jnp.where(earlier, 1.0, 0.0).astype(bf16), onehot.astype(bf16))
    carry = carry_scr[0:1, :]
    rank_ref[...] = jnp.where(routed, before + carry, -1.0)
    carry = carry + jnp.sum(onehot, axis=0, keepdims=True)
    carry_scr[...] = jnp.broadcast_to(carry, carry_scr.shape)
    cnt_ref[0] = jnp.broadcast_to(carry, cnt_ref.shape[1:])


SEL_COUNT_ROWS, SEL_COUNT_COLS = 64, 1024
TIE_NONE = 2 ** 15 - 1


def _dsa_select_body(qi_ref, wi_ref, kit_ref, out_ref, keys_scr, hi16_scr, low16_scr, *, tq, tk, top_k, idx_bits):
    q0 = pl.program_id(0) * tq
    n_ktc = (q0 + tq + SEL_COUNT_COLS - 1) // SEL_COUNT_COLS
    n_kt = n_ktc * (SEL_COUNT_COLS // tk)
    qh = [qi_ref[:, h * D_IDX:(h + 1) * D_IDX] for h in range(H_IDX)]
    w = wi_ref[...]
    wb = [jnp.broadcast_to(w[:, h:h + 1], (tq, tk)) for h in range(H_IDX)]
    row = q0 + lax.broadcasted_iota(i32, (tq, 1), 0)
    row_lim = (row // CHUNK + 1) * CHUNK

    def cols_of(kt):
        c0 = pl.multiple_of(kt * tk, tk)
        return c0, c0 + lax.broadcasted_iota(i32, (tq, tk), 1)

    def score_tile(kt, carry):
        c0, col = cols_of(kt)
        kt_tile = kit_ref[:, pl.ds(c0, tk)]
        acc = jnp.zeros((tq, tk), f32)
        for h in range(H_IDX):
            acc = acc + jnp.maximum(_dot(qh[h], kt_tile), 0.0) * wb[h]
        sc = jnp.where(col < row_lim, acc, NEG_INF)
        bits = lax.bitcast_convert_type(sc, i32)
        key = jnp.where(bits < 0, bits ^ 0x7FFFFFFF, bits)
        keys_scr[:, pl.ds(c0, tk)] = key
        hi16_scr[:, pl.ds(c0, tk)] = lax.shift_right_arithmetic(key, 16).astype(i16)
        return carry

    lax.fori_loop(0, n_kt, score_tile, 0)

    ones_mat = jnp.ones((LANES, LANES), bf16)

    def count_ge(cand_rep):
        parts = []
        for rg in range(tq // SEL_COUNT_ROWS):
            rows = slice(rg * SEL_COUNT_ROWS, (rg + 1) * SEL_COUNT_ROWS)
            cand = cand_rep[rows]

            def body(kt, acc, rows=rows, cand=cand):
                c0 = pl.multiple_of(kt * SEL_COUNT_COLS, SEL_COUNT_COLS)
                ks = keys_scr[rows, pl.ds(c0, SEL_COUNT_COLS)]
                for u in range(SEL_COUNT_COLS // LANES):
                    acc = acc + jnp.where(ks[:, u * LANES:(u + 1) * LANES] >= cand, 1.0, 0.0)
                return acc

            parts.append(lax.fori_loop(0, n_ktc, body, jnp.zeros((SEL_COUNT_ROWS, LANES), f32)))
        acc = jnp.concatenate(parts, axis=0)
        return _dot(acc.astype(bf16), ones_mat)

    rows16 = 2 * SEL_COUNT_ROWS
    groups16 = [slice(g * rows16, (g + 1) * rows16) for g in range(tq // rows16)]

    def count16(scr, cand_rep, strict, trips=None):
        parts = []
        for g, rows in enumerate(groups16):
            cand = cand_rep[rows].astype(i16)

            def body(kt, acc, rows=rows, cand=cand):
                c0 = pl.multiple_of(kt * SEL_COUNT_COLS, SEL_COUNT_COLS)
                ks = scr[rows, pl.ds(c0, SEL_COUNT_COLS)]
                for u in range(SEL_COUNT_COLS // LANES):
                    blk = ks[:, u * LANES:(u + 1) * LANES]
                    hit = blk > cand if strict else blk >= cand
                    acc = acc + jnp.where(hit, jnp.int16(1), jnp.int16(0))
                return acc

            n = n_ktc if trips is None else trips[g]
            parts.append(lax.fori_loop(0, n, body, jnp.zeros((rows16, LANES), i16)))
        acc = jnp.concatenate(parts, axis=0)
        return _dot(acc.astype(f32).astype(bf16), ones_mat)

    def thr_step_hi(it, u):
        cand_u = u | lax.shift_left(jnp.int32(1), 31 - it)
        cnt = count16(hi16_scr, lax.shift_right_arithmetic(cand_u ^ INT_MIN, 16), False)
        return jnp.where(cnt >= top_k, cand_u, u)

    u_hi = lax.fori_loop(0, 16, thr_step_hi, jnp.zeros((tq, LANES), i32))
    thr_hi = lax.shift_right_arithmetic(u_hi ^ INT_MIN, 16)
    n_above = count16(hi16_scr, thr_hi, True)

    def low_tile(kt, carry):
        c0 = pl.multiple_of(kt * tk, tk)
        low = ((keys_scr[:, pl.ds(c0, tk)] & 0xFFFF) - 32768).astype(i16)
        same = hi16_scr[:, pl.ds(c0, tk)] == jnp.tile(thr_hi, (1, tk // LANES)).astype(i16)
        low16_scr[:, pl.ds(c0, tk)] = jnp.where(same, low, jnp.int16(-32768))
        return carry

    lax.fori_loop(0, n_kt, low_tile, 0)

    def thr_step_lo(it, u):
        cand_u = u | lax.shift_left(jnp.int32(1), 31 - it)
        cnt = n_above + count16(low16_scr, (cand_u & 0xFFFF) - 32768, False)
        return jnp.where(cnt >= top_k, cand_u, u)

    thr_rep = lax.fori_loop(16, 32, thr_step_lo, u_hi) ^ INT_MIN
    n_ge = count_ge(thr_rep)[:, 0:1].astype(i32)
    thr = thr_rep[:, 0:1]
    thr_vis = jnp.maximum(thr, NEG_KEY + 1)
    excess = jnp.logical_and(n_ge > top_k, thr > NEG_KEY)
    any_excess = jnp.max(excess.astype(f32)) > 0.0
    out_ref[...] = jnp.full(out_ref.shape, NEG_INF, bf16)

    @pl.when(jnp.logical_not(any_excess))
    def _():
        def write_tile(kt, carry):
            c0, _ = cols_of(kt)
            ks = keys_scr[:, pl.ds(c0, tk)]
            out_ref[:, pl.ds(c0, tk)] = jnp.where(ks >= thr_vis, 0.0, NEG_INF).astype(bf16)
            return carry

        lax.fori_loop(0, n_kt, write_tile, 0)

    @pl.when(any_excess)
    def _():
        excess_f = excess.astype(f32)
        trips = [jnp.where(jnp.max(excess_f[rows]) > 0.0, n_ktc, 0) for rows in groups16]
        n_gt = n_above + count16(low16_scr, (thr_rep & 0xFFFF) - 32768, True, trips)
        need = top_k - n_gt

        def tie_tile(kt, carry):
            c0, col = cols_of(kt)
            hi16_scr[:, pl.ds(c0, tk)] = jnp.where(keys_scr[:, pl.ds(c0, tk)] == thr, col, TIE_NONE).astype(i16)
            return carry

        lax.fori_loop(0, n_kt, tie_tile, 0)
        n_cols = (n_ktc * SEL_COUNT_COLS).astype(f32)

        def tie_step(it, xv):
            cand = xv | lax.shift_left(jnp.int32(1), idx_bits - 1 - it)
            ties_below = n_cols - count16(hi16_scr, cand, False, trips)
            return jnp.where(ties_below < need, cand, xv)

        x_rep = lax.fori_loop(0, idx_bits, tie_step, jnp.zeros((tq, LANES), i32))
        xlim = jnp.where(excess, x_rep[:, 0:1], TIE_NONE - 1)

        def write_tile(kt, carry):
            c0, _ = cols_of(kt)
            ks = keys_scr[:, pl.ds(c0, tk)]
            tie_idx = hi16_scr[:, pl.ds(c0, tk)].astype(i32)
            sel = jnp.logical_and(jnp.logical_or(ks > thr, tie_idx <= xlim), ks >= thr_vis)
            out_ref[:, pl.ds(c0, tk)] = jnp.where(sel, 0.0, NEG_INF).astype(bf16)
            return carry

        lax.fori_loop(0, n_kt, write_tile, 0)


def _dsa_select(qi, wi, kit, seq, top_k):
    tq, tk = min(256, seq), 512
    assert seq < TIE_NONE and seq % SEL_COUNT_COLS == 0
    idx_bits = max(1, int(math.ceil(math.log2(seq))))
    body = functools.partial(_dsa_select_body, tq=tq, tk=tk, top_k=top_k, idx_bits=idx_bits)
    return pl.pallas_call(
        body,
        grid=(seq // tq,),
        in_specs=[pl.BlockSpec((tq, qi.shape[1]), lambda i: (i, 0)),
                  pl.BlockSpec((tq, wi.shape[1]), lambda i: (i, 0)),
                  pl.BlockSpec(kit.shape, lambda i: (0, 0))],
        out_specs=pl.BlockSpec((tq, seq), lambda i: (i, 0)),
        out_shape=jax.ShapeDtypeStruct((seq, seq), bf16),
        scratch_shapes=[pltpu.VMEM((tq, seq), i32), pltpu.VMEM((tq, seq), i16), pltpu.VMEM((tq, seq), i16)],
        compiler_params=_cparams(("parallel",)),
        name="dsa_select",
    )(qi, wi, kit)


FLAG_FIRST, FLAG_MASK, FLAG_LAST = 1, 2, 4


def _flash_body(it_ref, jt_ref, ft_ref, q_ref, k_ref, v_ref, *rest, tq, tk, n_heads, dq, dv, v_group, has_bias, diff,
                lambda_init):
    rest = list(rest)
    bias_ref = rest.pop(0) if has_bias else None
    if diff:
        lq1_ref, lk1_ref, lq2_ref, lk2_ref, gsub_ref = rest[:5]
        rest = rest[5:]
    o_ref, m_scr, acc_scr = rest
    dvp = _v_pad(dv)
    step = pl.program_id(0)
    i = it_ref[step]
    j = jt_ref[step]
    flag = ft_ref[step]

    @pl.when((flag & FLAG_FIRST) != 0)
    def _():
        m_scr[...] = jnp.full(m_scr.shape, NEG_INF, f32)
        acc_scr[...] = jnp.zeros(acc_scr.shape, f32)

    def attend(bias):
        for h in range(n_heads):
            hv = h // v_group
            s = _dot_nt(q_ref[:, h * dq:(h + 1) * dq], k_ref[:, h * dq:(h + 1) * dq])
            if bias is not None:
                s = s + bias
            m_prev = m_scr[h]
            m_new = jnp.maximum(m_prev, jnp.max(s, axis=1, keepdims=True))
            alpha = jnp.exp2(m_prev - m_new)
            p = jnp.exp2(s - jnp.tile(m_new, (1, tk // LANES)))
            acc_scr[h] = (acc_scr[h] * jnp.tile(alpha, (1, dvp // LANES))
                          + _dot(p.astype(bf16), v_ref[:, hv * dvp:(hv + 1) * dvp]))
            m_scr[h] = m_new

    if has_bias:
        attend(bias_ref[...].astype(f32))
    else:
        @pl.when((flag & FLAG_MASK) == 0)
        def _():
            attend(None)

        @pl.when((flag & FLAG_MASK) != 0)
        def _():
            r = (i * tq + lax.broadcasted_iota(i32, (tq, tk), 0)) // CHUNK
            c = (j * tk + lax.broadcasted_iota(i32, (tq, tk), 1)) // CHUNK
            attend(jnp.where(c <= r, 0.0, NEG_INF))

    @pl.when((flag & FLAG_LAST) != 0)
    def _():
        if diff:
            lam = (jnp.exp(jnp.sum(lq1_ref[...] * lk1_ref[...], axis=1, keepdims=True))
                   - jnp.exp(jnp.sum(lq2_ref[...] * lk2_ref[...], axis=1, keepdims=True)) + lambda_init)
            for hc in range(n_heads // 2):
                a1, a2 = acc_scr[2 * hc], acc_scr[2 * hc + 1]
                o1 = a1[:, :dv] / a1[:, dv:dv + 1]
                o2 = a2[:, :dv] / a2[:, dv:dv + 1]
                o = _rms(o1 - lam * o2, gsub_ref[...]) * (1.0 - lambda_init)
                o_ref[:, hc * dv:(hc + 1) * dv] = o.astype(o_ref.dtype)
        else:
            for h in range(n_heads):
                a = acc_scr[h]
                o_ref[:, h * dv:(h + 1) * dv] = (a[:, :dv] / a[:, dv:dv + 1]).astype(o_ref.dtype)


FLASH_TILE_ELEMS = 1024 * 512
FLASH_STATS_BYTES = 12 * 1024 * 1024


def _v_pad(dv):
    return (dv // LANES + 1) * LANES


def _with_ones_column(w_v, n_heads, dv):
    k = w_v.shape[0]
    dvp = _v_pad(dv)
    w = jnp.pad(w_v.reshape(k, n_heads, dv), ((0, 0), (0, 0), (0, dvp - dv))).reshape(k, n_heads * dvp)
    one = jnp.zeros((1, n_heads, dvp), f32).at[:, :, dv].set(1.0).reshape(1, n_heads * dvp)
    return w, one


def _flash_tiles(seq, n_heads, dv):
    tq = 1024
    while tq > 128 and n_heads * tq * (LANES + _v_pad(dv)) * 4 > FLASH_STATS_BYTES:
        tq //= 2
    return min(tq, seq), min(FLASH_TILE_ELEMS // tq, seq)


def _flash(q, k, v, *, n_heads, dq, dv, v_group=1, bias=None, diff_params=None, lambda_init=0.0, name):
    seq = q.shape[0]
    tq, tk = _flash_tiles(seq, n_heads, dv)
    pairs = []
    for i in range(seq // tq):
        j_last = ((i + 1) * tq - 1) // tk
        for j in range(j_last + 1):
            needs_mask = (j + 1) * tk > i * tq + CHUNK
            pairs.append((i, j, (FLAG_FIRST if j == 0 else 0) | (FLAG_MASK if needs_mask else 0)
                          | (FLAG_LAST if j == j_last else 0)))
    it = jnp.asarray([p[0] for p in pairs], i32)
    jt = jnp.asarray([p[1] for p in pairs], i32)
    ft = jnp.asarray([p[2] for p in pairs], i32)
    n_out = (n_heads // v_group) * dv
    in_specs = [pl.BlockSpec((tq, q.shape[1]), lambda s, it, jt, ft: (it[s], 0)),
                pl.BlockSpec((tk, k.shape[1]), lambda s, it, jt, ft: (jt[s], 0)),
                pl.BlockSpec((tk, v.shape[1]), lambda s, it, jt, ft: (jt[s], 0))]
    args = [q, k, v]
    if bias is not None:
        in_specs.append(pl.BlockSpec((tq, tk), lambda s, it, jt, ft: (it[s], jt[s])))
        args.append(bias)
    if diff_params is not None:
        for a in diff_params:
            in_specs.append(pl.BlockSpec(a.shape, lambda s, it, jt, ft: (0, 0)))
            args.append(a)
    body = functools.partial(_flash_body, tq=tq, tk=tk, n_heads=n_heads, dq=dq, dv=dv, v_group=v_group,
                             has_bias=bias is not None, diff=diff_params is not None, lambda_init=lambda_init)
    return pl.pallas_call(
        body,
        grid_spec=pltpu.PrefetchScalarGridSpec(
            num_scalar_prefetch=3,
            grid=(len(pairs),),
            in_specs=in_specs,
            out_specs=pl.BlockSpec((tq, n_out), lambda s, it, jt, ft: (it[s], 0)),
            scratch_shapes=[pltpu.VMEM((n_heads, tq, LANES), f32), pltpu.VMEM((n_heads, tq, _v_pad(dv)), f32)]),
        out_shape=jax.ShapeDtypeStruct((seq, n_out), bf16),
        compiler_params=_cparams(("arbitrary",)),
        name=name,
    )(it, jt, ft, *args)


def _silu_mul(gate, up):
    return gate / (1.0 + jnp.exp(-gate)) * up


def _ffn_body(x_ref, g_ref, wg_ref, wu_ref, wd_ref, o_ref, xn_scr, acc_scr):
    f = pl.program_id(1)

    @pl.when(f == 0)
    def _():
        xn_scr[...] = _rms(x_ref[...], g_ref[...]).astype(bf16)
        acc_scr[...] = jnp.zeros(acc_scr.shape, f32)

    xn = xn_scr[...]
    act = _silu_mul(_dot(xn, wg_ref[...]), _dot(xn, wu_ref[...])).astype(bf16)
    acc_scr[...] += _dot(act, wd_ref[...])

    @pl.when(f == pl.num_programs(1) - 1)
    def _():
        o_ref[...] = x_ref[...] + acc_scr[...]


def _ffn(x, g, wg, wu, wd, *, tm, tf, name):
    seq = x.shape[0]
    dff = wg.shape[1]
    return pl.pallas_call(
        _ffn_body,
        grid=(seq // tm, dff // tf),
        in_specs=[pl.BlockSpec((tm, D_MODEL), lambda i, f: (i, 0)),
                  pl.BlockSpec((1, D_MODEL), lambda i, f: (0, 0)),
                  pl.BlockSpec((D_MODEL, tf), lambda i, f: (0, f)),
                  pl.BlockSpec((D_MODEL, tf), lambda i, f: (0, f)),
                  pl.BlockSpec((tf, D_MODEL), lambda i, f: (f, 0))],
        out_specs=pl.BlockSpec((tm, D_MODEL), lambda i, f: (i, 0)),
        out_shape=jax.ShapeDtypeStruct((seq, D_MODEL), f32),
        scratch_shapes=[pltpu.VMEM((tm, D_MODEL), bf16), pltpu.VMEM((tm, D_MODEL), f32)],
        compiler_params=_cparams(("parallel", "arbitrary")),
        name=name,
    )(x, g, wg, wu, wd)


MOE_TILE = 512
PAIR_FIRST, PAIR_VALID, PAIR_LAST = 1, 2, 4


def _route(x, g, w_router):
    seq = x.shape[0]
    tm = min(MOE_TILE, seq)
    nb = seq // tm
    w_r = _pad_cols(w_router, LANES)
    w_hi = w_r.astype(bf16)
    w_lo = (w_r - w_hi.astype(f32)).astype(bf16)
    row = lambda n: pl.BlockSpec((tm, n), lambda i: (i, 0))
    const = lambda a: pl.BlockSpec(a.shape, lambda i: (0, 0))
    return pl.pallas_call(
        _router_body,
        grid=(nb,),
        in_specs=[row(D_MODEL), const(g), const(w_hi), const(w_lo)],
        out_specs=[row(D_MODEL), row(LANES), row(LANES), pl.BlockSpec((1, 8, LANES), lambda i: (i, 0, 0))],
        out_shape=[jax.ShapeDtypeStruct((seq, D_MODEL), bf16), jax.ShapeDtypeStruct((seq, LANES), f32),
                   jax.ShapeDtypeStruct((seq, LANES), f32), jax.ShapeDtypeStruct((nb, 8, LANES), f32)],
        scratch_shapes=[pltpu.VMEM((8, LANES), f32)],
        compiler_params=_cparams(("arbitrary",)),
        name="router",
    )(x, g, w_hi, w_lo)


def _moe_schedule(counts_after, seq):
    tm = min(MOE_TILE, seq)
    nb = seq // tm
    n_tiles = 2 * nb + N_EXP
    kmax = nb + 1
    max_pairs = n_tiles + N_EXP * nb
    bounds = jnp.concatenate([jnp.zeros((1, N_EXP), i32), counts_after[:, 0, :N_EXP].astype(i32)], 0)
    cnt = bounds[-1]
    ntile = (cnt + tm - 1) // tm
    tile_end = jnp.cumsum(ntile)
    tile_start = tile_end - ntile
    n_valid = tile_end[-1]
    p_ids = jnp.minimum(jnp.arange(n_tiles, dtype=i32), n_valid - 1)
    tile_expert = jnp.minimum(jnp.sum(p_ids[:, None] >= tile_end[None, :], axis=1), N_EXP - 1).astype(i32)
    tile_valid = (jnp.arange(n_tiles, dtype=i32) < n_valid).astype(i32)

    lo = jnp.transpose(bounds[:-1])[:, None, :]
    hi = jnp.transpose(bounds[1:])[:, None, :]
    k0 = (jnp.arange(kmax, dtype=i32) * tm)[None, :, None]
    meet = jnp.maximum(k0, lo) < jnp.minimum(k0 + tm, hi)
    n_pairs = jnp.sum(meet)
    s_ids = jnp.arange(max_pairs, dtype=i32)
    s_eff = jnp.minimum(s_ids, n_pairs - 1)

    def pair_list(flat, decode, group_of):
        idx = jnp.nonzero(flat, size=max_pairs, fill_value=0)[0].astype(i32)[s_eff]
        e, k, b = decode(idx)
        p = tile_start[e] + k
        grp = group_of(p, b)
        valid = s_ids < n_pairs
        first = jnp.concatenate([jnp.ones((1,), bool), grp[1:] != grp[:-1]])
        last = jnp.concatenate([grp[1:] != grp[:-1], jnp.ones((1,), bool)]) | (s_ids == n_pairs - 1)
        flags = jnp.where(valid, PAIR_VALID + PAIR_FIRST * first + PAIR_LAST * last, 0).astype(i32)
        return p.astype(i32), b.astype(i32), e.astype(i32), k.astype(i32), flags

    tile_major = pair_list(meet.reshape(-1),
                           lambda i: (i // (kmax * nb), (i // nb) % kmax, i % nb), lambda p, b: p)
    block_major = pair_list(jnp.transpose(meet, (2, 0, 1)).reshape(-1),
                            lambda i: ((i // kmax) % N_EXP, i % kmax, i // (N_EXP * kmax)), lambda p, b: b)
    return n_tiles, tile_expert, tile_valid, tile_major, block_major


def _moe_gather_body(pt, pb, pe, pk, pf, xn_ref, rank_t_ref, o_ref):
    s = pl.program_id(0)
    flag = pf[s]
    tmg, tb = o_ref.shape[0], xn_ref.shape[0]

    @pl.when((flag & PAIR_FIRST) != 0)
    def _():
        o_ref[...] = jnp.zeros(o_ref.shape, o_ref.dtype)

    @pl.when((flag & PAIR_VALID) != 0)
    def _():
        r = rank_t_ref[pl.ds(pe[s], 1), :] - (pk[s] * tmg).astype(f32)
        rows = lax.broadcasted_iota(i32, (tmg, tb), 0).astype(f32)
        onehot = jnp.where(rows == r, 1.0, 0.0).astype(bf16)
        o_ref[...] = o_ref[...] + _dot(onehot, xn_ref[...]).astype(o_ref.dtype)


def _moe_ffn_body(te, tv, x_ref, wg_ref, wu_ref, wd_ref, y_ref, acc_scr):
    p = pl.program_id(0)
    f = pl.program_id(1)
    last_f = f == pl.num_programs(1) - 1

    @pl.when(tv[p] != 0)
    def _():
        @pl.when(f == 0)
        def _():
            acc_scr[...] = jnp.zeros(acc_scr.shape, f32)

        x = x_ref[...]
        act = _silu_mul(_dot(x, wg_ref[0]), _dot(x, wu_ref[0])).astype(bf16)
        acc_scr[...] += _dot(act, wd_ref[0])

        @pl.when(last_f)
        def _():
            y_ref[...] = acc_scr[...].astype(y_ref.dtype)

    @pl.when(jnp.logical_and(tv[p] == 0, last_f))
    def _():
        y_ref[...] = jnp.zeros(y_ref.shape, y_ref.dtype)


def _moe_combine_body(ct, cb, ce, ck, cf, h_ref, y_ref, rank_ref, comb_ref, fg_ref, o_ref, acc_scr):
    s = pl.program_id(0)
    flag = cf[s]
    tb, tmg = h_ref.shape[0], y_ref.shape[0]

    @pl.when((flag & PAIR_FIRST) != 0)
    def _():
        acc_scr[...] = h_ref[...]

    @pl.when((flag & PAIR_VALID) != 0)
    def _():
        lane = lax.broadcasted_iota(i32, rank_ref.shape, 1)
        mine = lane == ce[s]
        r = jnp.sum(jnp.where(mine, rank_ref[...], 0.0), axis=1, keepdims=True) - (ck[s] * tmg).astype(f32)
        gate = jnp.sum(jnp.where(mine, comb_ref[...], 0.0), axis=1, keepdims=True)
        cols = lax.broadcasted_iota(i32, (tb, tmg), 1).astype(f32)
        onehot = jnp.where(cols == r, 1.0, 0.0).astype(bf16)
        acc_scr[...] += gate * _dot(onehot, y_ref[...])

    @pl.when((flag & PAIR_LAST) != 0)
    def _():
        o_ref[...] = _rms(acc_scr[...], fg_ref[...])


def _moe(h, g_ffn, w_router, w_gate_e, w_up_e, w_down_e, final_g):
    seq = h.shape[0]
    tm = min(MOE_TILE, seq)
    xn, comb, rank, counts_after = _route(h, g_ffn, w_router)
    n_tiles, tile_expert, tile_valid, tile_major, block_major = _moe_schedule(counts_after, seq)
    n_pairs = tile_major[0].shape[0]
    rank_t = jnp.transpose(rank[:, :8])

    x_sorted = pl.pallas_call(
        _moe_gather_body,
        grid_spec=pltpu.PrefetchScalarGridSpec(
            num_scalar_prefetch=5, grid=(n_pairs,),
            in_specs=[pl.BlockSpec((tm, D_MODEL), lambda s, pt, pb, pe, pk, pf: (pb[s], 0)),
                      pl.BlockSpec((8, tm), lambda s, pt, pb, pe, pk, pf: (0, pb[s]))],
            out_specs=pl.BlockSpec((tm, D_MODEL), lambda s, pt, pb, pe, pk, pf: (pt[s], 0))),
        out_shape=jax.ShapeDtypeStruct((n_tiles * tm, D_MODEL), bf16),
        compiler_params=_cparams(("arbitrary",)),
        name="moe_gather",
    )(*tile_major, xn, rank_t)

    tf = D_FF_E // 4
    n_f = D_FF_E // tf
    f_eff = lambda f, v: f * v + (n_f - 1) * (1 - v)
    y_sorted = pl.pallas_call(
        _moe_ffn_body,
        grid_spec=pltpu.PrefetchScalarGridSpec(
            num_scalar_prefetch=2, grid=(n_tiles, n_f),
            in_specs=[pl.BlockSpec((tm, D_MODEL), lambda p, f, te, tv: (p, 0)),
                      pl.BlockSpec((1, D_MODEL, tf), lambda p, f, te, tv: (te[p], 0, f_eff(f, tv[p]))),
                      pl.BlockSpec((1, D_MODEL, tf), lambda p, f, te, tv: (te[p], 0, f_eff(f, tv[p]))),
                      pl.BlockSpec((1, tf, D_MODEL), lambda p, f, te, tv: (te[p], f_eff(f, tv[p]), 0))],
            out_specs=pl.BlockSpec((tm, D_MODEL), lambda p, f, te, tv: (p, 0)),
            scratch_shapes=[pltpu.VMEM((tm, D_MODEL), f32)]),
        out_shape=jax.ShapeDtypeStruct((n_tiles * tm, D_MODEL), bf16),
        compiler_params=_cparams(("arbitrary", "arbitrary")),
        name="moe_ffn",
    )(tile_expert, tile_valid, x_sorted, w_gate_e.astype(bf16), w_up_e.astype(bf16), w_down_e.astype(bf16))

    return pl.pallas_call(
        _moe_combine_body,
        grid_spec=pltpu.PrefetchScalarGridSpec(
            num_scalar_prefetch=5, grid=(n_pairs,),
            in_specs=[pl.BlockSpec((tm, D_MODEL), lambda s, ct, cb, ce, ck, cf: (cb[s], 0)),
                      pl.BlockSpec((tm, D_MODEL), lambda s, ct, cb, ce, ck, cf: (ct[s], 0)),
                      pl.BlockSpec((tm, LANES), lambda s, ct, cb, ce, ck, cf: (cb[s], 0)),
                      pl.BlockSpec((tm, LANES), lambda s, ct, cb, ce, ck, cf: (cb[s], 0)),
                      pl.BlockSpec((1, D_MODEL), lambda s, ct, cb, ce, ck, cf: (0, 0))],
            out_specs=pl.BlockSpec((tm, D_MODEL), lambda s, ct, cb, ce, ck, cf: (cb[s], 0)),
            scratch_shapes=[pltpu.VMEM((tm, D_MODEL), f32)]),
        out_shape=jax.ShapeDtypeStruct((seq, D_MODEL), f32),
        compiler_params=_cparams(("arbitrary",)),
        name="moe_combine",
    )(*block_major, h, y_sorted, rank, comb, final_g)


def _even_layer(h, norm_mix, w_in, g_q_lat, w_uq, g_kv_lat, w_ukv, g_idx_k, w_out, norm_ffn, w_gate, w_up, w_down):
    seq = h.shape[0]
    sizes = (Q_LORA, KV_LORA, ROPE_A, H_B * DH_B, H_B * DH_B, H_B * DH_B, H_IDX * D_IDX, D_IDX, H_IDX)
    offs = np.cumsum((0,) + sizes)
    w_cq, w_ckv, w_kr, w_qb, w_kb, w_vb, w_qi, w_ki, w_wi = [w_in[:, offs[n]:offs[n + 1]] for n in range(9)]

    scale_a = (NOPE_A + ROPE_A) ** -0.5 * LOG2E
    wq3 = (w_uq * scale_a).reshape(Q_LORA, H_A, NOPE_A + ROPE_A)
    wq = jnp.pad(wq3, ((0, 0), (0, 0), (0, HEAD_PAD_A - NOPE_A - ROPE_A))).reshape(Q_LORA, H_A * HEAD_PAD_A)
    wq_sw = _swap_cols(wq, HEAD_PAD_A, NOPE_A, ROPE_A)
    wkv3 = w_ukv.reshape(KV_LORA, H_A, NOPE_A + V_A)
    wk = jnp.pad(wkv3[:, :, :NOPE_A], ((0, 0), (0, 0), (0, HEAD_PAD_A - NOPE_A))).reshape(KV_LORA, H_A * HEAD_PAD_A)
    wv, one_a = _with_ones_column(wkv3[:, :, NOPE_A:].reshape(KV_LORA, H_A * V_A), H_A, V_A)
    w_vb, one_b = _with_ones_column(w_vb, H_B, DH_B)
    place = jnp.zeros((ROPE_A, H_A, HEAD_PAD_A), f32)
    place = place.at[:, :, NOPE_A:NOPE_A + ROPE_A].set(jnp.eye(ROPE_A, dtype=f32)[:, None, :])
    place = place.reshape(ROPE_A, H_A * HEAD_PAD_A)

    w_qb = w_qb * (DH_B ** -0.5 * LOG2E)
    w_qi = w_qi * D_IDX ** -0.5
    w_small = jnp.concatenate([w_kr, _swap_cols(w_kr, ROPE_A, 0, ROPE_A), w_ki, _swap_cols(w_ki, D_IDX, 0, ROT_IDX)], 1)
    w_wi_p = _pad_cols(w_wi * H_IDX ** -0.5, LANES)
    g_idx = g_idx_k.reshape(1, D_IDX)
    g_idx_sw = jnp.concatenate([g_idx[:, ROT_IDX // 2:ROT_IDX], g_idx[:, :ROT_IDX // 2], g_idx[:, ROT_IDX:]], 1)

    ca, sa = _rope_tables(seq, ROPE_A, HEAD_PAD_A, NOPE_A, LANES)
    cb, sb = _rope_tables(seq, ROT_B, DH_B, 0, LANES)
    ci, si = _rope_tables(seq, ROT_IDX, D_IDX, 0, LANES)
    ckr, skr = _rope_tables(seq, ROPE_A, ROPE_A, 0, ROPE_A)

    consts = [norm_mix.reshape(1, -1), g_q_lat.reshape(1, -1), g_kv_lat.reshape(1, -1), g_idx, g_idx_sw,
              one_a, one_b]
    weights = [jnp.concatenate([w_cq, w_ckv], 1), wq, wq_sw, wk, wv, place,
               w_qb, _swap_cols(w_qb, DH_B, 0, ROT_B), w_kb, _swap_cols(w_kb, DH_B, 0, ROT_B), w_vb,
               w_qi, _swap_cols(w_qi, D_IDX, 0, ROT_IDX), w_small, w_wi_p]
    weights = [w.astype(bf16) for w in weights]
    sds = lambda n, dt: jax.ShapeDtypeStruct((seq, n), dt)
    outs = [sds(H_A * HEAD_PAD_A, bf16), sds(H_A * HEAD_PAD_A, bf16), sds(H_A * _v_pad(V_A), bf16),
            sds(H_B * DH_B, bf16), sds(H_B * DH_B, bf16), sds(H_B * _v_pad(DH_B), bf16),
            sds(H_IDX * D_IDX, bf16), sds(D_IDX, bf16), sds(LANES, f32)]
    qa, ka, va, qb, kb, vb, qi, ki, wi = _rows_call(
        _even_proj_body, seq, 256, [h, ca, sa, cb, sb, ci, si, ckr, skr], consts + weights, outs, "even_proj")

    o_a = _flash(qa, ka, va, n_heads=H_A, dq=HEAD_PAD_A, dv=V_A, name="mla_attn")
    top_k = min(TOPK_MAX, seq // 4)
    bias = _dsa_select(qi, wi, ki.T, seq, top_k)
    o_b = _flash(qb, kb, vb, n_heads=H_B, dq=DH_B, dv=DH_B, bias=bias, name="dsa_attn")

    w_out = w_out.astype(bf16)
    n_a = H_A * V_A
    (h,) = _rows_call(_out_proj2_body, seq, 512, [h, o_a, o_b], [w_out[:n_a], w_out[n_a:]],
                      [jax.ShapeDtypeStruct((seq, D_MODEL), f32)], "even_out_proj")
    return _ffn(h, norm_ffn.reshape(1, -1), w_gate.astype(bf16), w_up.astype(bf16), w_down.astype(bf16),
                tm=512, tf=D_FF // 2, name="dense_ffn")


def _odd_layer(h, layer, norm_mix, w_qkv, lq1, lk1, lq2, lk2, g_sub, w_out, norm_ffn, w_router, w_gate_e, w_up_e,
               w_down_e, final_norm):
    seq = h.shape[0]
    lambda_init = 0.8 - 0.6 * math.exp(-0.3 * layer)
    n = H_C * 2 * DH_C
    w_q = w_qkv[:, :n] * (DH_C ** -0.5 * LOG2E)
    w_k = w_qkv[:, n:2 * n]
    w_v, one_v = _with_ones_column(w_qkv[:, 2 * n:], H_C, 2 * DH_C)
    cb, sb = _rope_tables(seq, ROT_C, DH_C, 0, LANES)
    weights = [w_q, _swap_cols(w_q, DH_C, 0, ROT_C), w_k, _swap_cols(w_k, DH_C, 0, ROT_C), w_v]
    weights = [w.astype(bf16) for w in weights]
    sds = jax.ShapeDtypeStruct((seq, n), bf16)
    sds_v = jax.ShapeDtypeStruct((seq, w_v.shape[1]), bf16)
    q, k, v = _rows_call(_odd_proj_body, seq, 512, [h, cb, sb], [norm_mix.reshape(1, -1), one_v] + weights,
                         [sds, sds, sds_v], "odd_proj")
    diff_params = [lq1.reshape(1, -1), lk1.reshape(1, -1), lq2.reshape(1, -1), lk2.reshape(1, -1),
                   g_sub.reshape(1, -1)]
    o = _flash(q, k, v, n_heads=2 * H_C, dq=DH_C, dv=2 * DH_C, v_group=2, diff_params=diff_params,
               lambda_init=lambda_init, name="diff_attn")
    (h,) = _rows_call(_out_proj1_body, seq, 512, [h, o], [w_out.astype(bf16)],
                      [jax.ShapeDtypeStruct((seq, D_MODEL), f32)], "odd_out_proj")
    return _moe(h, norm_ffn.reshape(1, -1), w_router, w_gate_e, w_up_e, w_down_e, final_norm.reshape(1, -1))


def kernel(x, ev_norm_mix, ev_w_in, ev_g_q_lat, ev_w_uq, ev_g_kv_lat, ev_w_ukv, ev_g_idx_k, ev_w_out, ev_norm_ffn, ev_w_gate, ev_w_up, ev_w_down, od_norm_mix, od_w_qkv, od_lambda_q1, od_lambda_k1, od_lambda_q2, od_lambda_k2, od_g_sub, od_w_out, od_norm_ffn, od_w_router, od_w_gate_e, od_w_up_e, od_w_down_e, final_norm):
    batch, seq, _ = x.shape
    assert batch == 1 and ev_w_in.shape[0] == 1 and od_w_qkv.shape[0] == 1
    h = x[0]
    h = _even_layer(h, ev_norm_mix[0], ev_w_in[0], ev_g_q_lat[0], ev_w_uq[0], ev_g_kv_lat[0], ev_w_ukv[0],
                    ev_g_idx_k[0], ev_w_out[0], ev_norm_ffn[0], ev_w_gate[0], ev_w_up[0], ev_w_down[0])
    h = _odd_layer(h, 1, od_norm_mix[0], od_w_qkv[0], od_lambda_q1[0], od_lambda_k1[0], od_lambda_q2[0],
                   od_lambda_k2[0], od_g_sub[0], od_w_out[0], od_norm_ffn[0], od_w_router[0], od_w_gate_e[0],
                   od_w_up_e[0], od_w_down_e[0], final_norm)
    return h[None]
```

```python
import functools
import math

import numpy as np
import jax
import jax.numpy as jnp
from jax import lax
from jax.experimental import pallas as pl
from jax.experimental.pallas import tpu as pltpu

f32 = jnp.float32
bf16 = jnp.bfloat16
i32 = jnp.int32
i16 = jnp.int16

D_MODEL = 1024
CHUNK = 64
ROPE_THETA = 500000.0
NORM_EPS = 1e-6
NEG_INF = -1e30
LOG2E = math.log2(math.e)

H_A, Q_LORA, KV_LORA, NOPE_A, ROPE_A, V_A = 8, 256, 128, 64, 32, 64
H_B, DH_B, ROT_B = 8, 64, 16
H_IDX, D_IDX, ROT_IDX = 8, 32, 8
TOPK_MAX = 256
H_C, DH_C, ROT_C = 8, 64, 16
D_FF, N_EXP, D_FF_E = 2816, 8, 3584

LANES = 128
HEAD_PAD_A = 128

_NEG_BITS = int(np.float32(NEG_INF).view(np.int32))
NEG_KEY = _NEG_BITS ^ 0x7FFFFFFF
INT_MIN = -(2 ** 31)

VMEM_LIMIT = 56 * 1024 * 1024


def _cparams(sem):
    return pltpu.CompilerParams(dimension_semantics=sem, vmem_limit_bytes=VMEM_LIMIT)


def _rms(x, g):
    var = jnp.mean(x * x, axis=-1, keepdims=True)
    return x * lax.rsqrt(var + NORM_EPS) * g


def _dot(a, b):
    return jnp.dot(a, b, preferred_element_type=f32)


def _dot_nt(a, b):
    return lax.dot_general(a, b, (((1,), (1,)), ((), ())), preferred_element_type=f32)


def _rope_tables(seq, rot_dim, head_width, offset, width):
    pos = jnp.arange(seq, dtype=f32)
    inv_freq = ROPE_THETA ** (-jnp.arange(0, rot_dim, 2, dtype=f32) / rot_dim)
    ang = pos[:, None] * inv_freq[None, :]
    cos, sin = jnp.cos(ang), jnp.sin(ang)
    c = jnp.ones((seq, head_width), f32).at[:, offset:offset + rot_dim].set(jnp.concatenate([cos, cos], -1))
    s = jnp.zeros((seq, head_width), f32).at[:, offset:offset + rot_dim].set(jnp.concatenate([-sin, sin], -1))
    reps = width // head_width
    return jnp.tile(c, (1, reps)), jnp.tile(s, (1, reps))


def _swap_cols(w, head_width, offset, rot_dim):
    k, n = w.shape
    half = rot_dim // 2
    w3 = w.reshape(k, n // head_width, head_width)
    out = jnp.zeros_like(w3)
    out = out.at[:, :, offset:offset + half].set(w3[:, :, offset + half:offset + rot_dim])
    out = out.at[:, :, offset + half:offset + rot_dim].set(w3[:, :, offset:offset + half])
    return out.reshape(k, n)


def _pad_cols(w, width):
    return jnp.pad(w, ((0, 0), (0, width - w.shape[1])))


def _rows_call(body, seq, tm, row_ins, const_ins, out_sds, name):
    def rspec(a):
        return pl.BlockSpec((tm, a.shape[1]), lambda i: (i, 0))

    def cspec(a):
        nd = a.ndim
        return pl.BlockSpec(a.shape, lambda i: (0,) * nd)

    return pl.pallas_call(
        body,
        grid=(seq // tm,),
        in_specs=[rspec(a) for a in row_ins] + [cspec(a) for a in const_ins],
        out_specs=[pl.BlockSpec((tm, o.shape[1]), lambda i: (i, 0)) for o in out_sds],
        out_shape=out_sds,
        compiler_params=_cparams(("parallel",)),
        name=name,
    )(*row_ins, *const_ins)


def _even_proj_body(x_ref, ca_ref, sa_ref, cb_ref, sb_ref, ci_ref, si_ref, ckr_ref, skr_ref,
                    g_ref, gq_ref, gkv_ref, gi_ref, gisw_ref, onea_ref, oneb_ref,
                    wlat_ref, wq_ref, wqsw_ref, wk_ref, wv_ref, place_ref,
                    wqb_ref, wqbsw_ref, wkb_ref, wkbsw_ref, wvb_ref, wqi_ref, wqisw_ref, wsm_ref, wwi_ref,
                    qa_ref, ka_ref, va_ref, qb_ref, kb_ref, vb_ref, qi_ref, ki_ref, wi_ref):
    xn = _rms(x_ref[...], g_ref[...]).astype(bf16)
    lat = _dot(xn, wlat_ref[...])
    cqn = _rms(lat[:, :Q_LORA], gq_ref[...]).astype(bf16)
    ckvn = _rms(lat[:, Q_LORA:], gkv_ref[...]).astype(bf16)
    reps_a = qa_ref.shape[1] // LANES
    ca = jnp.tile(ca_ref[...], (1, reps_a))
    sa = jnp.tile(sa_ref[...], (1, reps_a))
    qa_ref[...] = (_dot(cqn, wq_ref[...]) * ca + _dot(cqn, wqsw_ref[...]) * sa).astype(bf16)
    small = _dot(xn, wsm_ref[...])
    kr, kr_sw = small[:, 0:ROPE_A], small[:, ROPE_A:2 * ROPE_A]
    kpe = (kr * ckr_ref[...] + kr_sw * skr_ref[...]).astype(bf16)
    ka_ref[...] = (_dot(ckvn, wk_ref[...]) + _dot(kpe, place_ref[...])).astype(bf16)
    va_ref[...] = (_dot(ckvn, wv_ref[...]) + onea_ref[...]).astype(bf16)
    reps_b = qb_ref.shape[1] // LANES
    cb = jnp.tile(cb_ref[...], (1, reps_b))
    sb = jnp.tile(sb_ref[...], (1, reps_b))
    qb_ref[...] = (_dot(xn, wqb_ref[...]) * cb + _dot(xn, wqbsw_ref[...]) * sb).astype(bf16)
    kb_ref[...] = (_dot(xn, wkb_ref[...]) * cb + _dot(xn, wkbsw_ref[...]) * sb).astype(bf16)
    vb_ref[...] = (_dot(xn, wvb_ref[...]) + oneb_ref[...]).astype(bf16)
    reps_i = qi_ref.shape[1] // LANES
    ci = jnp.tile(ci_ref[...], (1, reps_i))
    si = jnp.tile(si_ref[...], (1, reps_i))
    qi_ref[...] = (_dot(xn, wqi_ref[...]) * ci + _dot(xn, wqisw_ref[...]) * si).astype(bf16)
    ki, ki_sw = small[:, 2 * ROPE_A:2 * ROPE_A + D_IDX], small[:, 2 * ROPE_A + D_IDX:2 * ROPE_A + 2 * D_IDX]
    r = lax.rsqrt(jnp.mean(ki * ki, axis=-1, keepdims=True) + NORM_EPS)
    ci32, si32 = ci_ref[:, 0:D_IDX], si_ref[:, 0:D_IDX]
    ki_ref[...] = (ki * r * gi_ref[...] * ci32 + ki_sw * r * gisw_ref[...] * si32).astype(bf16)
    wi_ref[...] = _dot(xn, wwi_ref[...])


def _odd_proj_body(x_ref, cb_ref, sb_ref, g_ref, one_ref, wq_ref, wqsw_ref, wk_ref, wksw_ref, wv_ref,
                   q_ref, k_ref, v_ref):
    xn = _rms(x_ref[...], g_ref[...]).astype(bf16)
    reps = q_ref.shape[1] // LANES
    cb = jnp.tile(cb_ref[...], (1, reps))
    sb = jnp.tile(sb_ref[...], (1, reps))
    q_ref[...] = (_dot(xn, wq_ref[...]) * cb + _dot(xn, wqsw_ref[...]) * sb).astype(bf16)
    k_ref[...] = (_dot(xn, wk_ref[...]) * cb + _dot(xn, wksw_ref[...]) * sb).astype(bf16)
    v_ref[...] = (_dot(xn, wv_ref[...]) + one_ref[...]).astype(bf16)


def _out_proj2_body(x_ref, a1_ref, a2_ref, w1_ref, w2_ref, o_ref):
    o_ref[...] = x_ref[...] + _dot(a1_ref[...], w1_ref[...]) + _dot(a2_ref[...], w2_ref[...])


def _out_proj1_body(x_ref, a_ref, w_ref, o_ref):
    o_ref[...] = x_ref[...] + _dot(a_ref[...], w_ref[...])


def _router_body(x_ref, g_ref, whi_ref, wlo_ref, xn_ref, comb_ref, rank_ref, cnt_ref, carry_scr):
    @pl.when(pl.program_id(0) == 0)
    def _():
        carry_scr[...] = jnp.zeros(carry_scr.shape, f32)

    xn = _rms(x_ref[...], g_ref[...])
    xn_ref[...] = xn.astype(bf16)
    hi = xn.astype(bf16)
    lo = (xn - hi.astype(f32)).astype(bf16)
    logits = _dot(hi, whi_ref[...]) + _dot(lo, whi_ref[...]) + _dot(hi, wlo_ref[...])
    lane = lax.broadcasted_iota(i32, logits.shape, 1).astype(f32)
    lg = jnp.where(lane < N_EXP, logits, -jnp.inf)
    m1 = jnp.max(lg, axis=1, keepdims=True)
    i1 = jnp.min(jnp.where(lg == m1, lane, float(LANES)), axis=1, keepdims=True)
    lg2 = jnp.where(lane == i1, -jnp.inf, lg)
    m2 = jnp.max(lg2, axis=1, keepdims=True)
    i2 = jnp.min(jnp.where(lg2 == m2, lane, float(LANES)), axis=1, keepdims=True)
    e2 = jnp.exp(m2 - m1)
    den = 1.0 + e2
    comb_ref[...] = jnp.where(lane == i1, 1.0 / den, 0.0) + jnp.where(lane == i2, e2 / den, 0.0)
    routed = jnp.logical_or(lane == i1, lane == i2)
    onehot = jnp.where(routed, 1.0, 0.0)
    tm = onehot.shape[0]
    earlier = lax.broadcasted_iota(i32, (tm, tm), 1) < lax.broadcasted_iota(i32, (tm, tm), 0)
    before = _dot(jnp.where(earlier, 1.0, 0.0).astype(bf16), onehot.astype(bf16))
    carry = carry_scr[0:1, :]
    rank_ref[...] = jnp.where(routed, before + carry, -1.0)
    carry = carry + jnp.sum(onehot, axis=0, keepdims=True)
    carry_scr[...] = jnp.broadcast_to(carry, carry_scr.shape)
    cnt_ref[0] = jnp.broadcast_to(carry, cnt_ref.shape[1:])


SEL_COUNT_ROWS, SEL_COUNT_COLS = 128, 1024
TIE_NONE = 2 ** 15 - 1


def _dsa_select_body(qi_ref, wi_ref, kit_ref, out_ref, keys_scr, hi16_scr, low16_scr, *, tq, tk, top_k, idx_bits):
    q0 = pl.program_id(0) * tq
    n_ktc = (q0 + tq + SEL_COUNT_COLS - 1) // SEL_COUNT_COLS
    n_kt = n_ktc * (SEL_COUNT_COLS // tk)
    qh = [qi_ref[:, h * D_IDX:(h + 1) * D_IDX] for h in range(H_IDX)]
    w = wi_ref[...]
    wb = [jnp.broadcast_to(w[:, h:h + 1], (tq, tk)) for h in range(H_IDX)]
    row = q0 + lax.broadcasted_iota(i32, (tq, 1), 0)
    row_lim = (row // CHUNK + 1) * CHUNK

    def cols_of(kt):
        c0 = pl.multiple_of(kt * tk, tk)
        return c0, c0 + lax.broadcasted_iota(i32, (tq, tk), 1)

    def score_tile(kt, carry):
        c0, col = cols_of(kt)
        kt_tile = kit_ref[:, pl.ds(c0, tk)]
        acc = jnp.zeros((tq, tk), f32)
        for h in range(H_IDX):
            acc = acc + jnp.maximum(_dot(qh[h], kt_tile), 0.0) * wb[h]
        sc = jnp.where(col < row_lim, acc, NEG_INF)
        bits = lax.bitcast_convert_type(sc, i32)
        key = jnp.where(bits < 0, bits ^ 0x7FFFFFFF, bits)
        keys_scr[:, pl.ds(c0, tk)] = key
        hi16_scr[:, pl.ds(c0, tk)] = lax.shift_right_arithmetic(key, 16).astype(i16)
        return carry

    lax.fori_loop(0, n_kt, score_tile, 0)

    ones_mat = jnp.ones((LANES, LANES), bf16)

    groups16 = [slice(g * SEL_COUNT_ROWS, (g + 1) * SEL_COUNT_ROWS) for g in range(tq // SEL_COUNT_ROWS)]
    rows16 = SEL_COUNT_ROWS

    def count16(scr, cand_rep, strict, trips=None):
        parts = []
        for g, rows in enumerate(groups16):
            cand = cand_rep[rows].astype(i16)

            def body(kt, acc, rows=rows, cand=cand):
                c0 = pl.multiple_of(kt * SEL_COUNT_COLS, SEL_COUNT_COLS)
                ks = scr[rows, pl.ds(c0, SEL_COUNT_COLS)]
                for u in range(SEL_COUNT_COLS // LANES):
                    blk = ks[:, u * LANES:(u + 1) * LANES]
                    hit = blk > cand if strict else blk >= cand
                    acc = acc + jnp.where(hit, jnp.int16(1), jnp.int16(0))
                return acc

            n = n_ktc if trips is None else trips[g]
            parts.append(lax.fori_loop(0, n, body, jnp.zeros((rows16, LANES), i16)))
        acc = jnp.concatenate(parts, axis=0)
        return _dot(acc.astype(f32).astype(bf16), ones_mat)

    def thr_step_hi(it, carry):
        u, c = carry
        cand_u = u | lax.shift_left(jnp.int32(1), 31 - it)
        cnt = count16(hi16_scr, lax.shift_right_arithmetic(cand_u ^ INT_MIN, 16), False)
        keep = cnt >= top_k
        return jnp.where(keep, cand_u, u), jnp.where(keep, cnt, c)

    n_cols = jnp.full((tq, LANES), n_ktc * SEL_COUNT_COLS, i32).astype(f32)
    u_hi, c_hi = lax.fori_loop(0, 16, thr_step_hi, (jnp.zeros((tq, LANES), i32), n_cols))
    thr_hi = lax.shift_right_arithmetic(u_hi ^ INT_MIN, 16)
    n_above = count16(hi16_scr, thr_hi, True)

    def low_tile(kt, carry):
        c0 = pl.multiple_of(kt * tk, tk)
        low = ((keys_scr[:, pl.ds(c0, tk)] & 0xFFFF) - 32768).astype(i16)
        same = hi16_scr[:, pl.ds(c0, tk)] == jnp.tile(thr_hi, (1, tk // LANES)).astype(i16)
        low16_scr[:, pl.ds(c0, tk)] = jnp.where(same, low, jnp.int16(-32768))
        return carry

    lax.fori_loop(0, n_kt, low_tile, 0)

    def thr_step_lo(it, carry):
        u, c = carry
        cand_u = u | lax.shift_left(jnp.int32(1), 31 - it)
        cnt = n_above + count16(low16_scr, (cand_u & 0xFFFF) - 32768, False)
        keep = cnt >= top_k
        return jnp.where(keep, cand_u, u), jnp.where(keep, cnt, c)

    u_all, c_all = lax.fori_loop(16, 32, thr_step_lo, (u_hi, c_hi))
    thr_rep = u_all ^ INT_MIN
    n_ge = c_all[:, 0:1].astype(i32)
    thr = thr_rep[:, 0:1]
    thr_vis = jnp.maximum(thr, NEG_KEY + 1)
    excess = jnp.logical_and(n_ge > top_k, thr > NEG_KEY)
    any_excess = jnp.max(excess.astype(f32)) > 0.0
    out_ref[...] = jnp.full(out_ref.shape, NEG_INF, bf16)

    @pl.when(jnp.logical_not(any_excess))
    def _():
        low_thr = (jnp.maximum(thr_rep, NEG_KEY + 1) & 0xFFFF) - 32768
        whole_bucket = low_thr == -32768
        hx = jnp.where(whole_bucket, thr_hi - 1, thr_hi).astype(i16)
        lx = jnp.where(whole_bucket, TIE_NONE, low_thr - 1).astype(i16)
        hx_t = jnp.tile(hx, (1, tk // LANES))
        lx_t = jnp.tile(lx, (1, tk // LANES))
        zero_b, neg_b = jnp.zeros((), bf16), jnp.asarray(NEG_INF, bf16)

        def write_tile(kt, carry):
            c0 = pl.multiple_of(kt * tk, tk)
            sel = jnp.logical_or(hi16_scr[:, pl.ds(c0, tk)] > hx_t, low16_scr[:, pl.ds(c0, tk)] > lx_t)
            out_ref[:, pl.ds(c0, tk)] = jnp.where(sel, zero_b, neg_b)
            return carry

        lax.fori_loop(0, n_kt, write_tile, 0)

    @pl.when(any_excess)
    def _():
        excess_f = excess.astype(f32)
        trips = [jnp.where(jnp.max(excess_f[rows]) > 0.0, n_ktc, 0) for rows in groups16]
        n_gt = n_above + count16(low16_scr, (thr_rep & 0xFFFF) - 32768, True, trips)
        need = top_k - n_gt

        def tie_tile(kt, carry):
            c0, col = cols_of(kt)
            hi16_scr[:, pl.ds(c0, tk)] = jnp.where(keys_scr[:, pl.ds(c0, tk)] == thr, col, TIE_NONE).astype(i16)
            return carry

        lax.fori_loop(0, n_kt, tie_tile, 0)

        def tie_step(it, xv):
            cand = xv | lax.shift_left(jnp.int32(1), idx_bits - 1 - it)
            ties_below = n_cols - count16(hi16_scr, cand, False, trips)
            return jnp.where(ties_below < need, cand, xv)

        x_rep = lax.fori_loop(0, idx_bits, tie_step, jnp.zeros((tq, LANES), i32))
        xlim = jnp.where(excess, x_rep[:, 0:1], TIE_NONE - 1)

        def write_tile(kt, carry):
            c0, _ = cols_of(kt)
            ks = keys_scr[:, pl.ds(c0, tk)]
            tie_idx = hi16_scr[:, pl.ds(c0, tk)].astype(i32)
            sel = jnp.logical_and(jnp.logical_or(ks > thr, tie_idx <= xlim), ks >= thr_vis)
            out_ref[:, pl.ds(c0, tk)] = jnp.where(sel, 0.0, NEG_INF).astype(bf16)
            return carry

        lax.fori_loop(0, n_kt, write_tile, 0)


def _dsa_select(qi, wi, kit, seq, top_k):
    tq, tk = min(256, seq), 512
    assert seq < TIE_NONE and seq % SEL_COUNT_COLS == 0
    idx_bits = max(1, int(math.ceil(math.log2(seq))))
    body = functools.partial(_dsa_select_body, tq=tq, tk=tk, top_k=top_k, idx_bits=idx_bits)
    return pl.pallas_call(
        body,
        grid=(seq // tq,),
        in_specs=[pl.BlockSpec((tq, qi.shape[1]), lambda i: (i, 0)),
                  pl.BlockSpec((tq, wi.shape[1]), lambda i: (i, 0)),
                  pl.BlockSpec(kit.shape, lambda i: (0, 0))],
        out_specs=pl.BlockSpec((tq, seq), lambda i: (i, 0)),
        out_shape=jax.ShapeDtypeStruct((seq, seq), bf16),
        scratch_shapes=[pltpu.VMEM((tq, seq), i32), pltpu.VMEM((tq, seq), i16), pltpu.VMEM((tq, seq), i16)],
        compiler_params=_cparams(("parallel",)),
        name="dsa_select",
    )(qi, wi, kit)


FLAG_FIRST, FLAG_MASK, FLAG_LAST = 1, 2, 4


def _flash_body(it_ref, jt_ref, ft_ref, q_ref, k_ref, v_ref, *rest, tq, tk, n_heads, dq, dv, v_group, has_bias, diff,
                lambda_init):
    rest = list(rest)
    bias_ref = rest.pop(0) if has_bias else None
    if diff:
        lq1_ref, lk1_ref, lq2_ref, lk2_ref, gsub_ref = rest[:5]
        rest = rest[5:]
    o_ref, m_scr, acc_scr = rest
    dvp = _v_pad(dv)
    step = pl.program_id(0)
    i = it_ref[step]
    j = jt_ref[step]
    flag = ft_ref[step]

    @pl.when((flag & FLAG_FIRST) != 0)
    def _():
        m_scr[...] = jnp.full(m_scr.shape, NEG_INF, f32)
        acc_scr[...] = jnp.zeros(acc_scr.shape, f32)

    def attend(bias):
        for h in range(n_heads):
            hv = h // v_group
            s = _dot_nt(q_ref[:, h * dq:(h + 1) * dq], k_ref[:, h * dq:(h + 1) * dq])
            if bias is not None:
                s = s + bias
            m_prev = m_scr[h]
            m_new = jnp.maximum(m_prev, jnp.max(s, axis=1, keepdims=True))
            alpha = jnp.exp2(m_prev - m_new)
            p = jnp.exp2(s - jnp.tile(m_new, (1, tk // LANES)))
            acc_scr[h] = (acc_scr[h] * jnp.tile(alpha, (1, dvp // LANES))
                          + _dot(p.astype(bf16), v_ref[:, hv * dvp:(hv + 1) * dvp]))
            m_scr[h] = m_new

    if has_bias:
        attend(bias_ref[...].astype(f32))
    else:
        @pl.when((flag & FLAG_MASK) == 0)
        def _():
            attend(None)

        @pl.when((flag & FLAG_MASK) != 0)
        def _():
            r = (i * tq + lax.broadcasted_iota(i32, (tq, tk), 0)) // CHUNK
            c = (j * tk + lax.broadcasted_iota(i32, (tq, tk), 1)) // CHUNK
            attend(jnp.where(c <= r, 0.0, NEG_INF))

    @pl.when((flag & FLAG_LAST) != 0)
    def _():
        if diff:
            lam = (jnp.exp(jnp.sum(lq1_ref[...] * lk1_ref[...], axis=1, keepdims=True))
                   - jnp.exp(jnp.sum(lq2_ref[...] * lk2_ref[...], axis=1, keepdims=True)) + lambda_init)
            for hc in range(n_heads // 2):
                a1, a2 = acc_scr[2 * hc], acc_scr[2 * hc + 1]
                o1 = a1[:, :dv] / a1[:, dv:dv + 1]
                o2 = a2[:, :dv] / a2[:, dv:dv + 1]
                o = _rms(o1 - lam * o2, gsub_ref[...]) * (1.0 - lambda_init)
                o_ref[:, hc * dv:(hc + 1) * dv] = o.astype(o_ref.dtype)
        else:
            for h in range(n_heads):
                a = acc_scr[h]
                o_ref[:, h * dv:(h + 1) * dv] = (a[:, :dv] / a[:, dv:dv + 1]).astype(o_ref.dtype)


FLASH_TILE_ELEMS = 1024 * 512
FLASH_STATS_BYTES = 12 * 1024 * 1024


def _v_pad(dv):
    return (dv // LANES + 1) * LANES


def _with_ones_column(w_v, n_heads, dv):
    k = w_v.shape[0]
    dvp = _v_pad(dv)
    w = jnp.pad(w_v.reshape(k, n_heads, dv), ((0, 0), (0, 0), (0, dvp - dv))).reshape(k, n_heads * dvp)
    one = jnp.zeros((1, n_heads, dvp), f32).at[:, :, dv].set(1.0).reshape(1, n_heads * dvp)
    return w, one


def _flash_tiles(seq, n_heads, dv):
    tq = 1024
    while tq > 128 and n_heads * tq * (LANES + _v_pad(dv)) * 4 > FLASH_STATS_BYTES:
        tq //= 2
    return min(tq, seq), min(FLASH_TILE_ELEMS // tq, seq)


def _flash(q, k, v, *, n_heads, dq, dv, v_group=1, bias=None, diff_params=None, lambda_init=0.0, name):
    seq = q.shape[0]
    tq, tk = _flash_tiles(seq, n_heads, dv)
    pairs = []
    for i in range(seq // tq):
        j_last = ((i + 1) * tq - 1) // tk
        for j in range(j_last + 1):
            needs_mask = (j + 1) * tk > i * tq + CHUNK
            pairs.append((i, j, (FLAG_FIRST if j == 0 else 0) | (FLAG_MASK if needs_mask else 0)
                          | (FLAG_LAST if j == j_last else 0)))
    it = jnp.asarray([p[0] for p in pairs], i32)
    jt = jnp.asarray([p[1] for p in pairs], i32)
    ft = jnp.asarray([p[2] for p in pairs], i32)
    n_out = (n_heads // v_group) * dv
    in_specs = [pl.BlockSpec((tq, q.shape[1]), lambda s, it, jt, ft: (it[s], 0)),
                pl.BlockSpec((tk, k.shape[1]), lambda s, it, jt, ft: (jt[s], 0)),
                pl.BlockSpec((tk, v.shape[1]), lambda s, it, jt, ft: (jt[s], 0))]
    args = [q, k, v]
    if bias is not None:
        in_specs.append(pl.BlockSpec((tq, tk), lambda s, it, jt, ft: (it[s], jt[s])))
        args.append(bias)
    if diff_params is not None:
        for a in diff_params:
            in_specs.append(pl.BlockSpec(a.shape, lambda s, it, jt, ft: (0, 0)))
            args.append(a)
    body = functools.partial(_flash_body, tq=tq, tk=tk, n_heads=n_heads, dq=dq, dv=dv, v_group=v_group,
                             has_bias=bias is not None, diff=diff_params is not None, lambda_init=lambda_init)
    return pl.pallas_call(
        body,
        grid_spec=pltpu.PrefetchScalarGridSpec(
            num_scalar_prefetch=3,
            grid=(len(pairs),),
            in_specs=in_specs,
            out_specs=pl.BlockSpec((tq, n_out), lambda s, it, jt, ft: (it[s], 0)),
            scratch_shapes=[pltpu.VMEM((n_heads, tq, LANES), f32), pltpu.VMEM((n_heads, tq, _v_pad(dv)), f32)]),
        out_shape=jax.ShapeDtypeStruct((seq, n_out), bf16),
        compiler_params=_cparams(("arbitrary",)),
        name=name,
    )(it, jt, ft, *args)


def _silu_mul(gate, up):
    return gate / (1.0 + jnp.exp(-gate)) * up


def _ffn_body(x_ref, g_ref, wg_ref, wu_ref, wd_ref, o_ref, xn_scr, acc_scr):
    f = pl.program_id(1)

    @pl.when(f == 0)
    def _():
        xn_scr[...] = _rms(x_ref[...], g_ref[...]).astype(bf16)
        acc_scr[...] = jnp.zeros(acc_scr.shape, f32)

    xn = xn_scr[...]
    act = _silu_mul(_dot(xn, wg_ref[...]), _dot(xn, wu_ref[...])).astype(bf16)
    acc_scr[...] += _dot(act, wd_ref[...])

    @pl.when(f == pl.num_programs(1) - 1)
    def _():
        o_ref[...] = x_ref[...] + acc_scr[...]


def _ffn(x, g, wg, wu, wd, *, tm, tf, name):
    seq = x.shape[0]
    dff = wg.shape[1]
    return pl.pallas_call(
        _ffn_body,
        grid=(seq // tm, dff // tf),
        in_specs=[pl.BlockSpec((tm, D_MODEL), lambda i, f: (i, 0)),
                  pl.BlockSpec((1, D_MODEL), lambda i, f: (0, 0)),
                  pl.BlockSpec((D_MODEL, tf), lambda i, f: (0, f)),
                  pl.BlockSpec((D_MODEL, tf), lambda i, f: (0, f)),
                  pl.BlockSpec((tf, D_MODEL), lambda i, f: (f, 0))],
        out_specs=pl.BlockSpec((tm, D_MODEL), lambda i, f: (i, 0)),
        out_shape=jax.ShapeDtypeStruct((seq, D_MODEL), f32),
        scratch_shapes=[pltpu.VMEM((tm, D_MODEL), bf16), pltpu.VMEM((tm, D_MODEL), f32)],
        compiler_params=_cparams(("parallel", "arbitrary")),
        name=name,
    )(x, g, wg, wu, wd)


MOE_TILE = 512
PAIR_FIRST, PAIR_VALID, PAIR_LAST = 1, 2, 4


def _route(x, g, w_router):
    seq = x.shape[0]
    tm = min(MOE_TILE, seq)
    nb = seq // tm
    w_r = _pad_cols(w_router, LANES)
    w_hi = w_r.astype(bf16)
    w_lo = (w_r - w_hi.astype(f32)).astype(bf16)
    row = lambda n: pl.BlockSpec((tm, n), lambda i: (i, 0))
    const = lambda a: pl.BlockSpec(a.shape, lambda i: (0, 0))
    return pl.pallas_call(
        _router_body,
        grid=(nb,),
        in_specs=[row(D_MODEL), const(g), const(w_hi), const(w_lo)],
        out_specs=[row(D_MODEL), row(LANES), row(LANES), pl.BlockSpec((1, 8, LANES), lambda i: (i, 0, 0))],
        out_shape=[jax.ShapeDtypeStruct((seq, D_MODEL), bf16), jax.ShapeDtypeStruct((seq, LANES), f32),
                   jax.ShapeDtypeStruct((seq, LANES), f32), jax.ShapeDtypeStruct((nb, 8, LANES), f32)],
        scratch_shapes=[pltpu.VMEM((8, LANES), f32)],
        compiler_params=_cparams(("arbitrary",)),
        name="router",
    )(x, g, w_hi, w_lo)


def _moe_schedule(counts_after, seq):
    tm = min(MOE_TILE, seq)
    nb = seq // tm
    n_tiles = 2 * nb + N_EXP
    kmax = nb + 1
    max_pairs = n_tiles + N_EXP * nb
    bounds = jnp.concatenate([jnp.zeros((1, N_EXP), i32), counts_after[:, 0, :N_EXP].astype(i32)], 0)
    cnt = bounds[-1]
    ntile = (cnt + tm - 1) // tm
    tile_end = jnp.cumsum(ntile)
    tile_start = tile_end - ntile
    n_valid = tile_end[-1]
    p_ids = jnp.minimum(jnp.arange(n_tiles, dtype=i32), n_valid - 1)
    tile_expert = jnp.minimum(jnp.sum(p_ids[:, None] >= tile_end[None, :], axis=1), N_EXP - 1).astype(i32)
    tile_valid = (jnp.arange(n_tiles, dtype=i32) < n_valid).astype(i32)

    lo = jnp.transpose(bounds[:-1])[:, None, :]
    hi = jnp.transpose(bounds[1:])[:, None, :]
    k0 = (jnp.arange(kmax, dtype=i32) * tm)[None, :, None]
    meet = jnp.maximum(k0, lo) < jnp.minimum(k0 + tm, hi)
    n_pairs = jnp.sum(meet)
    s_ids = jnp.arange(max_pairs, dtype=i32)
    s_eff = jnp.minimum(s_ids, n_pairs - 1)

    def pair_list(flat, decode, group_of):
        idx = jnp.nonzero(flat, size=max_pairs, fill_value=0)[0].astype(i32)[s_eff]
        e, k, b = decode(idx)
        p = tile_start[e] + k
        grp = group_of(p, b)
        valid = s_ids < n_pairs
        first = jnp.concatenate([jnp.ones((1,), bool), grp[1:] != grp[:-1]])
        last = jnp.concatenate([grp[1:] != grp[:-1], jnp.ones((1,), bool)]) | (s_ids == n_pairs - 1)
        flags = jnp.where(valid, PAIR_VALID + PAIR_FIRST * first + PAIR_LAST * last, 0).astype(i32)
        return p.astype(i32), b.astype(i32), e.astype(i32), k.astype(i32), flags

    tile_major = pair_list(meet.reshape(-1),
                           lambda i: (i // (kmax * nb), (i // nb) % kmax, i % nb), lambda p, b: p)
    block_major = pair_list(jnp.transpose(meet, (2, 0, 1)).reshape(-1),
                            lambda i: ((i // kmax) % N_EXP, i % kmax, i // (N_EXP * kmax)), lambda p, b: b)
    return n_tiles, tile_expert, tile_valid, tile_major, block_major


def _moe_gather_body(pt, pb, pe, pk, pf, xn_ref, rank_t_ref, o_ref):
    s = pl.program_id(0)
    flag = pf[s]
    tmg, tb = o_ref.shape[0], xn_ref.shape[0]

    @pl.when((flag & PAIR_FIRST) != 0)
    def _():
        o_ref[...] = jnp.zeros(o_ref.shape, o_ref.dtype)

    @pl.when((flag & PAIR_VALID) != 0)
    def _():
        r = rank_t_ref[pl.ds(pe[s], 1), :] - (pk[s] * tmg).astype(f32)
        rows = lax.broadcasted_iota(i32, (tmg, tb), 0).astype(f32)
        onehot = jnp.where(rows == r, 1.0, 0.0).astype(bf16)
        o_ref[...] = o_ref[...] + _dot(onehot, xn_ref[...]).astype(o_ref.dtype)


def _moe_ffn_body(te, tv, x_ref, wg_ref, wu_ref, wd_ref, y_ref, acc_scr):
    p = pl.program_id(0)
    f = pl.program_id(1)
    last_f = f == pl.num_programs(1) - 1

    @pl.when(tv[p] != 0)
    def _():
        @pl.when(f == 0)
        def _():
            acc_scr[...] = jnp.zeros(acc_scr.shape, f32)

        x = x_ref[...]
        act = _silu_mul(_dot(x, wg_ref[0]), _dot(x, wu_ref[0])).astype(bf16)
        acc_scr[...] += _dot(act, wd_ref[0])

        @pl.when(last_f)
        def _():
            y_ref[...] = acc_scr[...].astype(y_ref.dtype)

    @pl.when(jnp.logical_and(tv[p] == 0, last_f))
    def _():
        y_ref[...] = jnp.zeros(y_ref.shape, y_ref.dtype)


def _moe_combine_body(ct, cb, ce, ck, cf, h_ref, y_ref, rank_ref, comb_ref, fg_ref, o_ref, acc_scr):
    s = pl.program_id(0)
    flag = cf[s]
    tb, tmg = h_ref.shape[0], y_ref.shape[0]

    @pl.when((flag & PAIR_FIRST) != 0)
    def _():
        acc_scr[...] = h_ref[...]

    @pl.when((flag & PAIR_VALID) != 0)
    def _():
        lane = lax.broadcasted_iota(i32, rank_ref.shape, 1)
        mine = lane == ce[s]
        r = jnp.sum(jnp.where(mine, rank_ref[...], 0.0), axis=1, keepdims=True) - (ck[s] * tmg).astype(f32)
        gate = jnp.sum(jnp.where(mine, comb_ref[...], 0.0), axis=1, keepdims=True)
        cols = lax.broadcasted_iota(i32, (tb, tmg), 1).astype(f32)
        onehot = jnp.where(cols == r, 1.0, 0.0).astype(bf16)
        acc_scr[...] += gate * _dot(onehot, y_ref[...])

    @pl.when((flag & PAIR_LAST) != 0)
    def _():
        o_ref[...] = _rms(acc_scr[...], fg_ref[...])


def _moe(h, g_ffn, w_router, w_gate_e, w_up_e, w_down_e, final_g):
    seq = h.shape[0]
    tm = min(MOE_TILE, seq)
    xn, comb, rank, counts_after = _route(h, g_ffn, w_router)
    n_tiles, tile_expert, tile_valid, tile_major, block_major = _moe_schedule(counts_after, seq)
    n_pairs = tile_major[0].shape[0]
    rank_t = jnp.transpose(rank[:, :8])

    x_sorted = pl.pallas_call(
        _moe_gather_body,
        grid_spec=pltpu.PrefetchScalarGridSpec(
            num_scalar_prefetch=5, grid=(n_pairs,),
            in_specs=[pl.BlockSpec((tm, D_MODEL), lambda s, pt, pb, pe, pk, pf: (pb[s], 0)),
                      pl.BlockSpec((8, tm), lambda s, pt, pb, pe, pk, pf: (0, pb[s]))],
            out_specs=pl.BlockSpec((tm, D_MODEL), lambda s, pt, pb, pe, pk, pf: (pt[s], 0))),
        out_shape=jax.ShapeDtypeStruct((n_tiles * tm, D_MODEL), bf16),
        compiler_params=_cparams(("arbitrary",)),
        name="moe_gather",
    )(*tile_major, xn, rank_t)

    tf = D_FF_E // 4
    n_f = D_FF_E // tf
    f_eff = lambda f, v: f * v + (n_f - 1) * (1 - v)
    y_sorted = pl.pallas_call(
        _moe_ffn_body,
        grid_spec=pltpu.PrefetchScalarGridSpec(
            num_scalar_prefetch=2, grid=(n_tiles, n_f),
            in_specs=[pl.BlockSpec((tm, D_MODEL), lambda p, f, te, tv: (p, 0)),
                      pl.BlockSpec((1, D_MODEL, tf), lambda p, f, te, tv: (te[p], 0, f_eff(f, tv[p]))),
                      pl.BlockSpec((1, D_MODEL, tf), lambda p, f, te, tv: (te[p], 0, f_eff(f, tv[p]))),
                      pl.BlockSpec((1, tf, D_MODEL), lambda p, f, te, tv: (te[p], f_eff(f, tv[p]), 0))],
            out_specs=pl.BlockSpec((tm, D_MODEL), lambda p, f, te, tv: (p, 0)),
            scratch_shapes=[pltpu.VMEM((tm, D_MODEL), f32)]),
        out_shape=jax.ShapeDtypeStruct((n_tiles * tm, D_MODEL), bf16),
        compiler_params=_cparams(("arbitrary", "arbitrary")),
        name="moe_ffn",
    )(tile_expert, tile_valid, x_sorted, w_gate_e.astype(bf16), w_up_e.astype(bf16), w_down_e.astype(bf16))

    return pl.pallas_call(
        _moe_combine_body,
        grid_spec=pltpu.PrefetchScalarGridSpec(
            num_scalar_prefetch=5, grid=(n_pairs,),
            in_specs=[pl.BlockSpec((tm, D_MODEL), lambda s, ct, cb, ce, ck, cf: (cb[s], 0)),
                      pl.BlockSpec((tm, D_MODEL), lambda s, ct, cb, ce, ck, cf: (ct[s], 0)),
                      pl.BlockSpec((tm, LANES), lambda s, ct, cb, ce, ck, cf: (cb[s], 0)),
                      pl.BlockSpec((tm, LANES), lambda s, ct, cb, ce, ck, cf: (cb[s], 0)),
                      pl.BlockSpec((1, D_MODEL), lambda s, ct, cb, ce, ck, cf: (0, 0))],
            out_specs=pl.BlockSpec((tm, D_MODEL), lambda s, ct, cb, ce, ck, cf: (cb[s], 0)),
            scratch_shapes=[pltpu.VMEM((tm, D_MODEL), f32)]),
        out_shape=jax.ShapeDtypeStruct((seq, D_MODEL), f32),
        compiler_params=_cparams(("arbitrary",)),
        name="moe_combine",
    )(*block_major, h, y_sorted, rank, comb, final_g)


def _even_layer(h, norm_mix, w_in, g_q_lat, w_uq, g_kv_lat, w_ukv, g_idx_k, w_out, norm_ffn, w_gate, w_up, w_down):
    seq = h.shape[0]
    sizes = (Q_LORA, KV_LORA, ROPE_A, H_B * DH_B, H_B * DH_B, H_B * DH_B, H_IDX * D_IDX, D_IDX, H_IDX)
    offs = np.cumsum((0,) + sizes)
    w_cq, w_ckv, w_kr, w_qb, w_kb, w_vb, w_qi, w_ki, w_wi = [w_in[:, offs[n]:offs[n + 1]] for n in range(9)]

    scale_a = (NOPE_A + ROPE_A) ** -0.5 * LOG2E
    wq3 = (w_uq * scale_a).reshape(Q_LORA, H_A, NOPE_A + ROPE_A)
    wq = jnp.pad(wq3, ((0, 0), (0, 0), (0, HEAD_PAD_A - NOPE_A - ROPE_A))).reshape(Q_LORA, H_A * HEAD_PAD_A)
    wq_sw = _swap_cols(wq, HEAD_PAD_A, NOPE_A, ROPE_A)
    wkv3 = w_ukv.reshape(KV_LORA, H_A, NOPE_A + V_A)
    wk = jnp.pad(wkv3[:, :, :NOPE_A], ((0, 0), (0, 0), (0, HEAD_PAD_A - NOPE_A))).reshape(KV_LORA, H_A * HEAD_PAD_A)
    wv, one_a = _with_ones_column(wkv3[:, :, NOPE_A:].reshape(KV_LORA, H_A * V_A), H_A, V_A)
    w_vb, one_b = _with_ones_column(w_vb, H_B, DH_B)
    place = jnp.zeros((ROPE_A, H_A, HEAD_PAD_A), f32)
    place = place.at[:, :, NOPE_A:NOPE_A + ROPE_A].set(jnp.eye(ROPE_A, dtype=f32)[:, None, :])
    place = place.reshape(ROPE_A, H_A * HEAD_PAD_A)

    w_qb = w_qb * (DH_B ** -0.5 * LOG2E)
    w_qi = w_qi * D_IDX ** -0.5
    w_small = jnp.concatenate([w_kr, _swap_cols(w_kr, ROPE_A, 0, ROPE_A), w_ki, _swap_cols(w_ki, D_IDX, 0, ROT_IDX)], 1)
    w_wi_p = _pad_cols(w_wi * H_IDX ** -0.5, LANES)
    g_idx = g_idx_k.reshape(1, D_IDX)
    g_idx_sw = jnp.concatenate([g_idx[:, ROT_IDX // 2:ROT_IDX], g_idx[:, :ROT_IDX // 2], g_idx[:, ROT_IDX:]], 1)

    ca, sa = _rope_tables(seq, ROPE_A, HEAD_PAD_A, NOPE_A, LANES)
    cb, sb = _rope_tables(seq, ROT_B, DH_B, 0, LANES)
    ci, si = _rope_tables(seq, ROT_IDX, D_IDX, 0, LANES)
    ckr, skr = _rope_tables(seq, ROPE_A, ROPE_A, 0, ROPE_A)

    consts = [norm_mix.reshape(1, -1), g_q_lat.reshape(1, -1), g_kv_lat.reshape(1, -1), g_idx, g_idx_sw,
              one_a, one_b]
    weights = [jnp.concatenate([w_cq, w_ckv], 1), wq, wq_sw, wk, wv, place,
               w_qb, _swap_cols(w_qb, DH_B, 0, ROT_B), w_kb, _swap_cols(w_kb, DH_B, 0, ROT_B), w_vb,
               w_qi, _swap_cols(w_qi, D_IDX, 0, ROT_IDX), w_small, w_wi_p]
    weights = [w.astype(bf16) for w in weights]
    sds = lambda n, dt: jax.ShapeDtypeStruct((seq, n), dt)
    outs = [sds(H_A * HEAD_PAD_A, bf16), sds(H_A * HEAD_PAD_A, bf16), sds(H_A * _v_pad(V_A), bf16),
            sds(H_B * DH_B, bf16), sds(H_B * DH_B, bf16), sds(H_B * _v_pad(DH_B), bf16),
            sds(H_IDX * D_IDX, bf16), sds(D_IDX, bf16), sds(LANES, f32)]
    qa, ka, va, qb, kb, vb, qi, ki, wi = _rows_call(
        _even_proj_body, seq, 256, [h, ca, sa, cb, sb, ci, si, ckr, skr], consts + weights, outs, "even_proj")

    o_a = _flash(qa, ka, va, n_heads=H_A, dq=HEAD_PAD_A, dv=V_A, name="mla_attn")
    top_k = min(TOPK_MAX, seq // 4)
    bias = _dsa_select(qi, wi, ki.T, seq, top_k)
    o_b = _flash(qb, kb, vb, n_heads=H_B, dq=DH_B, dv=DH_B, bias=bias, name="dsa_attn")

    w_out = w_out.astype(bf16)
    n_a = H_A * V_A
    (h,) = _rows_call(_out_proj2_body, seq, 512, [h, o_a, o_b], [w_out[:n_a], w_out[n_a:]],
                      [jax.ShapeDtypeStruct((seq, D_MODEL), f32)], "even_out_proj")
    return _ffn(h, norm_ffn.reshape(1, -1), w_gate.astype(bf16), w_up.astype(bf16), w_down.astype(bf16),
                tm=512, tf=D_FF // 2, name="dense_ffn")


def _odd_layer(h, layer, norm_mix, w_qkv, lq1, lk1, lq2, lk2, g_sub, w_out, norm_ffn, w_router, w_gate_e, w_up_e,
               w_down_e, final_norm):
    seq = h.shape[0]
    lambda_init = 0.8 - 0.6 * math.exp(-0.3 * layer)
    n = H_C * 2 * DH_C
    w_q = w_qkv[:, :n] * (DH_C ** -0.5 * LOG2E)
    w_k = w_qkv[:, n:2 * n]
    w_v, one_v = _with_ones_column(w_qkv[:, 2 * n:], H_C, 2 * DH_C)
    cb, sb = _rope_tables(seq, ROT_C, DH_C, 0, LANES)
    weights = [w_q, _swap_cols(w_q, DH_C, 0, ROT_C), w_k, _swap_cols(w_k, DH_C, 0, ROT_C), w_v]
    weights = [w.astype(bf16) for w in weights]
    sds = jax.ShapeDtypeStruct((seq, n), bf16)
    sds_v = jax.ShapeDtypeStruct((seq, w_v.shape[1]), bf16)
    q, k, v = _rows_call(_odd_proj_body, seq, 512, [h, cb, sb], [norm_mix.reshape(1, -1), one_v] + weights,
                         [sds, sds, sds_v], "odd_proj")
    diff_params = [lq1.reshape(1, -1), lk1.reshape(1, -1), lq2.reshape(1, -1), lk2.reshape(1, -1),
                   g_sub.reshape(1, -1)]
    o = _flash(q, k, v, n_heads=2 * H_C, dq=DH_C, dv=2 * DH_C, v_group=2, diff_params=diff_params,
               lambda_init=lambda_init, name="diff_attn")
    (h,) = _rows_call(_out_proj1_body, seq, 512, [h, o], [w_out.astype(bf16)],
                      [jax.ShapeDtypeStruct((seq, D_MODEL), f32)], "odd_out_proj")
    return _moe(h, norm_ffn.reshape(1, -1), w_router, w_gate_e, w_up_e, w_down_e, final_norm.reshape(1, -1))


def kernel(x, ev_norm_mix, ev_w_in, ev_g_q_lat, ev_w_uq, ev_g_kv_lat, ev_w_ukv, ev_g_idx_k, ev_w_out, ev_norm_ffn, ev_w_gate, ev_w_up, ev_w_down, od_norm_mix, od_w_qkv, od_lambda_q1, od_lambda_k1, od_lambda_q2, od_lambda_k2, od_g_sub, od_w_out, od_norm_ffn, od_w_router, od_w_gate_e, od_w_up_e, od_w_down_e, final_norm):
    batch, seq, _ = x.shape
    assert batch == 1 and ev_w_in.shape[0] == 1 and od_w_qkv.shape[0] == 1
    h = x[0]
    h = _even_layer(h, ev_norm_mix[0], ev_w_in[0], ev_g_q_lat[0], ev_w_uq[0], ev_g_kv_lat[0], ev_w_ukv[0],
                    ev_g_idx_k[0], ev_w_out[0], ev_norm_ffn[0], ev_w_gate[0], ev_w_up[0], ev_w_down[0])
    h = _odd_layer(h, 1, od_norm_mix[0], od_w_qkv[0], od_lambda_q1[0], od_lambda_k1[0], od_lambda_q2[0],
                   od_lambda_k2[0], od_g_sub[0], od_w_out[0], od_norm_ffn[0], od_w_router[0], od_w_gate_e[0],
                   od_w_up_e[0], od_w_down_e[0], final_norm)
    return h[None]
```

```python
import functools
import math

import numpy as np
import jax
import jax.numpy as jnp
from jax import lax
from jax.experimental import pallas as pl
from jax.experimental.pallas import tpu as pltpu

f32 = jnp.float32
bf16 = jnp.bfloat16
i32 = jnp.int32

D_MODEL = 1024
CHUNK = 64
ROPE_THETA = 500000.0
NORM_EPS = 1e-6
NEG_INF = -1e30
LOG2E = math.log2(math.e)

H_A, Q_LORA, KV_LORA, NOPE_A, ROPE_A, V_A = 8, 256, 128, 64, 32, 64
H_B, DH_B, ROT_B = 8, 64, 16
H_IDX, D_IDX, ROT_IDX = 8, 32, 8
TOPK_MAX = 256
H_C, DH_C, ROT_C = 8, 64, 16
D_FF, N_EXP, D_FF_E = 2816, 8, 3584

LANES = 128
HEAD_PAD_A = 128

_NEG_BITS = int(np.float32(NEG_INF).view(np.int32))
NEG_KEY = _NEG_BITS ^ 0x7FFFFFFF
INT_MIN = -(2 ** 31)

VMEM_LIMIT = 56 * 1024 * 1024


def _cparams(sem):
    return pltpu.CompilerParams(dimension_semantics=sem, vmem_limit_bytes=VMEM_LIMIT)


def _rms(x, g):
    var = jnp.mean(x * x, axis=-1, keepdims=True)
    return x * lax.rsqrt(var + NORM_EPS) * g


def _dot(a, b):
    return jnp.dot(a, b, preferred_element_type=f32)


def _dot_nt(a, b):
    return lax.dot_general(a, b, (((1,), (1,)), ((), ())), preferred_element_type=f32)


def _rope_tables(seq, rot_dim, head_width, offset, width):
    pos = jnp.arange(seq, dtype=f32)
    inv_freq = ROPE_THETA ** (-jnp.arange(0, rot_dim, 2, dtype=f32) / rot_dim)
    ang = pos[:, None] * inv_freq[None, :]
    cos, sin = jnp.cos(ang), jnp.sin(ang)
    c = jnp.ones((seq, head_width), f32).at[:, offset:offset + rot_dim].set(jnp.concatenate([cos, cos], -1))
    s = jnp.zeros((seq, head_width), f32).at[:, offset:offset + rot_dim].set(jnp.concatenate([-sin, sin], -1))
    reps = width // head_width
    return jnp.tile(c, (1, reps)), jnp.tile(s, (1, reps))


def _swap_cols(w, head_width, offset, rot_dim):
    k, n = w.shape
    half = rot_dim // 2
    w3 = w.reshape(k, n // head_width, head_width)
    out = jnp.zeros_like(w3)
    out = out.at[:, :, offset:offset + half].set(w3[:, :, offset + half:offset + rot_dim])
    out = out.at[:, :, offset + half:offset + rot_dim].set(w3[:, :, offset:offset + half])
    return out.reshape(k, n)


def _pad_cols(w, width):
    return jnp.pad(w, ((0, 0), (0, width - w.shape[1])))


def _rows_call(body, seq, tm, row_ins, const_ins, out_sds, name):
    def rspec(a):
        return pl.BlockSpec((tm, a.shape[1]), lambda i: (i, 0))

    def cspec(a):
        nd = a.ndim
        return pl.BlockSpec(a.shape, lambda i: (0,) * nd)

    return pl.pallas_call(
        body,
        grid=(seq // tm,),
        in_specs=[rspec(a) for a in row_ins] + [cspec(a) for a in const_ins],
        out_specs=[pl.BlockSpec((tm, o.shape[1]), lambda i: (i, 0)) for o in out_sds],
        out_shape=out_sds,
        compiler_params=_cparams(("parallel",)),
        name=name,
    )(*row_ins, *const_ins)


def _even_proj_body(x_ref, ca_ref, sa_ref, cb_ref, sb_ref, ci_ref, si_ref, ckr_ref, skr_ref,
                    g_ref, gq_ref, gkv_ref, gi_ref, gisw_ref, onea_ref, oneb_ref,
                    wlat_ref, wq_ref, wqsw_ref, wk_ref, wv_ref, place_ref,
                    wqb_ref, wqbsw_ref, wkb_ref, wkbsw_ref, wvb_ref, wqi_ref, wqisw_ref, wsm_ref, wwi_ref,
                    qa_ref, ka_ref, va_ref, qb_ref, kb_ref, vb_ref, qi_ref, ki_ref, wi_ref):
    xn = _rms(x_ref[...], g_ref[...]).astype(bf16)
    lat = _dot(xn, wlat_ref[...])
    cqn = _rms(lat[:, :Q_LORA], gq_ref[...]).astype(bf16)
    ckvn = _rms(lat[:, Q_LORA:], gkv_ref[...]).astype(bf16)
    reps_a = qa_ref.shape[1] // LANES
    ca = jnp.tile(ca_ref[...], (1, reps_a))
    sa = jnp.tile(sa_ref[...], (1, reps_a))
    qa_ref[...] = (_dot(cqn, wq_ref[...]) * ca + _dot(cqn, wqsw_ref[...]) * sa).astype(bf16)
    small = _dot(xn, wsm_ref[...])
    kr, kr_sw = small[:, 0:ROPE_A], small[:, ROPE_A:2 * ROPE_A]
    kpe = (kr * ckr_ref[...] + kr_sw * skr_ref[...]).astype(bf16)
    ka_ref[...] = (_dot(ckvn, wk_ref[...]) + _dot(kpe, place_ref[...])).astype(bf16)
    va_ref[...] = (_dot(ckvn, wv_ref[...]) + onea_ref[...]).astype(bf16)
    reps_b = qb_ref.shape[1] // LANES
    cb = jnp.tile(cb_ref[...], (1, reps_b))
    sb = jnp.tile(sb_ref[...], (1, reps_b))
    qb_ref[...] = (_dot(xn, wqb_ref[...]) * cb + _dot(xn, wqbsw_ref[...]) * sb).astype(bf16)
    kb_ref[...] = (_dot(xn, wkb_ref[...]) * cb + _dot(xn, wkbsw_ref[...]) * sb).astype(bf16)
    vb_ref[...] = (_dot(xn, wvb_ref[...]) + oneb_ref[...]).astype(bf16)
    reps_i = qi_ref.shape[1] // LANES
    ci = jnp.tile(ci_ref[...], (1, reps_i))
    si = jnp.tile(si_ref[...], (1, reps_i))
    qi_ref[...] = (_dot(xn, wqi_ref[...]) * ci + _dot(xn, wqisw_ref[...]) * si).astype(bf16)
    ki, ki_sw = small[:, 2 * ROPE_A:2 * ROPE_A + D_IDX], small[:, 2 * ROPE_A + D_IDX:2 * ROPE_A + 2 * D_IDX]
    r = lax.rsqrt(jnp.mean(ki * ki, axis=-1, keepdims=True) + NORM_EPS)
    ci32, si32 = ci_ref[:, 0:D_IDX], si_ref[:, 0:D_IDX]
    ki_ref[...] = (ki * r * gi_ref[...] * ci32 + ki_sw * r * gisw_ref[...] * si32).astype(bf16)
    wi_ref[...] = _dot(xn, wwi_ref[...])


def _odd_proj_body(x_ref, cb_ref, sb_ref, g_ref, one_ref, wq_ref, wqsw_ref, wk_ref, wksw_ref, wv_ref,
                   q_ref, k_ref, v_ref):
    xn = _rms(x_ref[...], g_ref[...]).astype(bf16)
    reps = q_ref.shape[1] // LANES
    cb = jnp.tile(cb_ref[...], (1, reps))
    sb = jnp.tile(sb_ref[...], (1, reps))
    q_ref[...] = (_dot(xn, wq_ref[...]) * cb + _dot(xn, wqsw_ref[...]) * sb).astype(bf16)
    k_ref[...] = (_dot(xn, wk_ref[...]) * cb + _dot(xn, wksw_ref[...]) * sb).astype(bf16)
    v_ref[...] = (_dot(xn, wv_ref[...]) + one_ref[...]).astype(bf16)


def _out_proj2_body(x_ref, a1_ref, a2_ref, w1_ref, w2_ref, o_ref):
    o_ref[...] = x_ref[...] + _dot(a1_ref[...], w1_ref[...]) + _dot(a2_ref[...], w2_ref[...])


def _out_proj1_body(x_ref, a_ref, w_ref, o_ref):
    o_ref[...] = x_ref[...] + _dot(a_ref[...], w_ref[...])


def _router_body(x_ref, g_ref, whi_ref, wlo_ref, xn_ref, comb_ref, rank_ref, cnt_ref, carry_scr):
    @pl.when(pl.program_id(0) == 0)
    def _():
        carry_scr[...] = jnp.zeros(carry_scr.shape, f32)

    xn = _rms(x_ref[...], g_ref[...])
    xn_ref[...] = xn.astype(bf16)
    hi = xn.astype(bf16)
    lo = (xn - hi.astype(f32)).astype(bf16)
    logits = _dot(hi, whi_ref[...]) + _dot(lo, whi_ref[...]) + _dot(hi, wlo_ref[...])
    lane = lax.broadcasted_iota(i32, logits.shape, 1).astype(f32)
    lg = jnp.where(lane < N_EXP, logits, -jnp.inf)
    m1 = jnp.max(lg, axis=1, keepdims=True)
    i1 = jnp.min(jnp.where(lg == m1, lane, float(LANES)), axis=1, keepdims=True)
    lg2 = jnp.where(lane == i1, -jnp.inf, lg)
    m2 = jnp.max(lg2, axis=1, keepdims=True)
    i2 = jnp.min(jnp.where(lg2 == m2, lane, float(LANES)), axis=1, keepdims=True)
    e2 = jnp.exp(m2 - m1)
    den = 1.0 + e2
    comb_ref[...] = jnp.where(lane == i1, 1.0 / den, 0.0) + jnp.where(lane == i2, e2 / den, 0.0)
    routed = jnp.logical_or(lane == i1, lane == i2)
    onehot = jnp.where(routed, 1.0, 0.0)
    tm = onehot.shape[0]
    earlier = lax.broadcasted_iota(i32, (tm, tm), 1) < lax.broadcasted_iota(i32, (tm, tm), 0)
    before = _dot(jnp.where(earlier, 1.0, 0.0).astype(bf16), onehot.astype(bf16))
    carry = carry_scr[0:1, :]
    rank_ref[...] = jnp.where(routed, before + carry, -1.0)
    carry = carry + jnp.sum(onehot, axis=0, keepdims=True)
    carry_scr[...] = jnp.broadcast_to(carry, carry_scr.shape)
    cnt_ref[0] = jnp.broadcast_to(carry, cnt_ref.shape[1:])


SEL_COUNT_ROWS, SEL_COUNT_COLS = 64, 1024
TIE_NONE = 2 ** 30


def _dsa_select_body(qi_ref, wi_ref, kit_ref, out_ref, keys_scr, tie_scr, *, tq, tk, top_k, idx_bits):
    q0 = pl.program_id(0) * tq
    n_ktc = (q0 + tq + SEL_COUNT_COLS - 1) // SEL_COUNT_COLS
    n_kt = n_ktc * (SEL_COUNT_COLS // tk)
    qh = [qi_ref[:, h * D_IDX:(h + 1) * D_IDX] for h in range(H_IDX)]
    w = wi_ref[...]
    wb = [jnp.broadcast_to(w[:, h:h + 1], (tq, tk)) for h in range(H_IDX)]
    row = q0 + lax.broadcasted_iota(i32, (tq, 1), 0)
    row_lim = (row // CHUNK + 1) * CHUNK

    def cols_of(kt):
        c0 = pl.multiple_of(kt * tk, tk)
        return c0, c0 + lax.broadcasted_iota(i32, (tq, tk), 1)

    def score_tile(kt, carry):
        c0, col = cols_of(kt)
        kt_tile = kit_ref[:, pl.ds(c0, tk)]
        acc = jnp.zeros((tq, tk), f32)
        for h in range(H_IDX):
            acc = acc + jnp.maximum(_dot(qh[h], kt_tile), 0.0) * wb[h]
        sc = jnp.where(col < row_lim, acc, NEG_INF)
        bits = lax.bitcast_convert_type(sc, i32)
        keys_scr[:, pl.ds(c0, tk)] = jnp.where(bits < 0, bits ^ 0x7FFFFFFF, bits)
        return carry

    lax.fori_loop(0, n_kt, score_tile, 0)

    ones_mat = jnp.ones((LANES, LANES), bf16)

    groups = [slice(g * SEL_COUNT_ROWS, (g + 1) * SEL_COUNT_ROWS) for g in range(tq // SEL_COUNT_ROWS)]

    def count(scr, cand_rep, strict=False, trips=None):
        parts = []
        for g, rows in enumerate(groups):
            cand = cand_rep[rows]

            def body(kt, acc, rows=rows, cand=cand):
                c0 = pl.multiple_of(kt * SEL_COUNT_COLS, SEL_COUNT_COLS)
                ks = scr[rows, pl.ds(c0, SEL_COUNT_COLS)]
                for u in range(SEL_COUNT_COLS // LANES):
                    blk = ks[:, u * LANES:(u + 1) * LANES]
                    acc = acc + jnp.where(blk > cand if strict else blk >= cand, 1.0, 0.0)
                return acc

            n = n_ktc if trips is None else trips[g]
            parts.append(lax.fori_loop(0, n, body, jnp.zeros((SEL_COUNT_ROWS, LANES), f32)))
        return _dot(jnp.concatenate(parts, axis=0).astype(bf16), ones_mat)

    def key_float(k):
        return lax.bitcast_convert_type(jnp.where(k < 0, k ^ 0x7FFFFFFF, k), f32)

    lo_parts, hi_parts = [], []
    for rows in groups:
        def bounds_body(kt, carry, rows=rows):
            m_even, m_odd = carry
            c0 = pl.multiple_of(kt * SEL_COUNT_COLS, SEL_COUNT_COLS)
            ks = keys_scr[rows, pl.ds(c0, SEL_COUNT_COLS)]
            for u in range(0, SEL_COUNT_COLS // LANES, 2):
                m_even = jnp.maximum(m_even, ks[:, u * LANES:(u + 1) * LANES])
                m_odd = jnp.maximum(m_odd, ks[:, (u + 1) * LANES:(u + 2) * LANES])
            return m_even, m_odd

        lowest = jnp.full((SEL_COUNT_ROWS, LANES), INT_MIN, i32)
        m_even, m_odd = lax.fori_loop(0, n_ktc, bounds_body, (lowest, lowest))
        lo_parts.append(jnp.min(key_float(jnp.minimum(m_even, m_odd)), axis=1, keepdims=True))
        hi_parts.append(jnp.max(key_float(jnp.maximum(m_even, m_odd)), axis=1, keepdims=True))

    def float_key_rep(parts):
        bits = lax.bitcast_convert_type(jnp.broadcast_to(jnp.concatenate(parts, axis=0), (tq, LANES)), i32)
        return jnp.where(bits < 0, bits ^ 0x7FFFFFFF, bits)

    lo0 = float_key_rep(lo_parts) ^ INT_MIN
    width0 = (float_key_rep(hi_parts) ^ INT_MIN) - lo0 + 1
    n_steps = jnp.max((32 - lax.clz(width0 - 1)).astype(f32)).astype(i32)

    def thr_step(it, carry):
        lo, width, c = carry
        half = lax.shift_right_logical(width, 1)
        mid = lo + half
        cnt = count(keys_scr, mid ^ INT_MIN)
        keep = cnt >= top_k
        return jnp.where(keep, mid, lo), jnp.where(keep, width - half, half), jnp.where(keep, cnt, c)

    n_cols = jnp.full((tq, LANES), n_ktc * SEL_COUNT_COLS, i32).astype(f32)
    u_all, _, c_all = lax.fori_loop(0, n_steps, thr_step, (lo0, width0, count(keys_scr, lo0 ^ INT_MIN)))
    thr_rep = u_all ^ INT_MIN
    n_ge = c_all[:, 0:1].astype(i32)
    thr = thr_rep[:, 0:1]
    thr_vis = jnp.maximum(thr, NEG_KEY + 1)
    thr_vis_t = jnp.tile(jnp.maximum(thr_rep, NEG_KEY + 1), (1, tk // LANES))
    excess = jnp.logical_and(n_ge > top_k, thr > NEG_KEY)
    any_excess = jnp.max(excess.astype(f32)) > 0.0
    out_ref[...] = jnp.full(out_ref.shape, NEG_INF, bf16)

    @pl.when(jnp.logical_not(any_excess))
    def _():
        def write_tile(kt, carry):
            c0 = pl.multiple_of(kt * tk, tk)
            out_ref[:, pl.ds(c0, tk)] = jnp.where(keys_scr[:, pl.ds(c0, tk)] >= thr_vis_t, 0.0, NEG_INF).astype(bf16)
            return carry

        lax.fori_loop(0, n_kt, write_tile, 0)

    @pl.when(any_excess)
    def _():
        excess_f = excess.astype(f32)
        trips = [jnp.where(jnp.max(excess_f[rows]) > 0.0, n_ktc, 0) for rows in groups]
        need = top_k - count(keys_scr, thr_rep, strict=True, trips=trips)

        def tie_tile(kt, carry):
            c0, col = cols_of(kt)
            tie_scr[:, pl.ds(c0, tk)] = jnp.where(keys_scr[:, pl.ds(c0, tk)] == thr, col, TIE_NONE)
            return carry

        lax.fori_loop(0, n_kt, tie_tile, 0)

        def tie_step(it, xv):
            cand = xv | lax.shift_left(jnp.int32(1), idx_bits - 1 - it)
            ties_below = n_cols - count(tie_scr, cand, trips=trips)
            return jnp.where(ties_below < need, cand, xv)

        x_rep = lax.fori_loop(0, idx_bits, tie_step, jnp.zeros((tq, LANES), i32))
        xlim = jnp.where(excess, x_rep[:, 0:1], TIE_NONE - 1)

        def write_tile(kt, carry):
            c0 = pl.multiple_of(kt * tk, tk)
            ks = keys_scr[:, pl.ds(c0, tk)]
            sel = jnp.logical_or(ks > thr, tie_scr[:, pl.ds(c0, tk)] <= xlim)
            sel = jnp.logical_and(sel, ks >= thr_vis)
            out_ref[:, pl.ds(c0, tk)] = jnp.where(sel, 0.0, NEG_INF).astype(bf16)
            return carry

        lax.fori_loop(0, n_kt, write_tile, 0)


def _dsa_select(qi, wi, kit, seq, top_k):
    tq, tk = min(256, seq), 512
    assert seq < TIE_NONE and seq % SEL_COUNT_COLS == 0 and top_k <= 2 * LANES
    idx_bits = max(1, int(math.ceil(math.log2(seq))))
    body = functools.partial(_dsa_select_body, tq=tq, tk=tk, top_k=top_k, idx_bits=idx_bits)
    return pl.pallas_call(
        body,
        grid=(seq // tq,),
        in_specs=[pl.BlockSpec((tq, qi.shape[1]), lambda i: (i, 0)),
                  pl.BlockSpec((tq, wi.shape[1]), lambda i: (i, 0)),
                  pl.BlockSpec(kit.shape, lambda i: (0, 0))],
        out_specs=pl.BlockSpec((tq, seq), lambda i: (i, 0)),
        out_shape=jax.ShapeDtypeStruct((seq, seq), bf16),
        scratch_shapes=[pltpu.VMEM((tq, seq), i32), pltpu.VMEM((tq, seq), i32)],
        compiler_params=_cparams(("parallel",)),
        name="dsa_select",
    )(qi, wi, kit)


FLAG_FIRST, FLAG_MASK, FLAG_LAST = 1, 2, 4


def _flash_body(it_ref, jt_ref, ft_ref, q_ref, k_ref, v_ref, *rest, tq, tk, n_heads, dq, dv, v_group, has_bias, diff,
                lambda_init):
    rest = list(rest)
    bias_ref = rest.pop(0) if has_bias else None
    if diff:
        lq1_ref, lk1_ref, lq2_ref, lk2_ref, gsub_ref = rest[:5]
        rest = rest[5:]
    o_ref, m_scr, acc_scr = rest
    dvp = _v_pad(dv)
    step = pl.program_id(0)
    i = it_ref[step]
    j = jt_ref[step]
    flag = ft_ref[step]

    @pl.when((flag & FLAG_FIRST) != 0)
    def _():
        m_scr[...] = jnp.full(m_scr.shape, NEG_INF, f32)
        acc_scr[...] = jnp.zeros(acc_scr.shape, f32)

    def attend(bias):
        for h in range(n_heads):
            hv = h // v_group
            s = _dot_nt(q_ref[:, h * dq:(h + 1) * dq], k_ref[:, h * dq:(h + 1) * dq])
            if bias is not None:
                s = s + bias
            m_prev = m_scr[h]
            m_new = jnp.maximum(m_prev, jnp.max(s, axis=1, keepdims=True))
            alpha = jnp.exp2(m_prev - m_new)
            p = jnp.exp2(s - jnp.tile(m_new, (1, tk // LANES)))
            acc_scr[h] = (acc_scr[h] * jnp.tile(alpha, (1, dvp // LANES))
                          + _dot(p.astype(bf16), v_ref[:, hv * dvp:(hv + 1) * dvp]))
            m_scr[h] = m_new

    if has_bias:
        attend(bias_ref[...].astype(f32))
    else:
        @pl.when((flag & FLAG_MASK) == 0)
        def _():
            attend(None)

        @pl.when((flag & FLAG_MASK) != 0)
        def _():
            r = (i * tq + lax.broadcasted_iota(i32, (tq, tk), 0)) // CHUNK
            c = (j * tk + lax.broadcasted_iota(i32, (tq, tk), 1)) // CHUNK
            attend(jnp.where(c <= r, 0.0, NEG_INF))

    @pl.when((flag & FLAG_LAST) != 0)
    def _():
        if diff:
            lam = (jnp.exp(jnp.sum(lq1_ref[...] * lk1_ref[...], axis=1, keepdims=True))
                   - jnp.exp(jnp.sum(lq2_ref[...] * lk2_ref[...], axis=1, keepdims=True)) + lambda_init)
            for hc in range(n_heads // 2):
                a1, a2 = acc_scr[2 * hc], acc_scr[2 * hc + 1]
                o1 = a1[:, :dv] / a1[:, dv:dv + 1]
                o2 = a2[:, :dv] / a2[:, dv:dv + 1]
                o = _rms(o1 - lam * o2, gsub_ref[...]) * (1.0 - lambda_init)
                o_ref[:, hc * dv:(hc + 1) * dv] = o.astype(o_ref.dtype)
        else:
            for h in range(n_heads):
                a = acc_scr[h]
                o_ref[:, h * dv:(h + 1) * dv] = (a[:, :dv] / a[:, dv:dv + 1]).astype(o_ref.dtype)


FLASH_TILE_ELEMS = 1024 * 512
FLASH_STATS_BYTES = 12 * 1024 * 1024


def _v_pad(dv):
    return (dv // LANES + 1) * LANES


def _with_ones_column(w_v, n_heads, dv):
    k = w_v.shape[0]
    dvp = _v_pad(dv)
    w = jnp.pad(w_v.reshape(k, n_heads, dv), ((0, 0), (0, 0), (0, dvp - dv))).reshape(k, n_heads * dvp)
    one = jnp.zeros((1, n_heads, dvp), f32).at[:, :, dv].set(1.0).reshape(1, n_heads * dvp)
    return w, one


def _flash_tiles(seq, n_heads, dv):
    tq = 1024
    while tq > 128 and n_heads * tq * (LANES + _v_pad(dv)) * 4 > FLASH_STATS_BYTES:
        tq //= 2
    return min(tq, seq), min(FLASH_TILE_ELEMS // tq, seq)


def _flash(q, k, v, *, n_heads, dq, dv, v_group=1, bias=None, diff_params=None, lambda_init=0.0, name):
    seq = q.shape[0]
    tq, tk = _flash_tiles(seq, n_heads, dv)
    pairs = []
    for i in range(seq // tq):
        j_last = ((i + 1) * tq - 1) // tk
        for j in range(j_last + 1):
            needs_mask = (j + 1) * tk > i * tq + CHUNK
            pairs.append((i, j, (FLAG_FIRST if j == 0 else 0) | (FLAG_MASK if needs_mask else 0)
                          | (FLAG_LAST if j == j_last else 0)))
    it = jnp.asarray([p[0] for p in pairs], i32)
    jt = jnp.asarray([p[1] for p in pairs], i32)
    ft = jnp.asarray([p[2] for p in pairs], i32)
    n_out = (n_heads // v_group) * dv
    in_specs = [pl.BlockSpec((tq, q.shape[1]), lambda s, it, jt, ft: (it[s], 0)),
                pl.BlockSpec((tk, k.shape[1]), lambda s, it, jt, ft: (jt[s], 0)),
                pl.BlockSpec((tk, v.shape[1]), lambda s, it, jt, ft: (jt[s], 0))]
    args = [q, k, v]
    if bias is not None:
        in_specs.append(pl.BlockSpec((tq, tk), lambda s, it, jt, ft: (it[s], jt[s])))
        args.append(bias)
    if diff_params is not None:
        for a in diff_params:
            in_specs.append(pl.BlockSpec(a.shape, lambda s, it, jt, ft: (0, 0)))
            args.append(a)
    body = functools.partial(_flash_body, tq=tq, tk=tk, n_heads=n_heads, dq=dq, dv=dv, v_group=v_group,
                             has_bias=bias is not None, diff=diff_params is not None, lambda_init=lambda_init)
    return pl.pallas_call(
        body,
        grid_spec=pltpu.PrefetchScalarGridSpec(
            num_scalar_prefetch=3,
            grid=(len(pairs),),
            in_specs=in_specs,
            out_specs=pl.BlockSpec((tq, n_out), lambda s, it, jt, ft: (it[s], 0)),
            scratch_shapes=[pltpu.VMEM((n_heads, tq, LANES), f32), pltpu.VMEM((n_heads, tq, _v_pad(dv)), f32)]),
        out_shape=jax.ShapeDtypeStruct((seq, n_out), bf16),
        compiler_params=_cparams(("arbitrary",)),
        name=name,
    )(it, jt, ft, *args)


def _silu_mul(gate, up):
    return gate / (1.0 + jnp.exp(-gate)) * up


def _ffn_body(x_ref, g_ref, wg_ref, wu_ref, wd_ref, o_ref, xn_scr, acc_scr):
    f = pl.program_id(1)

    @pl.when(f == 0)
    def _():
        xn_scr[...] = _rms(x_ref[...], g_ref[...]).astype(bf16)
        acc_scr[...] = jnp.zeros(acc_scr.shape, f32)

    xn = xn_scr[...]
    act = _silu_mul(_dot(xn, wg_ref[...]), _dot(xn, wu_ref[...])).astype(bf16)
    acc_scr[...] += _dot(act, wd_ref[...])

    @pl.when(f == pl.num_programs(1) - 1)
    def _():
        o_ref[...] = x_ref[...] + acc_scr[...]


def _ffn(x, g, wg, wu, wd, *, tm, tf, name):
    seq = x.shape[0]
    dff = wg.shape[1]
    return pl.pallas_call(
        _ffn_body,
        grid=(seq // tm, dff // tf),
        in_specs=[pl.BlockSpec((tm, D_MODEL), lambda i, f: (i, 0)),
                  pl.BlockSpec((1, D_MODEL), lambda i, f: (0, 0)),
                  pl.BlockSpec((D_MODEL, tf), lambda i, f: (0, f)),
                  pl.BlockSpec((D_MODEL, tf), lambda i, f: (0, f)),
                  pl.BlockSpec((tf, D_MODEL), lambda i, f: (f, 0))],
        out_specs=pl.BlockSpec((tm, D_MODEL), lambda i, f: (i, 0)),
        out_shape=jax.ShapeDtypeStruct((seq, D_MODEL), f32),
        scratch_shapes=[pltpu.VMEM((tm, D_MODEL), bf16), pltpu.VMEM((tm, D_MODEL), f32)],
        compiler_params=_cparams(("parallel", "arbitrary")),
        name=name,
    )(x, g, wg, wu, wd)


MOE_TILE = 512
PAIR_FIRST, PAIR_VALID, PAIR_LAST = 1, 2, 4


def _route(x, g, w_router):
    seq = x.shape[0]
    tm = min(MOE_TILE, seq)
    nb = seq // tm
    w_r = _pad_cols(w_router, LANES)
    w_hi = w_r.astype(bf16)
    w_lo = (w_r - w_hi.astype(f32)).astype(bf16)
    row = lambda n: pl.BlockSpec((tm, n), lambda i: (i, 0))
    const = lambda a: pl.BlockSpec(a.shape, lambda i: (0, 0))
    return pl.pallas_call(
        _router_body,
        grid=(nb,),
        in_specs=[row(D_MODEL), const(g), const(w_hi), const(w_lo)],
        out_specs=[row(D_MODEL), row(LANES), row(LANES), pl.BlockSpec((1, 8, LANES), lambda i: (i, 0, 0))],
        out_shape=[jax.ShapeDtypeStruct((seq, D_MODEL), bf16), jax.ShapeDtypeStruct((seq, LANES), f32),
                   jax.ShapeDtypeStruct((seq, LANES), f32), jax.ShapeDtypeStruct((nb, 8, LANES), f32)],
        scratch_shapes=[pltpu.VMEM((8, LANES), f32)],
        compiler_params=_cparams(("arbitrary",)),
        name="router",
    )(x, g, w_hi, w_lo)


def _moe_schedule(counts_after, seq):
    tm = min(MOE_TILE, seq)
    nb = seq // tm
    n_tiles = 2 * nb + N_EXP
    kmax = nb + 1
    max_pairs = n_tiles + N_EXP * nb
    bounds = jnp.concatenate([jnp.zeros((1, N_EXP), i32), counts_after[:, 0, :N_EXP].astype(i32)], 0)
    cnt = bounds[-1]
    ntile = (cnt + tm - 1) // tm
    tile_end = jnp.cumsum(ntile)
    tile_start = tile_end - ntile
    n_valid = tile_end[-1]
    p_ids = jnp.minimum(jnp.arange(n_tiles, dtype=i32), n_valid - 1)
    tile_expert = jnp.minimum(jnp.sum(p_ids[:, None] >= tile_end[None, :], axis=1), N_EXP - 1).astype(i32)
    tile_valid = (jnp.arange(n_tiles, dtype=i32) < n_valid).astype(i32)

    lo = jnp.transpose(bounds[:-1])[:, None, :]
    hi = jnp.transpose(bounds[1:])[:, None, :]
    k0 = (jnp.arange(kmax, dtype=i32) * tm)[None, :, None]
    meet = jnp.maximum(k0, lo) < jnp.minimum(k0 + tm, hi)
    n_pairs = jnp.sum(meet)
    s_ids = jnp.arange(max_pairs, dtype=i32)
    s_eff = jnp.minimum(s_ids, n_pairs - 1)

    def pair_list(flat, decode, group_of):
        idx = jnp.nonzero(flat, size=max_pairs, fill_value=0)[0].astype(i32)[s_eff]
        e, k, b = decode(idx)
        p = tile_start[e] + k
        grp = group_of(p, b)
        valid = s_ids < n_pairs
        first = jnp.concatenate([jnp.ones((1,), bool), grp[1:] != grp[:-1]])
        last = jnp.concatenate([grp[1:] != grp[:-1], jnp.ones((1,), bool)]) | (s_ids == n_pairs - 1)
        flags = jnp.where(valid, PAIR_VALID + PAIR_FIRST * first + PAIR_LAST * last, 0).astype(i32)
        return p.astype(i32), b.astype(i32), e.astype(i32), k.astype(i32), flags

    tile_major = pair_list(meet.reshape(-1),
                           lambda i: (i // (kmax * nb), (i // nb) % kmax, i % nb), lambda p, b: p)
    block_major = pair_list(jnp.transpose(meet, (2, 0, 1)).reshape(-1),
                            lambda i: ((i // kmax) % N_EXP, i % kmax, i // (N_EXP * kmax)), lambda p, b: b)
    return n_tiles, tile_expert, tile_valid, tile_major, block_major


def _moe_gather_body(pt, pb, pe, pk, pf, xn_ref, rank_t_ref, o_ref):
    s = pl.program_id(0)
    flag = pf[s]
    tmg, tb = o_ref.shape[0], xn_ref.shape[0]

    @pl.when((flag & PAIR_FIRST) != 0)
    def _():
        o_ref[...] = jnp.zeros(o_ref.shape, o_ref.dtype)

    @pl.when((flag & PAIR_VALID) != 0)
    def _():
        r = rank_t_ref[pl.ds(pe[s], 1), :] - (pk[s] * tmg).astype(f32)
        rows = lax.broadcasted_iota(i32, (tmg, tb), 0).astype(f32)
        onehot = jnp.where(rows == r, 1.0, 0.0).astype(bf16)
        o_ref[...] = o_ref[...] + _dot(onehot, xn_ref[...]).astype(o_ref.dtype)


def _moe_ffn_body(te, tv, x_ref, wg_ref, wu_ref, wd_ref, y_ref, acc_scr):
    p = pl.program_id(0)
    f = pl.program_id(1)
    last_f = f == pl.num_programs(1) - 1

    @pl.when(tv[p] != 0)
    def _():
        @pl.when(f == 0)
        def _():
            acc_scr[...] = jnp.zeros(acc_scr.shape, f32)

        x = x_ref[...]
        act = _silu_mul(_dot(x, wg_ref[0]), _dot(x, wu_ref[0])).astype(bf16)
        acc_scr[...] += _dot(act, wd_ref[0])

        @pl.when(last_f)
        def _():
            y_ref[...] = acc_scr[...].astype(y_ref.dtype)

    @pl.when(jnp.logical_and(tv[p] == 0, last_f))
    def _():
        y_ref[...] = jnp.zeros(y_ref.shape, y_ref.dtype)


def _moe_combine_body(ct, cb, ce, ck, cf, h_ref, y_ref, rank_ref, comb_ref, fg_ref, o_ref, acc_scr):
    s = pl.program_id(0)
    flag = cf[s]
    tb, tmg = h_ref.shape[0], y_ref.shape[0]

    @pl.when((flag & PAIR_FIRST) != 0)
    def _():
        acc_scr[...] = h_ref[...]

    @pl.when((flag & PAIR_VALID) != 0)
    def _():
        lane = lax.broadcasted_iota(i32, rank_ref.shape, 1)
        mine = lane == ce[s]
        r = jnp.sum(jnp.where(mine, rank_ref[...], 0.0), axis=1, keepdims=True) - (ck[s] * tmg).astype(f32)
        gate = jnp.sum(jnp.where(mine, comb_ref[...], 0.0), axis=1, keepdims=True)
        cols = lax.broadcasted_iota(i32, (tb, tmg), 1).astype(f32)
        onehot = jnp.where(cols == r, 1.0, 0.0).astype(bf16)
        acc_scr[...] += gate * _dot(onehot, y_ref[...])

    @pl.when((flag & PAIR_LAST) != 0)
    def _():
        o_ref[...] = _rms(acc_scr[...], fg_ref[...])


def _moe(h, g_ffn, w_router, w_gate_e, w_up_e, w_down_e, final_g):
    seq = h.shape[0]
    tm = min(MOE_TILE, seq)
    xn, comb, rank, counts_after = _route(h, g_ffn, w_router)
    n_tiles, tile_expert, tile_valid, tile_major, block_major = _moe_schedule(counts_after, seq)
    n_pairs = tile_major[0].shape[0]
    rank_t = jnp.transpose(rank[:, :8])

    x_sorted = pl.pallas_call(
        _moe_gather_body,
        grid_spec=pltpu.PrefetchScalarGridSpec(
            num_scalar_prefetch=5, grid=(n_pairs,),
            in_specs=[pl.BlockSpec((tm, D_MODEL), lambda s, pt, pb, pe, pk, pf: (pb[s], 0)),
                      pl.BlockSpec((8, tm), lambda s, pt, pb, pe, pk, pf: (0, pb[s]))],
            out_specs=pl.BlockSpec((tm, D_MODEL), lambda s, pt, pb, pe, pk, pf: (pt[s], 0))),
        out_shape=jax.ShapeDtypeStruct((n_tiles * tm, D_MODEL), bf16),
        compiler_params=_cparams(("arbitrary",)),
        name="moe_gather",
    )(*tile_major, xn, rank_t)

    tf = D_FF_E // 4
    n_f = D_FF_E // tf
    f_eff = lambda f, v: f * v + (n_f - 1) * (1 - v)
    y_sorted = pl.pallas_call(
        _moe_ffn_body,
        grid_spec=pltpu.PrefetchScalarGridSpec(
            num_scalar_prefetch=2, grid=(n_tiles, n_f),
            in_specs=[pl.BlockSpec((tm, D_MODEL), lambda p, f, te, tv: (p, 0)),
                      pl.BlockSpec((1, D_MODEL, tf), lambda p, f, te, tv: (te[p], 0, f_eff(f, tv[p]))),
                      pl.BlockSpec((1, D_MODEL, tf), lambda p, f, te, tv: (te[p], 0, f_eff(f, tv[p]))),
                      pl.BlockSpec((1, tf, D_MODEL), lambda p, f, te, tv: (te[p], f_eff(f, tv[p]), 0))],
            out_specs=pl.BlockSpec((tm, D_MODEL), lambda p, f, te, tv: (p, 0)),
            scratch_shapes=[pltpu.VMEM((tm, D_MODEL), f32)]),
        out_shape=jax.ShapeDtypeStruct((n_tiles * tm, D_MODEL), bf16),
        compiler_params=_cparams(("arbitrary", "arbitrary")),
        name="moe_ffn",
    )(tile_expert, tile_valid, x_sorted, w_gate_e.astype(bf16), w_up_e.astype(bf16), w_down_e.astype(bf16))

    return pl.pallas_call(
        _moe_combine_body,
        grid_spec=pltpu.PrefetchScalarGridSpec(
            num_scalar_prefetch=5, grid=(n_pairs,),
            in_specs=[pl.BlockSpec((tm, D_MODEL), lambda s, ct, cb, ce, ck, cf: (cb[s], 0)),
                      pl.BlockSpec((tm, D_MODEL), lambda s, ct, cb, ce, ck, cf: (ct[s], 0)),
                      pl.BlockSpec((tm, LANES), lambda s, ct, cb, ce, ck, cf: (cb[s], 0)),
                      pl.BlockSpec((tm, LANES), lambda s, ct, cb, ce, ck, cf: (cb[s], 0)),
                      pl.BlockSpec((1, D_MODEL), lambda s, ct, cb, ce, ck, cf: (0, 0))],
            out_specs=pl.BlockSpec((tm, D_MODEL), lambda s, ct, cb, ce, ck, cf: (cb[s], 0)),
            scratch_shapes=[pltpu.VMEM((tm, D_MODEL), f32)]),
        out_shape=jax.ShapeDtypeStruct((seq, D_MODEL), f32),
        compiler_params=_cparams(("arbitrary",)),
        name="moe_combine",
    )(*block_major, h, y_sorted, rank, comb, final_g)


def _even_layer(h, norm_mix, w_in, g_q_lat, w_uq, g_kv_lat, w_ukv, g_idx_k, w_out, norm_ffn, w_gate, w_up, w_down):
    seq = h.shape[0]
    sizes = (Q_LORA, KV_LORA, ROPE_A, H_B * DH_B, H_B * DH_B, H_B * DH_B, H_IDX * D_IDX, D_IDX, H_IDX)
    offs = np.cumsum((0,) + sizes)
    w_cq, w_ckv, w_kr, w_qb, w_kb, w_vb, w_qi, w_ki, w_wi = [w_in[:, offs[n]:offs[n + 1]] for n in range(9)]

    scale_a = (NOPE_A + ROPE_A) ** -0.5 * LOG2E
    wq3 = (w_uq * scale_a).reshape(Q_LORA, H_A, NOPE_A + ROPE_A)
    wq = jnp.pad(wq3, ((0, 0), (0, 0), (0, HEAD_PAD_A - NOPE_A - ROPE_A))).reshape(Q_LORA, H_A * HEAD_PAD_A)
    wq_sw = _swap_cols(wq, HEAD_PAD_A, NOPE_A, ROPE_A)
    wkv3 = w_ukv.reshape(KV_LORA, H_A, NOPE_A + V_A)
    wk = jnp.pad(wkv3[:, :, :NOPE_A], ((0, 0), (0, 0), (0, HEAD_PAD_A - NOPE_A))).reshape(KV_LORA, H_A * HEAD_PAD_A)
    wv, one_a = _with_ones_column(wkv3[:, :, NOPE_A:].reshape(KV_LORA, H_A * V_A), H_A, V_A)
    w_vb, one_b = _with_ones_column(w_vb, H_B, DH_B)
    place = jnp.zeros((ROPE_A, H_A, HEAD_PAD_A), f32)
    place = place.at[:, :, NOPE_A:NOPE_A + ROPE_A].set(jnp.eye(ROPE_A, dtype=f32)[:, None, :])
    place = place.reshape(ROPE_A, H_A * HEAD_PAD_A)

    w_qb = w_qb * (DH_B ** -0.5 * LOG2E)
    w_qi = w_qi * D_IDX ** -0.5
    w_small = jnp.concatenate([w_kr, _swap_cols(w_kr, ROPE_A, 0, ROPE_A), w_ki, _swap_cols(w_ki, D_IDX, 0, ROT_IDX)], 1)
    w_wi_p = _pad_cols(w_wi * H_IDX ** -0.5, LANES)
    g_idx = g_idx_k.reshape(1, D_IDX)
    g_idx_sw = jnp.concatenate([g_idx[:, ROT_IDX // 2:ROT_IDX], g_idx[:, :ROT_IDX // 2], g_idx[:, ROT_IDX:]], 1)

    ca, sa = _rope_tables(seq, ROPE_A, HEAD_PAD_A, NOPE_A, LANES)
    cb, sb = _rope_tables(seq, ROT_B, DH_B, 0, LANES)
    ci, si = _rope_tables(seq, ROT_IDX, D_IDX, 0, LANES)
    ckr, skr = _rope_tables(seq, ROPE_A, ROPE_A, 0, ROPE_A)

    consts = [norm_mix.reshape(1, -1), g_q_lat.reshape(1, -1), g_kv_lat.reshape(1, -1), g_idx, g_idx_sw,
              one_a, one_b]
    weights = [jnp.concatenate([w_cq, w_ckv], 1), wq, wq_sw, wk, wv, place,
               w_qb, _swap_cols(w_qb, DH_B, 0, ROT_B), w_kb, _swap_cols(w_kb, DH_B, 0, ROT_B), w_vb,
               w_qi, _swap_cols(w_qi, D_IDX, 0, ROT_IDX), w_small, w_wi_p]
    weights = [w.astype(bf16) for w in weights]
    sds = lambda n, dt: jax.ShapeDtypeStruct((seq, n), dt)
    outs = [sds(H_A * HEAD_PAD_A, bf16), sds(H_A * HEAD_PAD_A, bf16), sds(H_A * _v_pad(V_A), bf16),
            sds(H_B * DH_B, bf16), sds(H_B * DH_B, bf16), sds(H_B * _v_pad(DH_B), bf16),
            sds(H_IDX * D_IDX, bf16), sds(D_IDX, bf16), sds(LANES, f32)]
    qa, ka, va, qb, kb, vb, qi, ki, wi = _rows_call(
        _even_proj_body, seq, 256, [h, ca, sa, cb, sb, ci, si, ckr, skr], consts + weights, outs, "even_proj")

    o_a = _flash(qa, ka, va, n_heads=H_A, dq=HEAD_PAD_A, dv=V_A, name="mla_attn")
    top_k = min(TOPK_MAX, seq // 4)
    bias = _dsa_select(qi, wi, ki.T, seq, top_k)
    o_b = _flash(qb, kb, vb, n_heads=H_B, dq=DH_B, dv=DH_B, bias=bias, name="dsa_attn")

    w_out = w_out.astype(bf16)
    n_a = H_A * V_A
    (h,) = _rows_call(_out_proj2_body, seq, 512, [h, o_a, o_b], [w_out[:n_a], w_out[n_a:]],
                      [jax.ShapeDtypeStruct((seq, D_MODEL), f32)], "even_out_proj")
    return _ffn(h, norm_ffn.reshape(1, -1), w_gate.astype(bf16), w_up.astype(bf16), w_down.astype(bf16),
                tm=512, tf=D_FF // 2, name="dense_ffn")


def _odd_layer(h, layer, norm_mix, w_qkv, lq1, lk1, lq2, lk2, g_sub, w_out, norm_ffn, w_router, w_gate_e, w_up_e,
               w_down_e, final_norm):
    seq = h.shape[0]
    lambda_init = 0.8 - 0.6 * math.exp(-0.3 * layer)
    n = H_C * 2 * DH_C
    w_q = w_qkv[:, :n] * (DH_C ** -0.5 * LOG2E)
    w_k = w_qkv[:, n:2 * n]
    w_v, one_v = _with_ones_column(w_qkv[:, 2 * n:], H_C, 2 * DH_C)
    cb, sb = _rope_tables(seq, ROT_C, DH_C, 0, LANES)
    weights = [w_q, _swap_cols(w_q, DH_C, 0, ROT_C), w_k, _swap_cols(w_k, DH_C, 0, ROT_C), w_v]
    weights = [w.astype(bf16) for w in weights]
    sds = jax.ShapeDtypeStruct((seq, n), bf16)
    sds_v = jax.ShapeDtypeStruct((seq, w_v.shape[1]), bf16)
    q, k, v = _rows_call(_odd_proj_body, seq, 512, [h, cb, sb], [norm_mix.reshape(1, -1), one_v] + weights,
                         [sds, sds, sds_v], "odd_proj")
    diff_params = [lq1.reshape(1, -1), lk1.reshape(1, -1), lq2.reshape(1, -1), lk2.reshape(1, -1),
                   g_sub.reshape(1, -1)]
    o = _flash(q, k, v, n_heads=2 * H_C, dq=DH_C, dv=2 * DH_C, v_group=2, diff_params=diff_params,
               lambda_init=lambda_init, name="diff_attn")
    (h,) = _rows_call(_out_proj1_body, seq, 512, [h, o], [w_out.astype(bf16)],
                      [jax.ShapeDtypeStruct((seq, D_MODEL), f32)], "odd_out_proj")
    return _moe(h, norm_ffn.reshape(1, -1), w_router, w_gate_e, w_up_e, w_down_e, final_norm.reshape(1, -1))


def kernel(x, ev_norm_mix, ev_w_in, ev_g_q_lat, ev_w_uq, ev_g_kv_lat, ev_w_ukv, ev_g_idx_k, ev_w_out, ev_norm_ffn, ev_w_gate, ev_w_up, ev_w_down, od_norm_mix, od_w_qkv, od_lambda_q1, od_lambda_k1, od_lambda_q2, od_lambda_k2, od_g_sub, od_w_out, od_norm_ffn, od_w_router, od_w_gate_e, od_w_up_e, od_w_down_e, final_norm):
    batch, seq, _ = x.shape
    assert batch == 1 and ev_w_in.shape[0] == 1 and od_w_qkv.shape[0] == 1
    h = x[0]
    h = _even_layer(h, ev_norm_mix[0], ev_w_in[0], ev_g_q_lat[0], ev_w_uq[0], ev_g_kv_lat[0], ev_w_ukv[0],
                    ev_g_idx_k[0], ev_w_out[0], ev_norm_ffn[0], ev_w_gate[0], ev_w_up[0], ev_w_down[0])
    h = _odd_layer(h, 1, od_norm_mix[0], od_w_qkv[0], od_lambda_q1[0], od_lambda_k1[0], od_lambda_q2[0],
                   od_lambda_k2[0], od_g_sub[0], od_w_out[0], od_norm_ffn[0], od_w_router[0], od_w_gate_e[0],
                   od_w_up_e[0], od_w_down_e[0], final_norm)
    return h[None]
```

```python
import functools
import math

import numpy as np
import jax
import jax.numpy as jnp
from jax import lax
from jax.experimental import pallas as pl
from jax.experimental.pallas import tpu as pltpu

f32 = jnp.float32
bf16 = jnp.bfloat16
i32 = jnp.int32

D_MODEL = 1024
CHUNK = 64
ROPE_THETA = 500000.0
NORM_EPS = 1e-6
NEG_INF = -1e30
LOG2E = math.log2(math.e)

H_A, Q_LORA, KV_LORA, NOPE_A, ROPE_A, V_A = 8, 256, 128, 64, 32, 64
H_B, DH_B, ROT_B = 8, 64, 16
H_IDX, D_IDX, ROT_IDX = 8, 32, 8
TOPK_MAX = 256
H_C, DH_C, ROT_C = 8, 64, 16
D_FF, N_EXP, D_FF_E = 2816, 8, 3584

LANES = 128
HEAD_PAD_A = 128

_NEG_BITS = int(np.float32(NEG_INF).view(np.int32))
NEG_KEY = _NEG_BITS ^ 0x7FFFFFFF
INT_MIN = -(2 ** 31)

VMEM_LIMIT = 56 * 1024 * 1024


def _cparams(sem):
    return pltpu.CompilerParams(dimension_semantics=sem, vmem_limit_bytes=VMEM_LIMIT)


def _rms(x, g):
    var = jnp.mean(x * x, axis=-1, keepdims=True)
    return x * lax.rsqrt(var + NORM_EPS) * g


def _dot(a, b):
    return jnp.dot(a, b, preferred_element_type=f32)


def _dot_nt(a, b):
    return lax.dot_general(a, b, (((1,), (1,)), ((), ())), preferred_element_type=f32)


def _rope_tables(seq, rot_dim, head_width, offset, width):
    pos = jnp.arange(seq, dtype=f32)
    inv_freq = ROPE_THETA ** (-jnp.arange(0, rot_dim, 2, dtype=f32) / rot_dim)
    ang = pos[:, None] * inv_freq[None, :]
    cos, sin = jnp.cos(ang), jnp.sin(ang)
    c = jnp.ones((seq, head_width), f32).at[:, offset:offset + rot_dim].set(jnp.concatenate([cos, cos], -1))
    s = jnp.zeros((seq, head_width), f32).at[:, offset:offset + rot_dim].set(jnp.concatenate([-sin, sin], -1))
    reps = width // head_width
    return jnp.tile(c, (1, reps)), jnp.tile(s, (1, reps))


def _swap_cols(w, head_width, offset, rot_dim):
    k, n = w.shape
    half = rot_dim // 2
    w3 = w.reshape(k, n // head_width, head_width)
    out = jnp.zeros_like(w3)
    out = out.at[:, :, offset:offset + half].set(w3[:, :, offset + half:offset + rot_dim])
    out = out.at[:, :, offset + half:offset + rot_dim].set(w3[:, :, offset:offset + half])
    return out.reshape(k, n)


def _pad_cols(w, width):
    return jnp.pad(w, ((0, 0), (0, width - w.shape[1])))


def _rows_call(body, seq, tm, row_ins, const_ins, out_sds, name):
    def rspec(a):
        return pl.BlockSpec((tm, a.shape[1]), lambda i: (i, 0))

    def cspec(a):
        nd = a.ndim
        return pl.BlockSpec(a.shape, lambda i: (0,) * nd)

    return pl.pallas_call(
        body,
        grid=(seq // tm,),
        in_specs=[rspec(a) for a in row_ins] + [cspec(a) for a in const_ins],
        out_specs=[pl.BlockSpec((tm, o.shape[1]), lambda i: (i, 0)) for o in out_sds],
        out_shape=out_sds,
        compiler_params=_cparams(("parallel",)),
        name=name,
    )(*row_ins, *const_ins)


def _even_proj_body(x_ref, ca_ref, sa_ref, cb_ref, sb_ref, ci_ref, si_ref, ckr_ref, skr_ref,
                    g_ref, gq_ref, gkv_ref, gi_ref, gisw_ref, onea_ref, oneb_ref,
                    wlat_ref, wq_ref, wqsw_ref, wk_ref, wv_ref, place_ref,
                    wqb_ref, wqbsw_ref, wkb_ref, wkbsw_ref, wvb_ref, wqi_ref, wqisw_ref, wsm_ref, wwi_ref,
                    qa_ref, ka_ref, va_ref, qb_ref, kb_ref, vb_ref, qi_ref, ki_ref, wi_ref):
    xn = _rms(x_ref[...], g_ref[...]).astype(bf16)
    lat = _dot(xn, wlat_ref[...])
    cqn = _rms(lat[:, :Q_LORA], gq_ref[...]).astype(bf16)
    ckvn = _rms(lat[:, Q_LORA:], gkv_ref[...]).astype(bf16)
    reps_a = qa_ref.shape[1] // LANES
    ca = jnp.tile(ca_ref[...], (1, reps_a))
    sa = jnp.tile(sa_ref[...], (1, reps_a))
    qa_ref[...] = (_dot(cqn, wq_ref[...]) * ca + _dot(cqn, wqsw_ref[...]) * sa).astype(bf16)
    small = _dot(xn, wsm_ref[...])
    kr, kr_sw = small[:, 0:ROPE_A], small[:, ROPE_A:2 * ROPE_A]
    kpe = (kr * ckr_ref[...] + kr_sw * skr_ref[...]).astype(bf16)
    ka_ref[...] = (_dot(ckvn, wk_ref[...]) + _dot(kpe, place_ref[...])).astype(bf16)
    va_ref[...] = (_dot(ckvn, wv_ref[...]) + onea_ref[...]).astype(bf16)
    reps_b = qb_ref.shape[1] // LANES
    cb = jnp.tile(cb_ref[...], (1, reps_b))
    sb = jnp.tile(sb_ref[...], (1, reps_b))
    qb_ref[...] = (_dot(xn, wqb_ref[...]) * cb + _dot(xn, wqbsw_ref[...]) * sb).astype(bf16)
    kb_ref[...] = (_dot(xn, wkb_ref[...]) * cb + _dot(xn, wkbsw_ref[...]) * sb).astype(bf16)
    vb_ref[...] = (_dot(xn, wvb_ref[...]) + oneb_ref[...]).astype(bf16)
    reps_i = qi_ref.shape[1] // LANES
    ci = jnp.tile(ci_ref[...], (1, reps_i))
    si = jnp.tile(si_ref[...], (1, reps_i))
    qi_ref[...] = (_dot(xn, wqi_ref[...]) * ci + _dot(xn, wqisw_ref[...]) * si).astype(bf16)
    ki, ki_sw = small[:, 2 * ROPE_A:2 * ROPE_A + D_IDX], small[:, 2 * ROPE_A + D_IDX:2 * ROPE_A + 2 * D_IDX]
    r = lax.rsqrt(jnp.mean(ki * ki, axis=-1, keepdims=True) + NORM_EPS)
    ci32, si32 = ci_ref[:, 0:D_IDX], si_ref[:, 0:D_IDX]
    ki_ref[...] = (ki * r * gi_ref[...] * ci32 + ki_sw * r * gisw_ref[...] * si32).astype(bf16)
    wi_ref[...] = _dot(xn, wwi_ref[...])


def _odd_proj_body(x_ref, cb_ref, sb_ref, g_ref, one_ref, wq_ref, wqsw_ref, wk_ref, wksw_ref, wv_ref,
                   q_ref, k_ref, v_ref):
    xn = _rms(x_ref[...], g_ref[...]).astype(bf16)
    reps = q_ref.shape[1] // LANES
    cb = jnp.tile(cb_ref[...], (1, reps))
    sb = jnp.tile(sb_ref[...], (1, reps))
    q_ref[...] = (_dot(xn, wq_ref[...]) * cb + _dot(xn, wqsw_ref[...]) * sb).astype(bf16)
    k_ref[...] = (_dot(xn, wk_ref[...]) * cb + _dot(xn, wksw_ref[...]) * sb).astype(bf16)
    v_ref[...] = (_dot(xn, wv_ref[...]) + one_ref[...]).astype(bf16)


def _router_body(x_ref, a_ref, wo_ref, g_ref, whi_ref, wlo_ref, h_ref, xn_ref, comb_ref, rank_ref, cnt_ref,
                 carry_scr):
    @pl.when(pl.program_id(0) == 0)
    def _():
        carry_scr[...] = jnp.zeros(carry_scr.shape, f32)

    h = x_ref[...] + _dot(a_ref[...], wo_ref[...])
    h_ref[...] = h
    xn = _rms(h, g_ref[...])
    xn_ref[...] = xn.astype(bf16)
    hi = xn.astype(bf16)
    lo = (xn - hi.astype(f32)).astype(bf16)
    logits = _dot(hi, whi_ref[...]) + _dot(lo, whi_ref[...]) + _dot(hi, wlo_ref[...])
    lane = lax.broadcasted_iota(i32, logits.shape, 1).astype(f32)
    lg = jnp.where(lane < N_EXP, logits, -jnp.inf)
    m1 = jnp.max(lg, axis=1, keepdims=True)
    i1 = jnp.min(jnp.where(lg == m1, lane, float(LANES)), axis=1, keepdims=True)
    lg2 = jnp.where(lane == i1, -jnp.inf, lg)
    m2 = jnp.max(lg2, axis=1, keepdims=True)
    i2 = jnp.min(jnp.where(lg2 == m2, lane, float(LANES)), axis=1, keepdims=True)
    e2 = jnp.exp(m2 - m1)
    den = 1.0 + e2
    comb_ref[...] = jnp.where(lane == i1, 1.0 / den, 0.0) + jnp.where(lane == i2, e2 / den, 0.0)
    routed = jnp.logical_or(lane == i1, lane == i2)
    onehot = jnp.where(routed, 1.0, 0.0)
    tm = onehot.shape[0]
    earlier = lax.broadcasted_iota(i32, (tm, tm), 1) < lax.broadcasted_iota(i32, (tm, tm), 0)
    before = _dot(jnp.where(earlier, 1.0, 0.0).astype(bf16), onehot.astype(bf16))
    carry = carry_scr[0:1, :]
    rank_ref[...] = jnp.where(routed, before + carry, -1.0)
    carry = carry + jnp.sum(onehot, axis=0, keepdims=True)
    carry_scr[...] = jnp.broadcast_to(carry, carry_scr.shape)
    cnt_ref[0] = jnp.broadcast_to(carry, cnt_ref.shape[1:])


SEL_COUNT_ROWS, SEL_COUNT_COLS = 64, 1024
TIE_NONE = 2 ** 30


def _key_to_float(key):
    return lax.bitcast_convert_type(jnp.where(key < 0, key ^ 0x7FFFFFFF, key), f32)


def _dsa_select_body(qi_ref, wi_ref, kit_ref, out_ref, score_scr, tie_scr, *, tq, tk, top_k, idx_bits):
    q0 = pl.program_id(0) * tq
    n_ktc = (q0 + tq + SEL_COUNT_COLS - 1) // SEL_COUNT_COLS
    n_kt = n_ktc * (SEL_COUNT_COLS // tk)
    qh = [qi_ref[:, h * D_IDX:(h + 1) * D_IDX] for h in range(H_IDX)]
    w = wi_ref[...]
    wb = [jnp.broadcast_to(w[:, h:h + 1], (tq, tk)) for h in range(H_IDX)]
    row = q0 + lax.broadcasted_iota(i32, (tq, 1), 0)
    row_lim = (row // CHUNK + 1) * CHUNK

    def cols_of(kt):
        c0 = pl.multiple_of(kt * tk, tk)
        return c0, c0 + lax.broadcasted_iota(i32, (tq, tk), 1)

    def score_tile(kt, carry):
        c0, col = cols_of(kt)
        kt_tile = kit_ref[:, pl.ds(c0, tk)]
        acc = jnp.zeros((tq, tk), f32)
        for h in range(H_IDX):
            acc = acc + jnp.maximum(_dot(qh[h], kt_tile), 0.0) * wb[h]
        score_scr[:, pl.ds(c0, tk)] = jnp.where(col < row_lim, acc, NEG_INF)
        return carry

    lax.fori_loop(0, n_kt, score_tile, 0)

    ones_mat = jnp.ones((LANES, LANES), bf16)

    groups = [slice(g * SEL_COUNT_ROWS, (g + 1) * SEL_COUNT_ROWS) for g in range(tq // SEL_COUNT_ROWS)]

    def count(scr, cand_rep, strict=False, trips=None):
        parts = []
        for g, rows in enumerate(groups):
            cand = cand_rep[rows]

            def body(kt, acc, rows=rows, cand=cand):
                c0 = pl.multiple_of(kt * SEL_COUNT_COLS, SEL_COUNT_COLS)
                ks = scr[rows, pl.ds(c0, SEL_COUNT_COLS)]
                for u in range(SEL_COUNT_COLS // LANES):
                    blk = ks[:, u * LANES:(u + 1) * LANES]
                    acc = acc + jnp.where(blk > cand if strict else blk >= cand, 1.0, 0.0)
                return acc

            n = n_ktc if trips is None else trips[g]
            parts.append(lax.fori_loop(0, n, body, jnp.zeros((SEL_COUNT_ROWS, LANES), f32)))
        return _dot(jnp.concatenate(parts, axis=0).astype(bf16), ones_mat)

    def thr_step(it, carry):
        u, c = carry
        cand_u = u | lax.shift_left(jnp.int32(1), 31 - it)
        cnt = count(score_scr, _key_to_float(cand_u ^ INT_MIN))
        keep = cnt >= top_k
        return jnp.where(keep, cand_u, u), jnp.where(keep, cnt, c)

    n_cols = jnp.full((tq, LANES), n_ktc * SEL_COUNT_COLS, i32).astype(f32)
    u_all, c_all = lax.fori_loop(0, 32, thr_step, (jnp.zeros((tq, LANES), i32), n_cols))
    thr_key_rep = u_all ^ INT_MIN
    thr_rep = _key_to_float(thr_key_rep)
    thr_vis_rep = _key_to_float(jnp.maximum(thr_key_rep, NEG_KEY + 1))
    n_ge = c_all[:, 0:1].astype(i32)
    thr, thr_vis = thr_rep[:, 0:1], thr_vis_rep[:, 0:1]
    thr_vis_t = jnp.tile(thr_vis_rep, (1, tk // LANES))
    excess = jnp.logical_and(n_ge > top_k, thr_key_rep[:, 0:1] > NEG_KEY)
    any_excess = jnp.max(excess.astype(f32)) > 0.0
    out_ref[...] = jnp.full(out_ref.shape, NEG_INF, bf16)

    @pl.when(jnp.logical_not(any_excess))
    def _():
        def write_tile(kt, carry):
            c0 = pl.multiple_of(kt * tk, tk)
            out_ref[:, pl.ds(c0, tk)] = jnp.where(score_scr[:, pl.ds(c0, tk)] >= thr_vis_t, 0.0, NEG_INF).astype(bf16)
            return carry

        lax.fori_loop(0, n_kt, write_tile, 0)

    @pl.when(any_excess)
    def _():
        excess_f = excess.astype(f32)
        trips = [jnp.where(jnp.max(excess_f[rows]) > 0.0, n_ktc, 0) for rows in groups]
        need = top_k - count(score_scr, thr_rep, strict=True, trips=trips)

        def tie_tile(kt, carry):
            c0, col = cols_of(kt)
            tie_scr[:, pl.ds(c0, tk)] = jnp.where(score_scr[:, pl.ds(c0, tk)] == thr, col, TIE_NONE)
            return carry

        lax.fori_loop(0, n_kt, tie_tile, 0)

        def tie_step(it, xv):
            cand = xv | lax.shift_left(jnp.int32(1), idx_bits - 1 - it)
            ties_below = n_cols - count(tie_scr, cand, trips=trips)
            return jnp.where(ties_below < need, cand, xv)

        x_rep = lax.fori_loop(0, idx_bits, tie_step, jnp.zeros((tq, LANES), i32))
        xlim = jnp.where(excess, x_rep[:, 0:1], TIE_NONE - 1)

        def write_tile(kt, carry):
            c0 = pl.multiple_of(kt * tk, tk)
            ks = score_scr[:, pl.ds(c0, tk)]
            sel = jnp.logical_or(ks > thr, tie_scr[:, pl.ds(c0, tk)] <= xlim)
            sel = jnp.logical_and(sel, ks >= thr_vis)
            out_ref[:, pl.ds(c0, tk)] = jnp.where(sel, 0.0, NEG_INF).astype(bf16)
            return carry

        lax.fori_loop(0, n_kt, write_tile, 0)


def _dsa_select(qi, wi, kit, seq, top_k):
    tq, tk = min(256, seq), 512
    assert seq < TIE_NONE and seq % SEL_COUNT_COLS == 0
    idx_bits = max(1, int(math.ceil(math.log2(seq))))
    body = functools.partial(_dsa_select_body, tq=tq, tk=tk, top_k=top_k, idx_bits=idx_bits)
    return pl.pallas_call(
        body,
        grid=(seq // tq,),
        in_specs=[pl.BlockSpec((tq, qi.shape[1]), lambda i: (i, 0)),
                  pl.BlockSpec((tq, wi.shape[1]), lambda i: (i, 0)),
                  pl.BlockSpec(kit.shape, lambda i: (0, 0))],
        out_specs=pl.BlockSpec((tq, seq), lambda i: (i, 0)),
        out_shape=jax.ShapeDtypeStruct((seq, seq), bf16),
        scratch_shapes=[pltpu.VMEM((tq, seq), f32), pltpu.VMEM((tq, seq), i32)],
        compiler_params=_cparams(("parallel",)),
        name="dsa_select",
    )(qi, wi, kit)


FLAG_FIRST, FLAG_MASK, FLAG_LAST = 1, 2, 4


def _flash_body(it_ref, jt_ref, ft_ref, q_ref, k_ref, v_ref, *rest, tq, tk, n_heads, dq, dv, v_group, has_bias, diff,
                lambda_init):
    rest = list(rest)
    bias_ref = rest.pop(0) if has_bias else None
    if diff:
        lq1_ref, lk1_ref, lq2_ref, lk2_ref, gsub_ref = rest[:5]
        rest = rest[5:]
    o_ref, m_scr, acc_scr = rest
    dvp = _v_pad(dv)
    step = pl.program_id(0)
    i = it_ref[step]
    j = jt_ref[step]
    flag = ft_ref[step]

    @pl.when((flag & FLAG_FIRST) != 0)
    def _():
        m_scr[...] = jnp.full(m_scr.shape, NEG_INF, f32)
        acc_scr[...] = jnp.zeros(acc_scr.shape, f32)

    def attend(bias):
        for h in range(n_heads):
            hv = h // v_group
            s = _dot_nt(q_ref[:, h * dq:(h + 1) * dq], k_ref[:, h * dq:(h + 1) * dq])
            if bias is not None:
                s = s + bias
            m_prev = m_scr[h]
            m_new = jnp.maximum(m_prev, jnp.max(s, axis=1, keepdims=True))
            alpha = jnp.exp2(m_prev - m_new)
            p = jnp.exp2(s - jnp.tile(m_new, (1, tk // LANES)))
            acc_scr[h] = (acc_scr[h] * jnp.tile(alpha, (1, dvp // LANES))
                          + _dot(p.astype(bf16), v_ref[:, hv * dvp:(hv + 1) * dvp]))
            m_scr[h] = m_new

    if has_bias:
        attend(bias_ref[...].astype(f32))
    else:
        @pl.when((flag & FLAG_MASK) == 0)
        def _():
            attend(None)

        @pl.when((flag & FLAG_MASK) != 0)
        def _():
            r = (i * tq + lax.broadcasted_iota(i32, (tq, tk), 0)) // CHUNK
            c = (j * tk + lax.broadcasted_iota(i32, (tq, tk), 1)) // CHUNK
            attend(jnp.where(c <= r, 0.0, NEG_INF))

    @pl.when((flag & FLAG_LAST) != 0)
    def _():
        if diff:
            lam = (jnp.exp(jnp.sum(lq1_ref[...] * lk1_ref[...], axis=1, keepdims=True))
                   - jnp.exp(jnp.sum(lq2_ref[...] * lk2_ref[...], axis=1, keepdims=True)) + lambda_init)
            for hc in range(n_heads // 2):
                a1, a2 = acc_scr[2 * hc], acc_scr[2 * hc + 1]
                o1 = a1[:, :dv] / a1[:, dv:dv + 1]
                o2 = a2[:, :dv] / a2[:, dv:dv + 1]
                o = _rms(o1 - lam * o2, gsub_ref[...]) * (1.0 - lambda_init)
                o_ref[:, hc * dv:(hc + 1) * dv] = o.astype(o_ref.dtype)
        else:
            for h in range(n_heads):
                a = acc_scr[h]
                o_ref[:, h * dv:(h + 1) * dv] = (a[:, :dv] / a[:, dv:dv + 1]).astype(o_ref.dtype)


FLASH_TILE_ELEMS = 1024 * 512
FLASH_STATS_BYTES = 12 * 1024 * 1024


def _v_pad(dv):
    return (dv // LANES + 1) * LANES


def _with_ones_column(w_v, n_heads, dv):
    k = w_v.shape[0]
    dvp = _v_pad(dv)
    w = jnp.pad(w_v.reshape(k, n_heads, dv), ((0, 0), (0, 0), (0, dvp - dv))).reshape(k, n_heads * dvp)
    one = jnp.zeros((1, n_heads, dvp), f32).at[:, :, dv].set(1.0).reshape(1, n_heads * dvp)
    return w, one


def _flash_tiles(seq, n_heads, dv):
    tq = 1024
    while tq > 128 and n_heads * tq * (LANES + _v_pad(dv)) * 4 > FLASH_STATS_BYTES:
        tq //= 2
    return min(tq, seq), min(FLASH_TILE_ELEMS // tq, seq)


def _flash(q, k, v, *, n_heads, dq, dv, v_group=1, bias=None, diff_params=None, lambda_init=0.0, name):
    seq = q.shape[0]
    tq, tk = _flash_tiles(seq, n_heads, dv)
    pairs = []
    for i in range(seq // tq):
        j_last = ((i + 1) * tq - 1) // tk
        for j in range(j_last + 1):
            needs_mask = (j + 1) * tk > i * tq + CHUNK
            pairs.append((i, j, (FLAG_FIRST if j == 0 else 0) | (FLAG_MASK if needs_mask else 0)
                          | (FLAG_LAST if j == j_last else 0)))
    it = jnp.asarray([p[0] for p in pairs], i32)
    jt = jnp.asarray([p[1] for p in pairs], i32)
    ft = jnp.asarray([p[2] for p in pairs], i32)
    n_out = (n_heads // v_group) * dv
    in_specs = [pl.BlockSpec((tq, q.shape[1]), lambda s, it, jt, ft: (it[s], 0)),
                pl.BlockSpec((tk, k.shape[1]), lambda s, it, jt, ft: (jt[s], 0)),
                pl.BlockSpec((tk, v.shape[1]), lambda s, it, jt, ft: (jt[s], 0))]
    args = [q, k, v]
    if bias is not None:
        in_specs.append(pl.BlockSpec((tq, tk), lambda s, it, jt, ft: (it[s], jt[s])))
        args.append(bias)
    if diff_params is not None:
        for a in diff_params:
            in_specs.append(pl.BlockSpec(a.shape, lambda s, it, jt, ft: (0, 0)))
            args.append(a)
    body = functools.partial(_flash_body, tq=tq, tk=tk, n_heads=n_heads, dq=dq, dv=dv, v_group=v_group,
                             has_bias=bias is not None, diff=diff_params is not None, lambda_init=lambda_init)
    return pl.pallas_call(
        body,
        grid_spec=pltpu.PrefetchScalarGridSpec(
            num_scalar_prefetch=3,
            grid=(len(pairs),),
            in_specs=in_specs,
            out_specs=pl.BlockSpec((tq, n_out), lambda s, it, jt, ft: (it[s], 0)),
            scratch_shapes=[pltpu.VMEM((n_heads, tq, LANES), f32), pltpu.VMEM((n_heads, tq, _v_pad(dv)), f32)]),
        out_shape=jax.ShapeDtypeStruct((seq, n_out), bf16),
        compiler_params=_cparams(("arbitrary",)),
        name=name,
    )(it, jt, ft, *args)


def _silu_mul(gate, up):
    return gate / (1.0 + jnp.exp(-gate)) * up


def _ffn_body(x_ref, a1_ref, a2_ref, w1_ref, w2_ref, g_ref, wg_ref, wu_ref, wd_ref, o_ref, h_scr, xn_scr, acc_scr):
    f = pl.program_id(1)

    @pl.when(f == 0)
    def _():
        h = x_ref[...] + _dot(a1_ref[...], w1_ref[...]) + _dot(a2_ref[...], w2_ref[...])
        h_scr[...] = h
        xn_scr[...] = _rms(h, g_ref[...]).astype(bf16)
        acc_scr[...] = jnp.zeros(acc_scr.shape, f32)

    xn = xn_scr[...]
    act = _silu_mul(_dot(xn, wg_ref[...]), _dot(xn, wu_ref[...])).astype(bf16)
    acc_scr[...] += _dot(act, wd_ref[...])

    @pl.when(f == pl.num_programs(1) - 1)
    def _():
        o_ref[...] = h_scr[...] + acc_scr[...]


def _ffn(x, a1, a2, w1, w2, g, wg, wu, wd, *, tm, tf, name):
    seq = x.shape[0]
    dff = wg.shape[1]
    return pl.pallas_call(
        _ffn_body,
        grid=(seq // tm, dff // tf),
        in_specs=[pl.BlockSpec((tm, D_MODEL), lambda i, f: (i, 0)),
                  pl.BlockSpec((tm, a1.shape[1]), lambda i, f: (i, 0)),
                  pl.BlockSpec((tm, a2.shape[1]), lambda i, f: (i, 0)),
                  pl.BlockSpec(w1.shape, lambda i, f: (0, 0)),
                  pl.BlockSpec(w2.shape, lambda i, f: (0, 0)),
                  pl.BlockSpec((1, D_MODEL), lambda i, f: (0, 0)),
                  pl.BlockSpec((D_MODEL, tf), lambda i, f: (0, f)),
                  pl.BlockSpec((D_MODEL, tf), lambda i, f: (0, f)),
                  pl.BlockSpec((tf, D_MODEL), lambda i, f: (f, 0))],
        out_specs=pl.BlockSpec((tm, D_MODEL), lambda i, f: (i, 0)),
        out_shape=jax.ShapeDtypeStruct((seq, D_MODEL), f32),
        scratch_shapes=[pltpu.VMEM((tm, D_MODEL), f32), pltpu.VMEM((tm, D_MODEL), bf16),
                        pltpu.VMEM((tm, D_MODEL), f32)],
        compiler_params=_cparams(("parallel", "arbitrary")),
        name=name,
    )(x, a1, a2, w1, w2, g, wg, wu, wd)


MOE_TILE = 512
PAIR_FIRST, PAIR_VALID, PAIR_LAST = 1, 2, 4


def _route(x, a, w_out, g, w_router):
    seq = x.shape[0]
    tm = min(MOE_TILE, seq)
    nb = seq // tm
    w_r = _pad_cols(w_router, LANES)
    w_hi = w_r.astype(bf16)
    w_lo = (w_r - w_hi.astype(f32)).astype(bf16)
    row = lambda n: pl.BlockSpec((tm, n), lambda i: (i, 0))
    const = lambda a: pl.BlockSpec(a.shape, lambda i: (0, 0))
    return pl.pallas_call(
        _router_body,
        grid=(nb,),
        in_specs=[row(D_MODEL), row(a.shape[1]), const(w_out), const(g), const(w_hi), const(w_lo)],
        out_specs=[row(D_MODEL), row(D_MODEL), row(LANES), row(LANES),
                   pl.BlockSpec((1, 8, LANES), lambda i: (i, 0, 0))],
        out_shape=[jax.ShapeDtypeStruct((seq, D_MODEL), f32), jax.ShapeDtypeStruct((seq, D_MODEL), bf16),
                   jax.ShapeDtypeStruct((seq, LANES), f32), jax.ShapeDtypeStruct((seq, LANES), f32),
                   jax.ShapeDtypeStruct((nb, 8, LANES), f32)],
        scratch_shapes=[pltpu.VMEM((8, LANES), f32)],
        compiler_params=_cparams(("arbitrary",)),
        name="router",
    )(x, a, w_out, g, w_hi, w_lo)


def _moe_schedule(counts_after, seq):
    tm = min(MOE_TILE, seq)
    nb = seq // tm
    n_tiles = 2 * nb + N_EXP
    kmax = nb + 1
    max_pairs = n_tiles + N_EXP * nb
    bounds = jnp.concatenate([jnp.zeros((1, N_EXP), i32), counts_after[:, 0, :N_EXP].astype(i32)], 0)
    cnt = bounds[-1]
    ntile = (cnt + tm - 1) // tm
    tile_end = jnp.cumsum(ntile)
    tile_start = tile_end - ntile
    n_valid = tile_end[-1]
    p_ids = jnp.minimum(jnp.arange(n_tiles, dtype=i32), n_valid - 1)
    tile_expert = jnp.minimum(jnp.sum(p_ids[:, None] >= tile_end[None, :], axis=1), N_EXP - 1).astype(i32)
    tile_valid = (jnp.arange(n_tiles, dtype=i32) < n_valid).astype(i32)

    lo = jnp.transpose(bounds[:-1])[:, None, :]
    hi = jnp.transpose(bounds[1:])[:, None, :]
    k0 = (jnp.arange(kmax, dtype=i32) * tm)[None, :, None]
    meet = jnp.maximum(k0, lo) < jnp.minimum(k0 + tm, hi)
    n_pairs = jnp.sum(meet)
    s_ids = jnp.arange(max_pairs, dtype=i32)
    s_eff = jnp.minimum(s_ids, n_pairs - 1)

    def pair_list(flat, decode, group_of):
        idx = jnp.nonzero(flat, size=max_pairs, fill_value=0)[0].astype(i32)[s_eff]
        e, k, b = decode(idx)
        p = tile_start[e] + k
        grp = group_of(p, b)
        valid = s_ids < n_pairs
        first = jnp.concatenate([jnp.ones((1,), bool), grp[1:] != grp[:-1]])
        last = jnp.concatenate([grp[1:] != grp[:-1], jnp.ones((1,), bool)]) | (s_ids == n_pairs - 1)
        flags = jnp.where(valid, PAIR_VALID + PAIR_FIRST * first + PAIR_LAST * last, 0).astype(i32)
        return p.astype(i32), b.astype(i32), e.astype(i32), k.astype(i32), flags

    pt, pb, pe, pk, pf = pair_list(meet.reshape(-1),
                                   lambda i: (i // (kmax * nb), (i // nb) % kmax, i % nb), lambda p, b: p)
    pad_tile = n_valid + (s_ids - n_pairs)
    pt = jnp.where(s_ids < n_pairs, pt, jnp.minimum(pad_tile, n_tiles - 1)).astype(i32)
    pf = jnp.where(jnp.logical_and(s_ids >= n_pairs, pad_tile < n_tiles), PAIR_FIRST, pf).astype(i32)
    tile_major = (pt, pb, pe, pk, pf)
    block_major = pair_list(jnp.transpose(meet, (2, 0, 1)).reshape(-1),
                            lambda i: ((i // kmax) % N_EXP, i % kmax, i // (N_EXP * kmax)), lambda p, b: b)
    return n_tiles, tile_expert, tile_valid, tile_major, block_major


def _moe_gather_body(pt, pb, pe, pk, pf, xn_ref, rank_t_ref, o_ref):
    s = pl.program_id(0)
    flag = pf[s]
    tmg, tb = o_ref.shape[0], xn_ref.shape[0]

    @pl.when((flag & PAIR_FIRST) != 0)
    def _():
        o_ref[...] = jnp.zeros(o_ref.shape, o_ref.dtype)

    @pl.when((flag & PAIR_VALID) != 0)
    def _():
        r = rank_t_ref[pl.ds(pe[s], 1), :] - (pk[s] * tmg).astype(f32)
        rows = lax.broadcasted_iota(i32, (tmg, tb), 0).astype(f32)
        onehot = jnp.where(rows == r, 1.0, 0.0).astype(bf16)
        o_ref[...] = o_ref[...] + _dot(onehot, xn_ref[...]).astype(o_ref.dtype)


def _moe_ffn_body(te, tv, x_ref, wg_ref, wu_ref, wd_ref, y_ref, acc_scr):
    p = pl.program_id(0)
    f = pl.program_id(1)
    last_f = f == pl.num_programs(1) - 1

    @pl.when(tv[p] != 0)
    def _():
        @pl.when(f == 0)
        def _():
            acc_scr[...] = jnp.zeros(acc_scr.shape, f32)

        x = x_ref[...]
        act = _silu_mul(_dot(x, wg_ref[0]), _dot(x, wu_ref[0])).astype(bf16)
        acc_scr[...] += _dot(act, wd_ref[0])

        @pl.when(last_f)
        def _():
            y_ref[...] = acc_scr[...].astype(y_ref.dtype)

    @pl.when(jnp.logical_and(tv[p] == 0, last_f))
    def _():
        y_ref[...] = jnp.zeros(y_ref.shape, y_ref.dtype)


def _moe_combine_body(ct, cb, ce, ck, cf, h_ref, y_ref, rank_ref, comb_ref, fg_ref, o_ref, acc_scr):
    s = pl.program_id(0)
    flag = cf[s]
    tb, tmg = h_ref.shape[0], y_ref.shape[0]

    @pl.when((flag & PAIR_FIRST) != 0)
    def _():
        acc_scr[...] = h_ref[...]

    @pl.when((flag & PAIR_VALID) != 0)
    def _():
        lane = lax.broadcasted_iota(i32, rank_ref.shape, 1)
        mine = lane == ce[s]
        r = jnp.sum(jnp.where(mine, rank_ref[...], 0.0), axis=1, keepdims=True) - (ck[s] * tmg).astype(f32)
        gate = jnp.sum(jnp.where(mine, comb_ref[...], 0.0), axis=1, keepdims=True)
        cols = lax.broadcasted_iota(i32, (tb, tmg), 1).astype(f32)
        onehot = jnp.where(cols == r, 1.0, 0.0).astype(bf16)
        acc_scr[...] += gate * _dot(onehot, y_ref[...])

    @pl.when((flag & PAIR_LAST) != 0)
    def _():
        o_ref[...] = _rms(acc_scr[...], fg_ref[...])


def _moe(x, a, w_out, g_ffn, w_router, w_gate_e, w_up_e, w_down_e, final_g):
    seq = x.shape[0]
    tm = min(MOE_TILE, seq)
    h, xn, comb, rank, counts_after = _route(x, a, w_out, g_ffn, w_router)
    n_tiles, tile_expert, tile_valid, tile_major, block_major = _moe_schedule(counts_after, seq)
    n_pairs = tile_major[0].shape[0]
    rank_t = jnp.transpose(rank[:, :8])

    x_sorted = pl.pallas_call(
        _moe_gather_body,
        grid_spec=pltpu.PrefetchScalarGridSpec(
            num_scalar_prefetch=5, grid=(n_pairs,),
            in_specs=[pl.BlockSpec((tm, D_MODEL), lambda s, pt, pb, pe, pk, pf: (pb[s], 0)),
                      pl.BlockSpec((8, tm), lambda s, pt, pb, pe, pk, pf: (0, pb[s]))],
            out_specs=pl.BlockSpec((tm, D_MODEL), lambda s, pt, pb, pe, pk, pf: (pt[s], 0))),
        out_shape=jax.ShapeDtypeStruct((n_tiles * tm, D_MODEL), bf16),
        compiler_params=_cparams(("arbitrary",)),
        name="moe_gather",
    )(*tile_major, xn, rank_t)

    tf = D_FF_E // 2
    n_f = D_FF_E // tf
    f_eff = lambda f, v: f * v + (n_f - 1) * (1 - v)
    y_sorted = pl.pallas_call(
        _moe_ffn_body,
        grid_spec=pltpu.PrefetchScalarGridSpec(
            num_scalar_prefetch=2, grid=(n_tiles, n_f),
            in_specs=[pl.BlockSpec((tm, D_MODEL), lambda p, f, te, tv: (p, 0)),
                      pl.BlockSpec((1, D_MODEL, tf), lambda p, f, te, tv: (te[p], 0, f_eff(f, tv[p]))),
                      pl.BlockSpec((1, D_MODEL, tf), lambda p, f, te, tv: (te[p], 0, f_eff(f, tv[p]))),
                      pl.BlockSpec((1, tf, D_MODEL), lambda p, f, te, tv: (te[p], f_eff(f, tv[p]), 0))],
            out_specs=pl.BlockSpec((tm, D_MODEL), lambda p, f, te, tv: (p, 0)),
            scratch_shapes=[pltpu.VMEM((tm, D_MODEL), f32)]),
        out_shape=jax.ShapeDtypeStruct((n_tiles * tm, D_MODEL), bf16),
        compiler_params=_cparams(("arbitrary", "arbitrary")),
        name="moe_ffn",
    )(tile_expert, tile_valid, x_sorted, w_gate_e.astype(bf16), w_up_e.astype(bf16), w_down_e.astype(bf16))

    return pl.pallas_call(
        _moe_combine_body,
        grid_spec=pltpu.PrefetchScalarGridSpec(
            num_scalar_prefetch=5, grid=(n_pairs,),
            in_specs=[pl.BlockSpec((tm, D_MODEL), lambda s, ct, cb, ce, ck, cf: (cb[s], 0)),
                      pl.BlockSpec((tm, D_MODEL), lambda s, ct, cb, ce, ck, cf: (ct[s], 0)),
                      pl.BlockSpec((tm, LANES), lambda s, ct, cb, ce, ck, cf: (cb[s], 0)),
                      pl.BlockSpec((tm, LANES), lambda s, ct, cb, ce, ck, cf: (cb[s], 0)),
                      pl.BlockSpec((1, D_MODEL), lambda s, ct, cb, ce, ck, cf: (0, 0))],
            out_specs=pl.BlockSpec((tm, D_MODEL), lambda s, ct, cb, ce, ck, cf: (cb[s], 0)),
            scratch_shapes=[pltpu.VMEM((tm, D_MODEL), f32)]),
        out_shape=jax.ShapeDtypeStruct((seq, D_MODEL), f32),
        compiler_params=_cparams(("arbitrary",)),
        name="moe_combine",
    )(*block_major, h, y_sorted, rank, comb, final_g)


def _even_layer(h, norm_mix, w_in, g_q_lat, w_uq, g_kv_lat, w_ukv, g_idx_k, w_out, norm_ffn, w_gate, w_up, w_down):
    seq = h.shape[0]
    sizes = (Q_LORA, KV_LORA, ROPE_A, H_B * DH_B, H_B * DH_B, H_B * DH_B, H_IDX * D_IDX, D_IDX, H_IDX)
    offs = np.cumsum((0,) + sizes)
    w_cq, w_ckv, w_kr, w_qb, w_kb, w_vb, w_qi, w_ki, w_wi = [w_in[:, offs[n]:offs[n + 1]] for n in range(9)]

    scale_a = (NOPE_A + ROPE_A) ** -0.5 * LOG2E
    wq3 = (w_uq * scale_a).reshape(Q_LORA, H_A, NOPE_A + ROPE_A)
    wq = jnp.pad(wq3, ((0, 0), (0, 0), (0, HEAD_PAD_A - NOPE_A - ROPE_A))).reshape(Q_LORA, H_A * HEAD_PAD_A)
    wq_sw = _swap_cols(wq, HEAD_PAD_A, NOPE_A, ROPE_A)
    wkv3 = w_ukv.reshape(KV_LORA, H_A, NOPE_A + V_A)
    wk = jnp.pad(wkv3[:, :, :NOPE_A], ((0, 0), (0, 0), (0, HEAD_PAD_A - NOPE_A))).reshape(KV_LORA, H_A * HEAD_PAD_A)
    wv, one_a = _with_ones_column(wkv3[:, :, NOPE_A:].reshape(KV_LORA, H_A * V_A), H_A, V_A)
    w_vb, one_b = _with_ones_column(w_vb, H_B, DH_B)
    place = jnp.zeros((ROPE_A, H_A, HEAD_PAD_A), f32)
    place = place.at[:, :, NOPE_A:NOPE_A + ROPE_A].set(jnp.eye(ROPE_A, dtype=f32)[:, None, :])
    place = place.reshape(ROPE_A, H_A * HEAD_PAD_A)

    w_qb = w_qb * (DH_B ** -0.5 * LOG2E)
    w_qi = w_qi * D_IDX ** -0.5
    w_small = jnp.concatenate([w_kr, _swap_cols(w_kr, ROPE_A, 0, ROPE_A), w_ki, _swap_cols(w_ki, D_IDX, 0, ROT_IDX)], 1)
    w_wi_p = _pad_cols(w_wi * H_IDX ** -0.5, LANES)
    g_idx = g_idx_k.reshape(1, D_IDX)
    g_idx_sw = jnp.concatenate([g_idx[:, ROT_IDX // 2:ROT_IDX], g_idx[:, :ROT_IDX // 2], g_idx[:, ROT_IDX:]], 1)

    ca, sa = _rope_tables(seq, ROPE_A, HEAD_PAD_A, NOPE_A, LANES)
    cb, sb = _rope_tables(seq, ROT_B, DH_B, 0, LANES)
    ci, si = _rope_tables(seq, ROT_IDX, D_IDX, 0, LANES)
    ckr, skr = _rope_tables(seq, ROPE_A, ROPE_A, 0, ROPE_A)

    consts = [norm_mix.reshape(1, -1), g_q_lat.reshape(1, -1), g_kv_lat.reshape(1, -1), g_idx, g_idx_sw,
              one_a, one_b]
    weights = [jnp.concatenate([w_cq, w_ckv], 1), wq, wq_sw, wk, wv, place,
               w_qb, _swap_cols(w_qb, DH_B, 0, ROT_B), w_kb, _swap_cols(w_kb, DH_B, 0, ROT_B), w_vb,
               w_qi, _swap_cols(w_qi, D_IDX, 0, ROT_IDX), w_small, w_wi_p]
    weights = [w.astype(bf16) for w in weights]
    sds = lambda n, dt: jax.ShapeDtypeStruct((seq, n), dt)
    outs = [sds(H_A * HEAD_PAD_A, bf16), sds(H_A * HEAD_PAD_A, bf16), sds(H_A * _v_pad(V_A), bf16),
            sds(H_B * DH_B, bf16), sds(H_B * DH_B, bf16), sds(H_B * _v_pad(DH_B), bf16),
            sds(H_IDX * D_IDX, bf16), sds(D_IDX, bf16), sds(LANES, f32)]
    qa, ka, va, qb, kb, vb, qi, ki, wi = _rows_call(
        _even_proj_body, seq, 256, [h, ca, sa, cb, sb, ci, si, ckr, skr], consts + weights, outs, "even_proj")

    o_a = _flash(qa, ka, va, n_heads=H_A, dq=HEAD_PAD_A, dv=V_A, name="mla_attn")
    top_k = min(TOPK_MAX, seq // 4)
    bias = _dsa_select(qi, wi, ki.T, seq, top_k)
    o_b = _flash(qb, kb, vb, n_heads=H_B, dq=DH_B, dv=DH_B, bias=bias, name="dsa_attn")

    w_out = w_out.astype(bf16)
    n_a = H_A * V_A
    return _ffn(h, o_a, o_b, w_out[:n_a], w_out[n_a:], norm_ffn.reshape(1, -1), w_gate.astype(bf16),
                w_up.astype(bf16), w_down.astype(bf16), tm=512, tf=D_FF // 2, name="dense_ffn")


def _odd_layer(h, layer, norm_mix, w_qkv, lq1, lk1, lq2, lk2, g_sub, w_out, norm_ffn, w_router, w_gate_e, w_up_e,
               w_down_e, final_norm):
    seq = h.shape[0]
    lambda_init = 0.8 - 0.6 * math.exp(-0.3 * layer)
    n = H_C * 2 * DH_C
    w_q = w_qkv[:, :n] * (DH_C ** -0.5 * LOG2E)
    w_k = w_qkv[:, n:2 * n]
    w_v, one_v = _with_ones_column(w_qkv[:, 2 * n:], H_C, 2 * DH_C)
    cb, sb = _rope_tables(seq, ROT_C, DH_C, 0, LANES)
    weights = [w_q, _swap_cols(w_q, DH_C, 0, ROT_C), w_k, _swap_cols(w_k, DH_C, 0, ROT_C), w_v]
    weights = [w.astype(bf16) for w in weights]
    sds = jax.ShapeDtypeStruct((seq, n), bf16)
    sds_v = jax.ShapeDtypeStruct((seq, w_v.shape[1]), bf16)
    q, k, v = _rows_call(_odd_proj_body, seq, 512, [h, cb, sb], [norm_mix.reshape(1, -1), one_v] + weights,
                         [sds, sds, sds_v], "odd_proj")
    diff_params = [lq1.reshape(1, -1), lk1.reshape(1, -1), lq2.reshape(1, -1), lk2.reshape(1, -1),
                   g_sub.reshape(1, -1)]
    o = _flash(q, k, v, n_heads=2 * H_C, dq=DH_C, dv=2 * DH_C, v_group=2, diff_params=diff_params,
               lambda_init=lambda_init, name="diff_attn")
    return _moe(h, o, w_out.astype(bf16), norm_ffn.reshape(1, -1), w_router, w_gate_e, w_up_e, w_down_e,
                final_norm.reshape(1, -1))


def kernel(x, ev_norm_mix, ev_w_in, ev_g_q_lat, ev_w_uq, ev_g_kv_lat, ev_w_ukv, ev_g_idx_k, ev_w_out, ev_norm_ffn, ev_w_gate, ev_w_up, ev_w_down, od_norm_mix, od_w_qkv, od_lambda_q1, od_lambda_k1, od_lambda_q2, od_lambda_k2, od_g_sub, od_w_out, od_norm_ffn, od_w_router, od_w_gate_e, od_w_up_e, od_w_down_e, final_norm):
    batch, seq, _ = x.shape
    assert batch == 1 and ev_w_in.shape[0] == 1 and od_w_qkv.shape[0] == 1
    h = x[0]
    h = _even_layer(h, ev_norm_mix[0], ev_w_in[0], ev_g_q_lat[0], ev_w_uq[0], ev_g_kv_lat[0], ev_w_ukv[0],
                    ev_g_idx_k[0], ev_w_out[0], ev_norm_ffn[0], ev_w_gate[0], ev_w_up[0], ev_w_down[0])
    h = _odd_layer(h, 1, od_norm_mix[0], od_w_qkv[0], od_lambda_q1[0], od_lambda_k1[0], od_lambda_q2[0],
                   od_lambda_k2[0], od_g_sub[0], od_w_out[0], od_norm_ffn[0], od_w_router[0], od_w_gate_e[0],
                   od_w_up_e[0], od_w_down_e[0], final_norm)
    return h[None]
```
